```python
import math
import jax, jax.numpy as jnp
from jax import lax
import numpy as np

D_MODEL = 1024
BATCH = 4
SEQ = 4096
DEPTH = 2

F32 = jnp.float32
EPS = 1e-6
MEM_LEN = 256
MAX_POS_OFFSET = 1024
SSM_WIDTH = 512
SSM_GROUP = 16
SSM_GROUPS = SSM_WIDTH // SSM_GROUP
SSM_STATE = 64
DT_MIN = 1e-3
DT_MAX = 1e-1
ATT_HEAD_DIM = 64
ATT_SLOTS = 4
DIL_PAIRS = ((128, 1), (512, 4), (2048, 16))
N_DIL = len(DIL_PAIRS)
ATT_HEADS = ATT_SLOTS * N_DIL
ATT_WIDTH = ATT_HEADS * ATT_HEAD_DIM
ATT_OUT = ATT_SLOTS * ATT_HEAD_DIM
BLOCK = 128
ROPE_THETA = 10000.0
X_HEADS = 4
X_HEAD_DIM = 128
X_WIDTH = X_HEADS * X_HEAD_DIM
N_BRANCH = 3
IN_COLS = SSM_WIDTH + 3 * ATT_WIDTH + X_WIDTH + N_BRANCH * D_MODEL
D_FF = 2816
N_EXPERTS = 8
TOP_K = 2
D_FF_EXPERT = 3584
N_DENSE = (DEPTH + 1) // 2
N_MOE = DEPTH // 2

kernel_name = 'hybrid_s5_dilated_memory_moe_block'


def rmsnorm(x, g):
    xf = x.astype(F32)
    y = xf * lax.rsqrt(jnp.mean(xf * xf, axis=-1, keepdims=True) + EPS)
    return (y * g.astype(F32)).astype(x.dtype)


def rope(x, pos):
    half = x.shape[-1] // 2
    inv = ROPE_THETA ** (-jnp.arange(half, dtype=F32) / half)
    ang = pos.astype(F32)[:, None, :, None] * inv
    cos, sin = jnp.cos(ang), jnp.sin(ang)
    xf = x.astype(F32)
    x1, x2 = xf[..., :half], xf[..., half:]
    return jnp.concatenate([x1 * cos - x2 * sin, x1 * sin + x2 * cos], axis=-1).astype(x.dtype)


def ssm_branch(u, a_re, a_im, log_dt, b_re, b_im, c_re, c_im, d_skip, w_glu):
    bsz, L, _ = u.shape
    uf = u.astype(F32).reshape(bsz, L, SSM_GROUPS, SSM_GROUP)
    lam = lax.complex(a_re.astype(F32), a_im.astype(F32))
    dt = jnp.exp(log_dt.astype(F32))[:, None]
    lam_bar = jnp.exp(lam * dt)
    b = lax.complex(b_re.astype(F32), b_im.astype(F32))
    b_bar = ((lam_bar - 1.0) / lam)[..., None] * b
    bu = jnp.einsum('blgc,gpc->blgp', uf.astype(jnp.complex64), b_bar)
    a_el = jnp.broadcast_to(lam_bar, bu.shape)

    def combine(left, right):
        a1, s1 = left
        a2, s2 = right
        return a2 * a1, a2 * s1 + s2

    _, states = lax.associative_scan(combine, (a_el, bu), axis=1)
    c = lax.complex(c_re.astype(F32), c_im.astype(F32))
    y = jnp.einsum('blgp,gcp->blgc', states, c).real \
        + d_skip.astype(F32).reshape(SSM_GROUPS, SSM_GROUP) * uf
    y = jax.nn.gelu(y.reshape(bsz, L, SSM_WIDTH))
    ga = y @ w_glu.astype(F32)
    out = ga[..., :SSM_WIDTH] * jax.nn.sigmoid(ga[..., SSM_WIDTH:])
    return out.astype(u.dtype)


def dilated_group(q, k, v, dilation, span):
    bsz, H, L, hd = q.shape
    n = L // dilation
    nb = -(-n // BLOCK)
    npad = nb * BLOCK

    def stride(t):
        t = t.reshape(bsz, H, n, dilation, hd).transpose(0, 1, 3, 2, 4)
        return jnp.pad(t, ((0, 0), (0, 0), (0, 0), (0, npad - n), (0, 0)))

    def band(t):
        tb = t.reshape(bsz, H, dilation, nb, BLOCK, hd)
        prev = jnp.pad(tb, ((0, 0), (0, 0), (0, 0), (1, 0), (0, 0), (0, 0)))[:, :, :, :nb]
        return jnp.concatenate([prev, tb], axis=4)

    qb = stride(q).reshape(bsz, H, dilation, nb, BLOCK, hd)
    kb = band(stride(k))
    vb = band(stride(v))
    s = jnp.einsum('bhrnqd,bhrnkd->bhrnqk', qb, kb).astype(F32) * (hd ** -0.5)
    qi = jnp.arange(BLOCK)[:, None] + BLOCK
    ki = jnp.arange(2 * BLOCK)[None, :]
    off = qi - ki
    blk = jnp.arange(nb)[:, None, None]
    valid = (off >= 0) & (off <= span) & (blk * BLOCK + ki - BLOCK >= 0)
    s = jnp.where(valid, s, -1e30)
    m = jnp.max(s, axis=-1, keepdims=True)
    p = jnp.exp(s - m)
    den = jnp.sum(p, axis=-1, keepdims=True)
    o = jnp.einsum('bhrnqk,bhrnkd->bhrnqd', p, vb.astype(F32)) / den
    lse = (m + jnp.log(den))[..., 0]
    o = o.reshape(bsz, H, dilation, npad, hd)[:, :, :, :n].transpose(0, 1, 3, 2, 4).reshape(bsz, H, L, hd)
    lse = lse.reshape(bsz, H, dilation, npad)[:, :, :, :n].transpose(0, 1, 3, 2).reshape(bsz, H, L)
    return o, lse


def dilated_attention(q, k, v, pos, g_q, g_k):
    bsz, L, _ = q.shape

    def heads(t):
        return t.reshape(bsz, L, ATT_HEADS, ATT_HEAD_DIM).transpose(0, 2, 1, 3)

    qh = rope(rmsnorm(heads(q), g_q), pos)
    kh = rope(rmsnorm(heads(k), g_k), pos)
    vh = heads(v)
    outs, lses = [], []
    for gi, (window, dil) in enumerate(DIL_PAIRS):
        sl = slice(gi * ATT_SLOTS, (gi + 1) * ATT_SLOTS)
        o, lse = dilated_group(qh[:, sl], kh[:, sl], vh[:, sl], dil, window // dil)
        outs.append(o)
        lses.append(lse)
    alpha = jax.nn.softmax(jnp.stack(lses), axis=0)
    out = jnp.sum(alpha[..., None] * jnp.stack(outs), axis=0)
    return out.transpose(0, 2, 1, 3).reshape(bsz, L, ATT_OUT).astype(q.dtype)


def memory_attention(q, mem_n, w_kv, g_q, g_k):
    bsz, L, _ = q.shape
    M = mem_n.shape[1]
    kv = mem_n @ w_kv
    k = kv[..., :X_WIDTH].reshape(bsz, M, X_HEADS, X_HEAD_DIM).transpose(0, 2, 1, 3)
    v = kv[..., X_WIDTH:].reshape(bsz, M, X_HEADS, X_HEAD_DIM).transpose(0, 2, 1, 3)
    qh = q.reshape(bsz, L, X_HEADS, X_HEAD_DIM).transpose(0, 2, 1, 3)
    qh = rmsnorm(qh, g_q)
    k = rmsnorm(k, g_k)
    s = jnp.einsum('bhld,bhmd->bhlm', qh, k).astype(F32) * (X_HEAD_DIM ** -0.5)
    p = jax.nn.softmax(s, axis=-1)
    o = jnp.einsum('bhlm,bhmd->bhld', p, v.astype(F32))
    return o.transpose(0, 2, 1, 3).reshape(bsz, L, X_WIDTH).astype(q.dtype)


def swiglu(h, w_gate, w_up, w_down):
    return (jax.nn.silu(h @ w_gate) * (h @ w_up)) @ w_down


def moe_swiglu(h, w_router, w_gate, w_up, w_down):
    bsz, L, D = h.shape
    t = h.reshape(bsz * L, D)
    logits = (t @ w_router).astype(F32)
    top_v, top_i = lax.top_k(logits, TOP_K)
    wts = jax.nn.softmax(top_v, axis=-1)
    gate = jnp.sum(jax.nn.one_hot(top_i, N_EXPERTS, dtype=F32) * wts[..., None], axis=1)
    out = jnp.zeros((bsz * L, D), F32)
    for e in range(N_EXPERTS):
        ye = swiglu(t, w_gate[e], w_up[e], w_down[e]).astype(F32)
        out = out + gate[:, e:e + 1] * ye
    return out.reshape(bsz, L, D).astype(h.dtype)


def setup_inputs(seed: int = 0) -> dict:
    key = jax.random.key(seed)
    ks = iter(jax.random.split(key, 48))

    def nrm(shape, scale):
        return jax.random.normal(next(ks), shape, F32) * scale

    def gain(shape):
        return 1.0 + 0.02 * jax.random.normal(next(ks), shape, F32)

    x = nrm((BATCH, SEQ, D_MODEL), 1.0)
    mem = nrm((BATCH, MEM_LEN, D_MODEL), 1.0)
    positions = jnp.arange(SEQ, dtype=jnp.int32)[None, :] + jax.random.randint(
        next(ks), (BATCH, 1), 0, MAX_POS_OFFSET, dtype=jnp.int32)
    norm_mix = gain((DEPTH, D_MODEL))
    w_in = nrm((DEPTH, D_MODEL, IN_COLS), D_MODEL ** -0.5)
    n_idx = jnp.arange(SSM_STATE, dtype=F32)
    ssm_a_re = -0.5 + 0.01 * jax.random.normal(next(ks), (DEPTH, SSM_GROUPS, SSM_STATE), F32)
    ssm_a_im = math.pi * n_idx + 0.01 * jax.random.normal(next(ks), (DEPTH, SSM_GROUPS, SSM_STATE), F32)
    ssm_log_dt = jax.random.uniform(next(ks), (DEPTH, SSM_GROUPS), F32,
                                    math.log(DT_MIN), math.log(DT_MAX))
    b_scale = (2 * SSM_GROUP) ** -0.5
    c_scale = (2 * SSM_STATE) ** -0.5
    ssm_b_re = nrm((DEPTH, SSM_GROUPS, SSM_STATE, SSM_GROUP), b_scale)
    ssm_b_im = nrm((DEPTH, SSM_GROUPS, SSM_STATE, SSM_GROUP), b_scale)
    ssm_c_re = nrm((DEPTH, SSM_GROUPS, SSM_GROUP, SSM_STATE), c_scale)
    ssm_c_im = nrm((DEPTH, SSM_GROUPS, SSM_GROUP, SSM_STATE), c_scale)
    ssm_d = nrm((DEPTH, SSM_WIDTH), 1.0)
    ssm_w_glu = nrm((DEPTH, SSM_WIDTH, 2 * SSM_WIDTH), SSM_WIDTH ** -0.5)
    w_ssm_out = nrm((DEPTH, SSM_WIDTH, D_MODEL), SSM_WIDTH ** -0.5)
    att_q_norm = gain((DEPTH, ATT_HEAD_DIM))
    att_k_norm = gain((DEPTH, ATT_HEAD_DIM))
    w_att_out = nrm((DEPTH, ATT_OUT, D_MODEL), ATT_OUT ** -0.5)
    norm_mem = gain((DEPTH, D_MODEL))
    w_mem_kv = nrm((DEPTH, D_MODEL, 2 * X_WIDTH), D_MODEL ** -0.5)
    mem_q_norm = gain((DEPTH, X_HEAD_DIM))
    mem_k_norm = gain((DEPTH, X_HEAD_DIM))
    w_mem_out = nrm((DEPTH, X_WIDTH, D_MODEL), X_WIDTH ** -0.5)
    w_o = nrm((DEPTH, D_MODEL, D_MODEL), D_MODEL ** -0.5)
    norm_ffn = gain((DEPTH, D_MODEL))
    ffn_w_gate = nrm((N_DENSE, D_MODEL, D_FF), D_MODEL ** -0.5)
    ffn_w_up = nrm((N_DENSE, D_MODEL, D_FF), D_MODEL ** -0.5)
    ffn_w_down = nrm((N_DENSE, D_FF, D_MODEL), D_FF ** -0.5)
    moe_w_router = nrm((N_MOE, D_MODEL, N_EXPERTS), D_MODEL ** -0.5)
    moe_w_gate = nrm((N_MOE, N_EXPERTS, D_MODEL, D_FF_EXPERT), D_MODEL ** -0.5)
    moe_w_up = nrm((N_MOE, N_EXPERTS, D_MODEL, D_FF_EXPERT), D_MODEL ** -0.5)
    moe_w_down = nrm((N_MOE, N_EXPERTS, D_FF_EXPERT, D_MODEL), D_FF_EXPERT ** -0.5)
    return {'x': x, 'mem': mem, 'positions': positions, 'norm_mix': norm_mix, 'w_in': w_in,
            'ssm_a_re': ssm_a_re, 'ssm_a_im': ssm_a_im, 'ssm_log_dt': ssm_log_dt,
            'ssm_b_re': ssm_b_re, 'ssm_b_im': ssm_b_im, 'ssm_c_re': ssm_c_re, 'ssm_c_im': ssm_c_im,
            'ssm_d': ssm_d, 'ssm_w_glu': ssm_w_glu, 'w_ssm_out': w_ssm_out,
            'att_q_norm': att_q_norm, 'att_k_norm': att_k_norm, 'w_att_out': w_att_out,
            'norm_mem': norm_mem, 'w_mem_kv': w_mem_kv, 'mem_q_norm': mem_q_norm,
            'mem_k_norm': mem_k_norm, 'w_mem_out': w_mem_out, 'w_o': w_o, 'norm_ffn': norm_ffn,
            'ffn_w_gate': ffn_w_gate, 'ffn_w_up': ffn_w_up, 'ffn_w_down': ffn_w_down,
            'moe_w_router': moe_w_router, 'moe_w_gate': moe_w_gate, 'moe_w_up': moe_w_up,
            'moe_w_down': moe_w_down}


def reference(x, mem, positions, norm_mix, w_in, ssm_a_re, ssm_a_im, ssm_log_dt,
              ssm_b_re, ssm_b_im, ssm_c_re, ssm_c_im, ssm_d, ssm_w_glu, w_ssm_out,
              att_q_norm, att_k_norm, w_att_out, norm_mem, w_mem_kv, mem_q_norm,
              mem_k_norm, w_mem_out, w_o, norm_ffn, ffn_w_gate, ffn_w_up, ffn_w_down,
              moe_w_router, moe_w_gate, moe_w_up, moe_w_down):
    bsz, L, D = x.shape
    c0 = SSM_WIDTH
    c1 = c0 + ATT_WIDTH
    c2 = c1 + ATT_WIDTH
    c3 = c2 + ATT_WIDTH
    c4 = c3 + X_WIDTH
    for i in range(DEPTH):
        h = rmsnorm(x, norm_mix[i])
        z = h @ w_in[i]
        y_ssm = ssm_branch(z[..., :c0], ssm_a_re[i], ssm_a_im[i], ssm_log_dt[i],
                           ssm_b_re[i], ssm_b_im[i], ssm_c_re[i], ssm_c_im[i],
                           ssm_d[i], ssm_w_glu[i]) @ w_ssm_out[i]
        y_att = dilated_attention(z[..., c0:c1], z[..., c1:c2], z[..., c2:c3], positions,
                                  att_q_norm[i], att_k_norm[i]) @ w_att_out[i]
        mem_n = rmsnorm(mem, norm_mem[i])
        y_mem = memory_attention(z[..., c3:c4], mem_n, w_mem_kv[i],
                                 mem_q_norm[i], mem_k_norm[i]) @ w_mem_out[i]
        g = jax.nn.sigmoid(z[..., c4:].astype(F32)).reshape(bsz, L, N_BRANCH, D)
        merged = (g[..., 0, :] * y_ssm + g[..., 1, :] * y_att + g[..., 2, :] * y_mem).astype(x.dtype)
        x = x + merged @ w_o[i]
        h2 = rmsnorm(x, norm_ffn[i])
        if i % 2 == 0:
            j = i // 2
            x = x + swiglu(h2, ffn_w_gate[j], ffn_w_up[j], ffn_w_down[j])
        else:
            j = i // 2
            x = x + moe_swiglu(h2, moe_w_router[j], moe_w_gate[j], moe_w_up[j], moe_w_down[j])
    return x
```

```python
import functools
import math

import jax
import jax.numpy as jnp
from jax import lax
from jax.experimental import pallas as pl
from jax.experimental.pallas import tpu as pltpu

F32 = jnp.float32
BF16 = jnp.bfloat16
I32 = jnp.int32
U32 = jnp.uint32

EPS = 1e-6
D_MODEL = 1024
MEM_LEN = 256
SSM_WIDTH = 512
SSM_GROUP = 16
SSM_GROUPS = 32
SSM_STATE = 64
ATT_HEAD_DIM = 64
ATT_SLOTS = 4
DIL_PAIRS = ((128, 1), (512, 4), (2048, 16))
ATT_WIDTH = 768
ATT_OUT = 256
BLOCK = 128
ROPE_THETA = 10000.0
X_HEADS = 4
X_HEAD_DIM = 128
X_WIDTH = 512
D_FF = 2816
N_EXPERTS = 8
D_FF_EXPERT = 3584

LANES = 128
SSM_CHUNK = 128
QKV_W = 3 * ATT_OUT
ZMAIN_W = QKV_W + X_WIDTH + 3 * D_MODEL
VMEM_LIMIT = 56 * 1024 * 1024


def _cparams(sem, vmem=VMEM_LIMIT):
    return pltpu.CompilerParams(dimension_semantics=sem, vmem_limit_bytes=vmem)


def _rms(x, g):
    ms = jnp.mean(x * x, axis=-1, keepdims=True)
    return x * lax.rsqrt(ms + EPS) * g


def _dot(a, b):
    return jnp.dot(a, b, preferred_element_type=F32)


def _dot_nt(a, b):
    return lax.dot_general(a, b, (((1,), (1,)), ((), ())), preferred_element_type=F32)


def _rope_body(pos_ref, inv_ref, sgn_ref, cos_ref, sin_ref):
    ang = pos_ref[...].astype(F32) * inv_ref[...]
    cos_ref[...] = jnp.cos(ang)
    sin_ref[...] = jnp.sin(ang) * sgn_ref[...]


def _rope_tables(positions):
    n = positions.size
    half = ATT_HEAD_DIM // 2
    inv = ROPE_THETA ** (-jnp.arange(half, dtype=F32) / half)
    inv_row = jnp.tile(inv, LANES // half).reshape(1, LANES)
    lane = jnp.arange(LANES)
    sgn_row = jnp.where((lane % ATT_HEAD_DIM) < half, -1.0, 1.0).astype(F32).reshape(1, LANES)
    tm = 2048
    return pl.pallas_call(
        _rope_body,
        out_shape=(jax.ShapeDtypeStruct((n, LANES), F32), jax.ShapeDtypeStruct((n, LANES), F32)),
        grid=(n // tm,),
        in_specs=[pl.BlockSpec((tm, 1), lambda i: (i, 0)),
                  pl.BlockSpec((1, LANES), lambda i: (0, 0)),
                  pl.BlockSpec((1, LANES), lambda i: (0, 0))],
        out_specs=(pl.BlockSpec((tm, LANES), lambda i: (i, 0)),
                   pl.BlockSpec((tm, LANES), lambda i: (i, 0))),
        compiler_params=_cparams(("parallel",)),
        name="rope_tables",
    )(positions.reshape(n, 1), inv_row, sgn_row)


def _col_chunks(width, step=512):
    return [(c, min(step, width - c)) for c in range(0, width, step)]


def _inproj_main_body(x_ref, g_ref, wut_ref, w_ref, ut_ref, qkv_ref, memq_ref, gate_ref):
    h = _rms(x_ref[...], g_ref[...]).astype(BF16)
    ut_ref[...] = _dot_nt(wut_ref[...], h).astype(BF16)
    col = 0
    for ref in (qkv_ref, memq_ref, gate_ref):
        for c, w in _col_chunks(ref.shape[1]):
            ref[:, c:c + w] = _dot(h, w_ref[:, col + c:col + c + w]).astype(BF16)
        col += ref.shape[1]


def _inproj_main(x2d, g, wut, wmain):
    n = x2d.shape[0]
    tm = 512
    row = lambda i: (i, 0)
    const = lambda i: (0, 0)
    widths = (QKV_W, X_WIDTH, 3 * D_MODEL)
    return pl.pallas_call(
        _inproj_main_body,
        out_shape=(jax.ShapeDtypeStruct((SSM_WIDTH, n), BF16),)
        + tuple(jax.ShapeDtypeStruct((n, w), BF16) for w in widths),
        grid=(n // tm,),
        in_specs=[pl.BlockSpec((tm, D_MODEL), row),
                  pl.BlockSpec((1, D_MODEL), const),
                  pl.BlockSpec((SSM_WIDTH, D_MODEL), const),
                  pl.BlockSpec((D_MODEL, ZMAIN_W), const)],
        out_specs=(pl.BlockSpec((SSM_WIDTH, tm), lambda i: (0, i)),)
        + tuple(pl.BlockSpec((tm, w), row) for w in widths),
        compiler_params=_cparams(("parallel",)),
        name="inproj_main",
    )(x2d, g, wut, wmain)


def _inproj_dil_body(x_ref, g_ref, w_ref, z_ref):
    h = _rms(x_ref[...], g_ref[...]).astype(BF16)
    z_ref[...] = _dot(h, w_ref[...]).astype(BF16)


def _inproj_dil(x, g, w, dil):
    bsz, seq, d = x.shape
    n = seq // dil
    tm = min(n, 256)
    nt = n // tm
    xv = x.reshape(bsz * n, dil * d)

    def out_map(i, r):
        return ((i // nt) * dil + r) * nt + i % nt, 0

    return pl.pallas_call(
        _inproj_dil_body,
        out_shape=jax.ShapeDtypeStruct((bsz * seq, QKV_W), BF16),
        grid=(bsz * nt, dil),
        in_specs=[pl.BlockSpec((tm, d), lambda i, r: (i, r)),
                  pl.BlockSpec((1, d), lambda i, r: (0, 0)),
                  pl.BlockSpec((d, QKV_W), lambda i, r: (0, 0))],
        out_specs=pl.BlockSpec((tm, QKV_W), out_map),
        compiler_params=_cparams(("parallel", "parallel")),
        name=f"inproj_dil{dil}",
    )(xv, g, w).reshape(bsz * dil, n, QKV_W)


def _qk_prep(xb, cos2, sin2, g, gs, bd, perm, scale):
    xf = xb.astype(F32)
    ms = _dot((xf * xf).astype(BF16), bd)
    xs = _dot(xb, perm)
    y = lax.rsqrt(ms + EPS) * scale * (xf * (g * cos2) + xs * (gs * sin2))
    return y.astype(BF16)


def _attn_body(qkv_ref, halo_ref, cos_ref, sin_ref, cosh_ref, sinh_ref, gq_ref, gqs_ref,
               gk_ref, gks_ref, bd_ref, perm_ref, o_ref, lse_ref, qbuf, kbuf, vbuf, *, nres, tq):
    j = pl.program_id(2)
    bd = bd_ref[...]
    perm = perm_ref[...]
    lane = lax.broadcasted_iota(I32, (BLOCK, LANES), 1)
    low = lane < ATT_HEAD_DIM
    qi = lax.broadcasted_iota(I32, (BLOCK, 2 * BLOCK), 0) + BLOCK
    ki = lax.broadcasted_iota(I32, (BLOCK, 2 * BLOCK), 1)
    off = qi - ki
    band = (off >= 0) & (off <= BLOCK)
    for r in range(nres):
        cos = cos_ref[:, r * LANES:(r + 1) * LANES]
        sin = sin_ref[:, r * LANES:(r + 1) * LANES]
        cosh = cosh_ref[:, r * LANES:(r + 1) * LANES]
        sinh = sinh_ref[:, r * LANES:(r + 1) * LANES]
        cos2 = jnp.concatenate([cos, cos], axis=1)
        sin2 = jnp.concatenate([sin, sin], axis=1)
        cosh2 = jnp.concatenate([cosh, cosh], axis=1)
        sinh2 = jnp.concatenate([sinh, sinh], axis=1)
        qbuf[...] = _qk_prep(qkv_ref[r, :, 0:ATT_OUT], cos2, sin2, gq_ref[...], gqs_ref[...],
                             bd, perm, ATT_HEAD_DIM ** -0.5)
        kbuf[0:BLOCK, :] = _qk_prep(halo_ref[r, :, ATT_OUT:2 * ATT_OUT], cosh2, sinh2,
                                    gk_ref[...], gks_ref[...], bd, perm, 1.0)
        kbuf[BLOCK:, :] = _qk_prep(qkv_ref[r, :, ATT_OUT:2 * ATT_OUT], cos2, sin2,
                                   gk_ref[...], gks_ref[...], bd, perm, 1.0)
        vbuf[0:BLOCK, :] = halo_ref[r, :, 2 * ATT_OUT:3 * ATT_OUT]
        vbuf[BLOCK:, :] = qkv_ref[r, :, 2 * ATT_OUT:3 * ATT_OUT]

        def sub_block(s, carry):
            row0 = pl.multiple_of(s * BLOCK, BLOCK)
            valid = band & ((ki >= BLOCK) | (s > 0) | (j > 0))
            for p in range(ATT_OUT // LANES):
                cols = slice(p * LANES, (p + 1) * LANES)
                qp = qbuf[pl.ds(row0, BLOCK), cols]
                kp = kbuf[pl.ds(row0, 2 * BLOCK), cols]
                vp = vbuf[pl.ds(row0, 2 * BLOCK), cols]
                outs, lses = [], []
                for h in range(2):
                    qm = jnp.where(low if h == 0 else ~low, qp, jnp.zeros_like(qp))
                    sc = jnp.where(valid, _dot_nt(qm, kp), -1e30)
                    m = jnp.max(sc, axis=-1, keepdims=True)
                    pr = jnp.exp(sc - m)
                    den = jnp.sum(pr, axis=-1, keepdims=True)
                    outs.append(_dot(pr.astype(BF16), vp) / den)
                    lses.append(m + jnp.log(den))
                o_ref[pl.ds(row0, BLOCK), r * ATT_OUT + p * LANES:r * ATT_OUT + (p + 1) * LANES] = (
                    jnp.where(low, outs[0], outs[1]))
                lse_ref[pl.ds(row0, BLOCK), r * ATT_OUT + p * LANES:r * ATT_OUT + (p + 1) * LANES] = (
                    jnp.where(low, lses[0], lses[1]))
            return carry

        lax.fori_loop(0, tq // BLOCK, sub_block, 0)


def _dilated_attention_group(qkv3, cos_t, sin_t, gq, gqs, gk, gks, bd, perm, bsz, seq, dil):
    n = seq // dil
    nres = 4 if dil >= 4 * 4 else 1
    tq = min(n, 1024)
    nq = n // tq
    hb = tq // BLOCK
    ntok = bsz * seq
    cosv = cos_t.reshape(bsz * n, dil * LANES)
    sinv = sin_t.reshape(bsz * n, dil * LANES)

    def qkv_map(b, rb, j):
        return b * (dil // nres) + rb, j, 0

    def halo_map(b, rb, j):
        return b * (dil // nres) + rb, jnp.maximum(j * hb - 1, 0), 0

    def tab_map(b, rb, j):
        return b * nq + j, rb

    def tabh_map(b, rb, j):
        return b * (n // BLOCK) + jnp.maximum(j * hb - 1, 0), rb

    const = lambda b, rb, j: (0, 0)
    o, lse = pl.pallas_call(
        functools.partial(_attn_body, nres=nres, tq=tq),
        out_shape=(jax.ShapeDtypeStruct((bsz * n, dil * ATT_OUT), F32),
                   jax.ShapeDtypeStruct((bsz * n, dil * ATT_OUT), F32)),
        grid=(bsz, dil // nres, nq),
        in_specs=[pl.BlockSpec((nres, tq, QKV_W), qkv_map),
                  pl.BlockSpec((nres, BLOCK, QKV_W), halo_map),
                  pl.BlockSpec((tq, nres * LANES), tab_map),
                  pl.BlockSpec((tq, nres * LANES), tab_map),
                  pl.BlockSpec((BLOCK, nres * LANES), tabh_map),
                  pl.BlockSpec((BLOCK, nres * LANES), tabh_map),
                  pl.BlockSpec((1, ATT_OUT), const), pl.BlockSpec((1, ATT_OUT), const),
                  pl.BlockSpec((1, ATT_OUT), const), pl.BlockSpec((1, ATT_OUT), const),
                  pl.BlockSpec((ATT_OUT, ATT_OUT), const), pl.BlockSpec((ATT_OUT, ATT_OUT), const)],
        out_specs=(pl.BlockSpec((tq, nres * ATT_OUT), tab_map),
                   pl.BlockSpec((tq, nres * ATT_OUT), tab_map)),
        scratch_shapes=[pltpu.VMEM((tq, ATT_OUT), BF16),
                        pltpu.VMEM((BLOCK + tq, ATT_OUT), BF16),
                        pltpu.VMEM((BLOCK + tq, ATT_OUT), BF16)],
        compiler_params=_cparams(("parallel", "parallel", "arbitrary")),
        name=f"dilated_attn{dil}",
    )(qkv3, qkv3, cosv, sinv, cosv, sinv, gq, gqs, gk, gks, bd, perm)
    return o.reshape(ntok, ATT_OUT), lse.reshape(ntok, ATT_OUT)


def _ssm_body(u_ref, ktab_ref, w_ref, v_ref, lam_ref, dvec_ref, y_ref, m_ref, sloc_ref, sin_ref,
              *, bsz, cpb):
    tc = SSM_CHUNK
    row = lax.broadcasted_iota(I32, (tc, tc), 0)
    col = lax.broadcasted_iota(I32, (tc, tc), 1)
    causal = col >= row

    def build(cp, carry):
        r0 = pl.multiple_of(cp * tc, tc)
        for c in range(SSM_GROUP):
            kv = ktab_ref[0, pl.ds(cp * SSM_GROUP + c, 1), :]
            tile = pltpu.roll(jnp.broadcast_to(kv, (tc, tc)), 0, 1, stride=1, stride_axis=0)
            m_ref[pl.ds(r0, tc), c * tc:(c + 1) * tc] = jnp.where(causal, tile, 0.0).astype(BF16)
        return carry

    lax.fori_loop(0, SSM_GROUP, build, 0)

    u = jnp.concatenate([u_ref[c] for c in range(SSM_GROUP)], axis=1)
    y = _dot(u, m_ref[...])
    sloc_ref[...] = _dot(u, w_ref[0])

    a1 = lam_ref[0, 0:1, :]
    a2 = lam_ref[0, 1:2, :]
    s = jnp.zeros((bsz, 2 * SSM_STATE), F32)
    sin_ref[pl.ds(0, bsz, stride=cpb), :] = s
    for k in range(1, cpb):
        s = a1 * s + a2 * pltpu.roll(s, SSM_STATE, 1) + sloc_ref[pl.ds(k - 1, bsz, stride=cpb), :]
        sin_ref[pl.ds(k, bsz, stride=cpb), :] = s

    y = y + _dot(sin_ref[...].astype(BF16), v_ref[0]) + dvec_ref[0] * u.astype(F32)
    y = jax.nn.gelu(y, approximate=True)
    for c in range(SSM_GROUP):
        y_ref[c] = y[:, c * tc:(c + 1) * tc].astype(BF16)


def _ssm_operators(a_re, a_im, log_dt, b_re, b_im, c_re, c_im, d_skip):
    tc = SSM_CHUNK
    lam = lax.complex(a_re.astype(F32), a_im.astype(F32))
    dt = jnp.exp(log_dt.astype(F32))[:, None]
    lam_dt = lam * dt
    lam_bar = jnp.exp(lam_dt)
    b = lax.complex(b_re.astype(F32), b_im.astype(F32))
    b_bar = ((lam_bar - 1.0) / lam)[..., None] * b
    c = lax.complex(c_re.astype(F32), c_im.astype(F32))
    k = jnp.arange(tc + 1, dtype=F32)
    pw = jnp.exp(lam_dt[:, None, :] * k[None, :, None])
    ktab = jnp.einsum('gcp,gkp,gpd->gdck', c, pw[:, :tc], b_bar).real
    ktab = ktab.reshape(SSM_GROUPS, SSM_GROUP * SSM_GROUP, tc).astype(F32)
    wc = jnp.einsum('gjp,gpd->gdjp', pw[:, tc - 1::-1][:, :tc], b_bar)
    wc = wc.reshape(SSM_GROUPS, SSM_GROUP * tc, SSM_STATE)
    w = jnp.concatenate([wc.real, wc.imag], axis=-1).astype(BF16)
    vc = jnp.einsum('gcp,gtp->gpct', c, pw[:, 1:tc + 1]).reshape(SSM_GROUPS, SSM_STATE, SSM_GROUP * tc)
    v = jnp.concatenate([vc.real, -vc.imag], axis=1).astype(BF16)
    lt = pw[:, tc]
    lam_rows = jnp.stack([jnp.concatenate([lt.real, lt.real], -1),
                          jnp.concatenate([-lt.imag, lt.imag], -1)], axis=1).astype(F32)
    dvec = jnp.repeat(d_skip.astype(F32).reshape(SSM_GROUPS, SSM_GROUP), tc, axis=1)
    return ktab, w, v, lam_rows, dvec.reshape(SSM_GROUPS, 1, SSM_GROUP * tc)


def _ssm_scan(ut, ops, bsz, seq):
    ktab, w, v, lam_rows, dvec = ops
    ntok = bsz * seq
    tc = SSM_CHUNK
    nch = ntok // tc
    u3 = ut.reshape(SSM_WIDTH, nch, tc)
    gmap = lambda g: (g, 0, 0)
    y3 = pl.pallas_call(
        functools.partial(_ssm_body, bsz=bsz, cpb=seq // tc),
        out_shape=jax.ShapeDtypeStruct((SSM_WIDTH, nch, tc), BF16),
        grid=(SSM_GROUPS,),
        in_specs=[pl.BlockSpec((SSM_GROUP, nch, tc), gmap),
                  pl.BlockSpec((1, SSM_GROUP * SSM_GROUP, tc), gmap),
                  pl.BlockSpec((1, SSM_GROUP * tc, 2 * SSM_STATE), gmap),
                  pl.BlockSpec((1, 2 * SSM_STATE, SSM_GROUP * tc), gmap),
                  pl.BlockSpec((1, 2, 2 * SSM_STATE), gmap),
                  pl.BlockSpec((1, 1, SSM_GROUP * tc), gmap)],
        out_specs=pl.BlockSpec((SSM_GROUP, nch, tc), gmap),
        scratch_shapes=[pltpu.VMEM((SSM_GROUP * tc, SSM_GROUP * tc), BF16),
                        pltpu.VMEM((nch, 2 * SSM_STATE), F32),
                        pltpu.VMEM((nch, 2 * SSM_STATE), F32)],
        compiler_params=_cparams(("parallel",)),
        name="ssm_scan",
    )(u3, ktab, w, v, lam_rows, dvec)
    return y3.reshape(SSM_WIDTH, ntok)


def _memkv_body(mem_ref, g_ref, w_ref, gk_ref, k_ref, v_ref):
    h = _rms(mem_ref[...], g_ref[...]).astype(BF16)
    kv = _dot(h, w_ref[...])
    for hd in range(X_HEADS):
        cols = slice(hd * X_HEAD_DIM, (hd + 1) * X_HEAD_DIM)
        k_ref[:, cols] = _rms(kv[:, cols], gk_ref[...]).astype(BF16)
    v_ref[...] = kv[:, X_WIDTH:].astype(BF16)


def _memory_kv(mem2d, g, w_kv, gk):
    m = mem2d.shape[0]
    tm = MEM_LEN
    return pl.pallas_call(
        _memkv_body,
        out_shape=(jax.ShapeDtypeStruct((m, X_WIDTH), BF16), jax.ShapeDtypeStruct((m, X_WIDTH), BF16)),
        grid=(m // tm,),
        in_specs=[pl.BlockSpec((tm, D_MODEL), lambda i: (i, 0)),
                  pl.BlockSpec((1, D_MODEL), lambda i: (0, 0)),
                  pl.BlockSpec((D_MODEL, 2 * X_WIDTH), lambda i: (0, 0)),
                  pl.BlockSpec((1, X_HEAD_DIM), lambda i: (0, 0))],
        out_specs=(pl.BlockSpec((tm, X_WIDTH), lambda i: (i, 0)),
                   pl.BlockSpec((tm, X_WIDTH), lambda i: (i, 0))),
        compiler_params=_cparams(("parallel",)),
        name="memory_kv",
    )(mem2d, g, w_kv, gk)


def _pack_bf16_pairs(x):
    c = x.shape[1] // 2
    bits = pltpu.bitcast(x.astype(BF16).astype(F32), U32)
    return (bits[:, :c] & jnp.uint32(0xFFFF0000)) | (bits[:, c:] >> 16)


def _unpack_bf16_pairs(p):
    hi = pltpu.bitcast(p & jnp.uint32(0xFFFF0000), F32)
    lo = pltpu.bitcast(p << 16, F32)
    return jnp.concatenate([hi, lo], axis=1)


def _merge_body(*refs, moe):
    (x_ref, memq_ref, gate_ref, yt_ref, o0_ref, l0_ref, o1_ref, l1_ref, o2_ref, l2_ref,
     km_ref, vm_ref, wglut_ref, wso_ref, wao_ref, wmo_ref, wo_ref, gmq_ref, gffn_ref) = refs[:19]
    if moe:
        wrh_ref, wrl_ref, x1_ref, hp_ref, ri_ref, rw_ref = refs[19:]
    else:
        x1_ref, h2_ref = refs[19:]

    ga = _dot(wglut_ref[...], yt_ref[...])
    glu = ga[:SSM_WIDTH] * jax.nn.sigmoid(ga[SSM_WIDTH:])
    y_ssm = _dot(jnp.transpose(glu).astype(BF16), wso_ref[...])

    l0, l1, l2 = l0_ref[...], l1_ref[...], l2_ref[...]
    mx = jnp.maximum(jnp.maximum(l0, l1), l2)
    e0, e1, e2 = jnp.exp(l0 - mx), jnp.exp(l1 - mx), jnp.exp(l2 - mx)
    att = (e0 * o0_ref[...] + e1 * o1_ref[...] + e2 * o2_ref[...]) / (e0 + e1 + e2)
    y_att = _dot(att.astype(BF16), wao_ref[...])

    heads = []
    for hd in range(X_HEADS):
        cols = slice(hd * X_HEAD_DIM, (hd + 1) * X_HEAD_DIM)
        q = (_rms(memq_ref[:, cols].astype(F32), gmq_ref[...]) * (X_HEAD_DIM ** -0.5)).astype(BF16)
        sc = _dot_nt(q, km_ref[:, cols])
        m = jnp.max(sc, axis=-1, keepdims=True)
        pr = jnp.exp(sc - m)
        den = jnp.sum(pr, axis=-1, keepdims=True)
        heads.append(_dot(pr.astype(BF16), vm_ref[:, cols]) / den)
    y_mem = _dot(jnp.concatenate(heads, axis=1).astype(BF16), wmo_ref[...])

    d = D_MODEL
    merged = (jax.nn.sigmoid(gate_ref[:, 0:d].astype(F32)) * y_ssm
              + jax.nn.sigmoid(gate_ref[:, d:2 * d].astype(F32)) * y_att
              + jax.nn.sigmoid(gate_ref[:, 2 * d:3 * d].astype(F32)) * y_mem)
    x1 = x_ref[...] + _dot(merged.astype(BF16), wo_ref[...])
    x1_ref[...] = x1
    h2 = _rms(x1, gffn_ref[...])
    if not moe:
        h2_ref[...] = h2.astype(BF16)
        return

    hp_ref[...] = _pack_bf16_pairs(h2)
    hi = h2.astype(BF16)
    lo = (h2 - hi.astype(F32)).astype(BF16)
    logits = _dot(hi, wrh_ref[...]) + _dot(hi, wrl_ref[...]) + _dot(lo, wrh_ref[...])
    lane = lax.broadcasted_iota(I32, logits.shape, 1)
    lg = jnp.where(lane < N_EXPERTS, logits, -jnp.inf)
    v1 = jnp.max(lg, axis=-1, keepdims=True)
    i1 = jnp.min(jnp.where(lg == v1, lane, LANES), axis=-1, keepdims=True)
    lg2 = jnp.where(lane == i1, -jnp.inf, lg)
    v2 = jnp.max(lg2, axis=-1, keepdims=True)
    i2 = jnp.min(jnp.where(lg2 == v2, lane, LANES), axis=-1, keepdims=True)
    e = jnp.exp(v2 - v1)
    ri_ref[...] = jnp.where(lane == 0, i1, jnp.where(lane == 1, i2, 0))
    rw_ref[...] = jnp.where(lane == 0, 1.0 / (1.0 + e), jnp.where(lane == 1, e / (1.0 + e), 0.0))


def _merge(x2d, memq, gates, yt, att, kmem, vmem, wts, bsz, seq, router=None):
    n = x2d.shape[0]
    tm = 512
    tpb = seq // tm
    moe = router is not None
    row = lambda i: (i, 0)
    const = lambda i: (0, 0)
    in_specs = [pl.BlockSpec((tm, D_MODEL), row),
                pl.BlockSpec((tm, X_WIDTH), row),
                pl.BlockSpec((tm, 3 * D_MODEL), row),
                pl.BlockSpec((SSM_WIDTH, tm), lambda i: (0, i))]
    in_specs += [pl.BlockSpec((tm, ATT_OUT), row)] * 6
    in_specs += [pl.BlockSpec((MEM_LEN, X_WIDTH), lambda i: (i // tpb, 0))] * 2
    wglut, wso, wao, wmo, wo, gmq, gffn = wts
    in_specs += [pl.BlockSpec(w.shape, const) for w in (wglut, wso, wao, wmo, wo, gmq, gffn)]
    args = [x2d, memq, gates, yt, *att, kmem, vmem, wglut, wso, wao, wmo, wo, gmq, gffn]
    if moe:
        in_specs += [pl.BlockSpec(router[0].shape, const)] * 2
        args += list(router)
        out_shape = (jax.ShapeDtypeStruct((n, D_MODEL), F32),
                     jax.ShapeDtypeStruct((n, D_MODEL // 2), U32),
                     jax.ShapeDtypeStruct((n, LANES), I32),
                     jax.ShapeDtypeStruct((n, LANES), F32))
        out_specs = (pl.BlockSpec((tm, D_MODEL), row), pl.BlockSpec((tm, D_MODEL // 2), row),
                     pl.BlockSpec((tm, LANES), row), pl.BlockSpec((tm, LANES), row))
    else:
        out_shape = (jax.ShapeDtypeStruct((n, D_MODEL), F32), jax.ShapeDtypeStruct((n, D_MODEL), BF16))
        out_specs = (pl.BlockSpec((tm, D_MODEL), row), pl.BlockSpec((tm, D_MODEL), row))
    return pl.pallas_call(
        functools.partial(_merge_body, moe=moe),
        out_shape=out_shape,
        grid=(n // tm,),
        in_specs=in_specs,
        out_specs=out_specs,
        compiler_params=_cparams(("parallel",)),
        name="merge_moe" if moe else "merge_dense",
    )(*args)


def _ffn_body(h_ref, x_ref, wg_ref, wu_ref, wd_ref, o_ref):
    f = pl.program_id(1)
    h = h_ref[...]
    a = _dot(h, wg_ref[...])
    act = (a * jax.nn.sigmoid(a) * _dot(h, wu_ref[...])).astype(BF16)
    part = _dot(act, wd_ref[...])

    @pl.when(f == 0)
    def _():
        o_ref[...] = x_ref[...] + part

    @pl.when(f > 0)
    def _():
        o_ref[...] += part


def _dense_ffn(h2, x1, wg, wu, wd):
    n = h2.shape[0]
    tm = 512
    tf = D_FF // 2
    return pl.pallas_call(
        _ffn_body,
        out_shape=jax.ShapeDtypeStruct((n, D_MODEL), F32),
        grid=(n // tm, D_FF // tf),
        in_specs=[pl.BlockSpec((tm, D_MODEL), lambda i, f: (i, 0)),
                  pl.BlockSpec((tm, D_MODEL), lambda i, f: (i, 0)),
                  pl.BlockSpec((D_MODEL, tf), lambda i, f: (0, f)),
                  pl.BlockSpec((D_MODEL, tf), lambda i, f: (0, f)),
                  pl.BlockSpec((tf, D_MODEL), lambda i, f: (f, 0))],
        out_specs=pl.BlockSpec((tm, D_MODEL), lambda i, f: (i, 0)),
        compiler_params=_cparams(("parallel", "arbitrary")),
        name="dense_ffn",
    )(h2, x1, wg, wu, wd)


MOE_TM = 512
PLAN_TB = 512


def _moe_rows(ntok):
    return 2 * ntok + N_EXPERTS * MOE_TM


def _plan_body(ri_ref, rank_ref, cnt_ref, carry_ref):
    i = pl.program_id(0)

    @pl.when(i == 0)
    def _():
        carry_ref[...] = jnp.zeros_like(carry_ref)

    ri = ri_ref[...]
    lane = lax.broadcasted_iota(I32, ri.shape, 1)
    e1 = ri[:, 0:1]
    e2 = ri[:, 1:2]
    oh = (jnp.where(lane < N_EXPERTS, e1, e2 + N_EXPERTS) == lane) & (lane < 2 * N_EXPERTS)
    ohf = jnp.where(oh, 1.0, 0.0)
    tr = lax.broadcasted_iota(I32, (PLAN_TB, PLAN_TB), 0)
    tcol = lax.broadcasted_iota(I32, (PLAN_TB, PLAN_TB), 1)
    tri = jnp.where(tcol < tr, 1.0, 0.0).astype(BF16)
    excl = _dot(tri, ohf.astype(BF16)) + carry_ref[...]
    mine = jnp.where(oh, excl, 0.0)
    r0 = jnp.sum(jnp.where(lane < N_EXPERTS, mine, 0.0), axis=-1, keepdims=True)
    r1 = jnp.sum(jnp.where(lane >= N_EXPERTS, mine, 0.0), axis=-1, keepdims=True)
    rank_ref[...] = jnp.where(lane == 0, r0, jnp.where(lane == 1, r1, 0.0))
    carry_ref[...] += jnp.sum(ohf, axis=0, keepdims=True)
    cnt_ref[...] = carry_ref[...]


def _moe_plan(ri):
    n = ri.shape[0]
    return pl.pallas_call(
        _plan_body,
        out_shape=(jax.ShapeDtypeStruct((n, LANES), F32), jax.ShapeDtypeStruct((1, LANES), F32)),
        grid=(n // PLAN_TB,),
        in_specs=[pl.BlockSpec((PLAN_TB, LANES), lambda i: (i, 0))],
        out_specs=(pl.BlockSpec((PLAN_TB, LANES), lambda i: (i, 0)),
                   pl.BlockSpec((1, LANES), lambda i: (0, 0))),
        scratch_shapes=[pltpu.VMEM((1, LANES), F32)],
        compiler_params=_cparams(("arbitrary",)),
        name="moe_plan",
    )(ri)


DISPATCH_TB = 1024


def _row_copy(src, dst, s, d, sem):
    return pltpu.make_async_copy(src.at[pl.ds(s, 1)], dst.at[pl.ds(d, 1)], sem)


def _dispatch_body(pos_ref, h_ref, xs_in_ref, xs_ref, sem):
    del xs_in_ref
    base = pl.program_id(0) * DISPATCH_TB

    def issue(t, carry):
        tok = base + t
        _row_copy(h_ref, xs_ref, tok, pos_ref[2 * tok], sem).start()
        _row_copy(h_ref, xs_ref, tok, pos_ref[2 * tok + 1], sem).start()
        return carry

    lax.fori_loop(0, DISPATCH_TB, issue, 0, unroll=8)

    def drain(t, carry):
        _row_copy(h_ref, xs_ref, 0, 0, sem).wait()
        _row_copy(h_ref, xs_ref, 0, 0, sem).wait()
        return carry

    lax.fori_loop(0, DISPATCH_TB, drain, 0)


def _moe_dispatch(pos, hp):
    n, c = hp.shape
    xs0 = jnp.zeros((_moe_rows(n), c), U32)
    return pl.pallas_call(
        _dispatch_body,
        out_shape=jax.ShapeDtypeStruct(xs0.shape, U32),
        grid_spec=pltpu.PrefetchScalarGridSpec(
            num_scalar_prefetch=1,
            grid=(n // DISPATCH_TB,),
            in_specs=[pl.BlockSpec(memory_space=pl.ANY), pl.BlockSpec(memory_space=pl.ANY)],
            out_specs=pl.BlockSpec(memory_space=pl.ANY),
            scratch_shapes=[pltpu.SemaphoreType.DMA(())]),
        input_output_aliases={2: 0},
        compiler_params=_cparams(("arbitrary",)),
        name="moe_dispatch",
    )(pos, hp, xs0)


COMBINE_TB = 256


def _combine_body(pos_ref, x_ref, rw_ref, ys_ref, o_ref, buf, sem):
    base = pl.program_id(0) * COMBINE_TB

    def issue(t, carry):
        tok = base + t
        _row_copy(ys_ref, buf.at[0], pos_ref[2 * tok], t, sem).start()
        _row_copy(ys_ref, buf.at[1], pos_ref[2 * tok + 1], t, sem).start()
        return carry

    lax.fori_loop(0, COMBINE_TB, issue, 0, unroll=8)

    def drain(t, carry):
        _row_copy(ys_ref, buf.at[0], 0, 0, sem).wait()
        _row_copy(ys_ref, buf.at[1], 0, 0, sem).wait()
        return carry

    lax.fori_loop(0, COMBINE_TB, drain, 0)
    rw = rw_ref[...]
    o_ref[...] = (x_ref[...] + rw[:, 0:1] * _unpack_bf16_pairs(buf[0])
                  + rw[:, 1:2] * _unpack_bf16_pairs(buf[1]))


def _moe_combine(pos, x1, rw, ys):
    n = x1.shape[0]
    c = ys.shape[1]
    return pl.pallas_call(
        _combine_body,
        out_shape=jax.ShapeDtypeStruct((n, D_MODEL), F32),
        grid_spec=pltpu.PrefetchScalarGridSpec(
            num_scalar_prefetch=1,
            grid=(n // COMBINE_TB,),
            in_specs=[pl.BlockSpec((COMBINE_TB, D_MODEL), lambda i, p: (i, 0)),
                      pl.BlockSpec((COMBINE_TB, LANES), lambda i, p: (i, 0)),
                      pl.BlockSpec(memory_space=pl.ANY)],
            out_specs=pl.BlockSpec((COMBINE_TB, D_MODEL), lambda i, p: (i, 0)),
            scratch_shapes=[pltpu.VMEM((2, COMBINE_TB, c), U32), pltpu.SemaphoreType.DMA(())]),
        compiler_params=_cparams(("arbitrary",)),
        name="moe_combine",
    )(pos, x1, rw, ys)


MOE_TF = 896


def _experts_body(te_ref, nv_ref, xs_ref, wg_ref, wu_ref, wd_ref, ys_ref, xb_ref, acc_ref):
    i = pl.program_id(0)
    f = pl.program_id(1)
    nf = pl.num_programs(1)

    @pl.when(i < nv_ref[0])
    def _():
        @pl.when(f == 0)
        def _():
            xb_ref[...] = _unpack_bf16_pairs(xs_ref[...]).astype(BF16)

        h = xb_ref[...]
        a = _dot(h, wg_ref[0])
        act = (a * jax.nn.sigmoid(a) * _dot(h, wu_ref[0])).astype(BF16)
        part = _dot(act, wd_ref[0])

        @pl.when(f == 0)
        def _():
            acc_ref[...] = part

        @pl.when(f > 0)
        def _():
            acc_ref[...] += part

        @pl.when(f == nf - 1)
        def _():
            ys_ref[...] = _pack_bf16_pairs(acc_ref[...])

    @pl.when((i >= nv_ref[0]) & (f == nf - 1))
    def _():
        ys_ref[...] = jnp.zeros_like(ys_ref)


def _moe_experts(tile_expert, n_valid, xs, wg, wu, wd):
    rows, c = xs.shape
    nt = rows // MOE_TM
    nf = D_FF_EXPERT // MOE_TF

    def tile(i, nv):
        return jnp.minimum(i, nv[0] - 1)

    def fblk(i, f, nv):
        return jnp.where(i < nv[0], f, nf - 1)

    return pl.pallas_call(
        _experts_body,
        out_shape=jax.ShapeDtypeStruct((rows, c), U32),
        grid_spec=pltpu.PrefetchScalarGridSpec(
            num_scalar_prefetch=2,
            grid=(nt, nf),
            in_specs=[pl.BlockSpec((MOE_TM, c), lambda i, f, te, nv: (tile(i, nv), 0)),
                      pl.BlockSpec((1, D_MODEL, MOE_TF),
                                   lambda i, f, te, nv: (te[tile(i, nv)], 0, fblk(i, f, nv))),
                      pl.BlockSpec((1, D_MODEL, MOE_TF),
                                   lambda i, f, te, nv: (te[tile(i, nv)], 0, fblk(i, f, nv))),
                      pl.BlockSpec((1, MOE_TF, D_MODEL),
                                   lambda i, f, te, nv: (te[tile(i, nv)], fblk(i, f, nv), 0))],
            out_specs=pl.BlockSpec((MOE_TM, c), lambda i, f, te, nv: (i, 0)),
            scratch_shapes=[pltpu.VMEM((MOE_TM, D_MODEL), BF16), pltpu.VMEM((MOE_TM, D_MODEL), F32)]),
        compiler_params=_cparams(("arbitrary", "arbitrary")),
        name="moe_experts",
    )(tile_expert, n_valid, xs, wg, wu, wd)


def _moe_ffn(x1, hp, ri, rw, wg, wu, wd):
    rank, cnt = _moe_plan(ri)
    c0 = cnt[0, :N_EXPERTS].astype(I32)
    c1 = cnt[0, N_EXPERTS:2 * N_EXPERTS].astype(I32)
    padded = ((c0 + c1 + MOE_TM - 1) // MOE_TM) * MOE_TM
    ends = jnp.cumsum(padded)
    off = ends - padded
    e1, e2 = ri[:, 0], ri[:, 1]
    pos0 = off[e1] + rank[:, 0].astype(I32)
    pos1 = off[e2] + c0[e2] + rank[:, 1].astype(I32)
    pos = jnp.stack([pos0, pos1], axis=1).reshape(-1)
    nt = _moe_rows(x1.shape[0]) // MOE_TM
    tile_start = jnp.arange(nt, dtype=I32) * MOE_TM
    tile_expert = jnp.minimum(jnp.sum(tile_start[:, None] >= ends[None, :], axis=1),
                              N_EXPERTS - 1).astype(I32)
    n_valid = (ends[-1:] // MOE_TM).astype(I32)
    xs = _moe_dispatch(pos, hp)
    ys = _moe_experts(tile_expert, n_valid, xs, wg, wu, wd)
    return _moe_combine(pos, x1, rw, ys)


def _head_consts():
    lane = jnp.arange(ATT_OUT)
    bd = jnp.where((lane[:, None] // ATT_HEAD_DIM) == (lane[None, :] // ATT_HEAD_DIM),
                   1.0 / ATT_HEAD_DIM, 0.0).astype(BF16)
    perm = (lane[:, None] == (lane[None, :] ^ (ATT_HEAD_DIM // 2))).astype(BF16)
    return bd, perm


def _head_gains(g):
    full = jnp.tile(g.astype(F32), ATT_SLOTS).reshape(1, ATT_OUT)
    half = ATT_HEAD_DIM // 2
    swapped = jnp.tile(jnp.concatenate([g[half:], g[:half]]).astype(F32), ATT_SLOTS).reshape(1, ATT_OUT)
    return full, swapped


def kernel(x, mem, positions, norm_mix, w_in, ssm_a_re, ssm_a_im, ssm_log_dt, ssm_b_re, ssm_b_im,
           ssm_c_re, ssm_c_im, ssm_d, ssm_w_glu, w_ssm_out, att_q_norm, att_k_norm, w_att_out,
           norm_mem, w_mem_kv, mem_q_norm, mem_k_norm, w_mem_out, w_o, norm_ffn, ffn_w_gate,
           ffn_w_up, ffn_w_down, moe_w_router, moe_w_gate, moe_w_up, moe_w_down):
    bsz, seq, d = x.shape
    ntok = bsz * seq
    depth = w_in.shape[0]
    c0 = SSM_WIDTH
    c1 = c0 + ATT_WIDTH
    c2 = c1 + ATT_WIDTH
    c3 = c2 + ATT_WIDTH
    c4 = c3 + X_WIDTH
    cos_t, sin_t = _rope_tables(positions)
    bd, perm = _head_consts()
    mem2d = mem.reshape(bsz * MEM_LEN, d)
    x2d = x.reshape(ntok, d)
    for i in range(depth):
        wi = w_in[i]
        g_mix = norm_mix[i].reshape(1, d)

        def qkv_cols(gi):
            sl = slice(gi * ATT_OUT, (gi + 1) * ATT_OUT)
            return jnp.concatenate([wi[:, c0:c1][:, sl], wi[:, c1:c2][:, sl], wi[:, c2:c3][:, sl]], axis=1)

        wmain = jnp.concatenate([qkv_cols(0), wi[:, c3:c4], wi[:, c4:]], axis=1).astype(BF16)
        wut = jnp.transpose(wi[:, :c0]).astype(BF16)
        ut, qkv0, memq, gates = _inproj_main(x2d, g_mix, wut, wmain)

        ops = _ssm_operators(ssm_a_re[i], ssm_a_im[i], ssm_log_dt[i], ssm_b_re[i], ssm_b_im[i],
                             ssm_c_re[i], ssm_c_im[i], ssm_d[i])
        yt = _ssm_scan(ut, ops, bsz, seq)

        gq, gqs = _head_gains(att_q_norm[i])
        gk, gks = _head_gains(att_k_norm[i])
        x3 = x2d.reshape(bsz, seq, d)
        att = []
        for gi, (_, dil) in enumerate(DIL_PAIRS):
            if dil == 1:
                qkv3 = qkv0.reshape(bsz, seq, QKV_W)
            else:
                qkv3 = _inproj_dil(x3, g_mix, qkv_cols(gi).astype(BF16), dil)
            att.extend(_dilated_attention_group(qkv3, cos_t, sin_t, gq, gqs, gk, gks, bd, perm,
                                                bsz, seq, dil))

        kmem, vmem = _memory_kv(mem2d, norm_mem[i].reshape(1, d), w_mem_kv[i].astype(BF16),
                                mem_k_norm[i].reshape(1, X_HEAD_DIM))

        wts = (jnp.transpose(ssm_w_glu[i]).astype(BF16), w_ssm_out[i].astype(BF16),
               w_att_out[i].astype(BF16), w_mem_out[i].astype(BF16), w_o[i].astype(BF16),
               mem_q_norm[i].reshape(1, X_HEAD_DIM), norm_ffn[i].reshape(1, d))
        j = i // 2
        if i % 2 == 0:
            x1, h2 = _merge(x2d, memq, gates, yt, att, kmem, vmem, wts, bsz, seq)
            x2d = _dense_ffn(h2, x1, ffn_w_gate[j].astype(BF16), ffn_w_up[j].astype(BF16),
                             ffn_w_down[j].astype(BF16))
        else:
            wr = jnp.zeros((d, LANES), F32).at[:, :N_EXPERTS].set(moe_w_router[j])
            wr_hi = wr.astype(BF16)
            wr_lo = (wr - wr_hi.astype(F32)).astype(BF16)
            x1, hp, ri, rw = _merge(x2d, memq, gates, yt, att, kmem, vmem, wts, bsz, seq,
                                    router=(wr_hi, wr_lo))
            x2d = _moe_ffn(x1, hp, ri, rw, moe_w_gate[j].astype(BF16), moe_w_up[j].astype(BF16),
                           moe_w_down[j].astype(BF16))
    return x2d.reshape(bsz, seq, d)
```

```python
import functools
import math

import jax
import jax.numpy as jnp
from jax import lax
from jax.experimental import pallas as pl
from jax.experimental.pallas import tpu as pltpu

F32 = jnp.float32
BF16 = jnp.bfloat16
I32 = jnp.int32
U32 = jnp.uint32

EPS = 1e-6
D_MODEL = 1024
MEM_LEN = 256
SSM_WIDTH = 512
SSM_GROUP = 16
SSM_GROUPS = 32
SSM_STATE = 64
ATT_HEAD_DIM = 64
ATT_SLOTS = 4
DIL_PAIRS = ((128, 1), (512, 4), (2048, 16))
ATT_WIDTH = 768
ATT_OUT = 256
BLOCK = 128
ROPE_THETA = 10000.0
X_HEADS = 4
X_HEAD_DIM = 128
X_WIDTH = 512
D_FF = 2816
N_EXPERTS = 8
D_FF_EXPERT = 3584

LANES = 128
SSM_CHUNK = 128
QKV_W = 3 * ATT_OUT
ZMAIN_W = 3 * QKV_W + X_WIDTH + 3 * D_MODEL
VMEM_LIMIT = 56 * 1024 * 1024


def _cparams(sem, vmem=VMEM_LIMIT):
    return pltpu.CompilerParams(dimension_semantics=sem, vmem_limit_bytes=vmem)


def _rms(x, g):
    ms = jnp.mean(x * x, axis=-1, keepdims=True)
    return x * lax.rsqrt(ms + EPS) * g


def _dot(a, b):
    return jnp.dot(a, b, preferred_element_type=F32)


def _dot_nt(a, b):
    return lax.dot_general(a, b, (((1,), (1,)), ((), ())), preferred_element_type=F32)


def _rope_body(pos_ref, inv_ref, sgn_ref, cos_ref, sin_ref):
    ang = pos_ref[...].astype(F32) * inv_ref[...]
    cos_ref[...] = jnp.cos(ang)
    sin_ref[...] = jnp.sin(ang) * sgn_ref[...]


def _rope_tables(positions):
    n = positions.size
    half = ATT_HEAD_DIM // 2
    inv = ROPE_THETA ** (-jnp.arange(half, dtype=F32) / half)
    inv_row = jnp.tile(inv, LANES // half).reshape(1, LANES)
    lane = jnp.arange(LANES)
    sgn_row = jnp.where((lane % ATT_HEAD_DIM) < half, -1.0, 1.0).astype(F32).reshape(1, LANES)
    tm = 2048
    return pl.pallas_call(
        _rope_body,
        out_shape=(jax.ShapeDtypeStruct((n, LANES), F32), jax.ShapeDtypeStruct((n, LANES), F32)),
        grid=(n // tm,),
        in_specs=[pl.BlockSpec((tm, 1), lambda i: (i, 0)),
                  pl.BlockSpec((1, LANES), lambda i: (0, 0)),
                  pl.BlockSpec((1, LANES), lambda i: (0, 0))],
        out_specs=(pl.BlockSpec((tm, LANES), lambda i: (i, 0)),
                   pl.BlockSpec((tm, LANES), lambda i: (i, 0))),
        compiler_params=_cparams(("parallel",)),
        name="rope_tables",
    )(positions.reshape(n, 1), inv_row, sgn_row)


def _col_chunks(width, step=512):
    return [(c, min(step, width - c)) for c in range(0, width, step)]


def _inproj_main_body(x_ref, g_ref, wut_ref, w_ref, ut_ref, qkv_ref, memq_ref, gate_ref):
    h = _rms(x_ref[...], g_ref[...]).astype(BF16)
    ut_ref[...] = _dot_nt(wut_ref[...], h).astype(BF16)
    col = 0
    for ref in (qkv_ref, memq_ref, gate_ref):
        for c, w in _col_chunks(ref.shape[1]):
            ref[:, c:c + w] = _dot(h, w_ref[:, col + c:col + c + w]).astype(BF16)
        col += ref.shape[1]


def _inproj_main(x2d, g, wut, wmain):
    n = x2d.shape[0]
    tm = 512
    row = lambda i: (i, 0)
    const = lambda i: (0, 0)
    widths = (3 * QKV_W, X_WIDTH, 3 * D_MODEL)
    return pl.pallas_call(
        _inproj_main_body,
        out_shape=(jax.ShapeDtypeStruct((SSM_WIDTH, n), BF16),)
        + tuple(jax.ShapeDtypeStruct((n, w), BF16) for w in widths),
        grid=(n // tm,),
        in_specs=[pl.BlockSpec((tm, D_MODEL), row),
                  pl.BlockSpec((1, D_MODEL), const),
                  pl.BlockSpec((SSM_WIDTH, D_MODEL), const),
                  pl.BlockSpec((D_MODEL, ZMAIN_W), const)],
        out_specs=(pl.BlockSpec((SSM_WIDTH, tm), lambda i: (0, i)),)
        + tuple(pl.BlockSpec((tm, w), row) for w in widths),
        compiler_params=_cparams(("parallel",)),
        name="inproj_main",
    )(x2d, g, wut, wmain)


def _qk_prep(xf, cos2, sin2, g, gs, bd, perm, scale):
    xb = xf.astype(BF16)
    ms = _dot((xf * xf).astype(BF16), bd)
    xs = _dot(xb, perm)
    y = lax.rsqrt(ms + EPS) * scale * (xf * (g * cos2) + xs * (gs * sin2))
    return y.astype(BF16)


def _attn_body(qkv_ref, halo_ref, cos_ref, sin_ref, cosh_ref, sinh_ref, gq_ref, gqs_ref,
               gk_ref, gks_ref, bd_ref, perm_ref, o_ref, lse_ref, sbuf, qbuf, kbuf, vbuf, *, dil, tq):
    j = pl.program_id(1)
    hrows = BLOCK * dil
    bd = bd_ref[...]
    perm = perm_ref[...]
    lane = lax.broadcasted_iota(I32, (BLOCK, LANES), 1)
    low = lane < ATT_HEAD_DIM
    qi = lax.broadcasted_iota(I32, (BLOCK, 2 * BLOCK), 0) + BLOCK
    ki = lax.broadcasted_iota(I32, (BLOCK, 2 * BLOCK), 1)
    off = qi - ki
    band = (off >= 0) & (off <= BLOCK)
    band_first = band & ((ki >= BLOCK) | (j > 0))
    for c in range(QKV_W // LANES):
        sbuf[c, 0:hrows, :] = halo_ref[:, c * LANES:(c + 1) * LANES].astype(F32)
        sbuf[c, hrows:, :] = qkv_ref[:, c * LANES:(c + 1) * LANES].astype(F32)

    def rows(start, size):
        return pl.ds(start, size, stride=dil) if dil > 1 else pl.ds(start, size)

    def planes(first, sel):
        return jnp.concatenate([sbuf[first, sel, :], sbuf[first + 1, sel, :]], axis=1)

    def residue(r, carry):
        cos = cos_ref[rows(r, tq), :]
        sin = sin_ref[rows(r, tq), :]
        cosh = cosh_ref[rows(r, BLOCK), :]
        sinh = sinh_ref[rows(r, BLOCK), :]
        cos2 = jnp.concatenate([cos, cos], axis=1)
        sin2 = jnp.concatenate([sin, sin], axis=1)
        cosh2 = jnp.concatenate([cosh, cosh], axis=1)
        sinh2 = jnp.concatenate([sinh, sinh], axis=1)
        cur = rows(hrows + r, tq)
        hal = rows(r, BLOCK)
        qbuf[...] = _qk_prep(planes(0, cur), cos2, sin2, gq_ref[...], gqs_ref[...],
                             bd, perm, ATT_HEAD_DIM ** -0.5)
        kbuf[0:BLOCK, :] = _qk_prep(planes(2, hal), cosh2, sinh2,
                                    gk_ref[...], gks_ref[...], bd, perm, 1.0)
        kbuf[BLOCK:, :] = _qk_prep(planes(2, cur), cos2, sin2,
                                   gk_ref[...], gks_ref[...], bd, perm, 1.0)
        vbuf[0:BLOCK, :] = planes(4, hal).astype(BF16)
        vbuf[BLOCK:, :] = planes(4, cur).astype(BF16)
        for s in range(tq // BLOCK):
            row0 = s * BLOCK
            valid = band_first if s == 0 else band
            dst = rows(r + row0 * dil, BLOCK)
            for p in range(ATT_OUT // LANES):
                cols = slice(p * LANES, (p + 1) * LANES)
                qp = qbuf[row0:row0 + BLOCK, cols]
                kp = kbuf[row0:row0 + 2 * BLOCK, cols]
                vp = vbuf[row0:row0 + 2 * BLOCK, cols]
                outs, lses = [], []
                for h in range(2):
                    qm = jnp.where(low if h == 0 else ~low, qp, jnp.zeros_like(qp))
                    sc = jnp.where(valid, _dot_nt(qm, kp), -1e30)
                    m = jnp.max(sc, axis=-1, keepdims=True)
                    pr = jnp.exp(sc - m)
                    den = jnp.sum(pr, axis=-1, keepdims=True)
                    outs.append(_dot(pr.astype(BF16), vp) / den)
                    lses.append(m + jnp.log(den))
                o_ref[p, dst, :] = jnp.where(low, outs[0], outs[1])
                lse_ref[p, dst, :] = jnp.where(low, lses[0], lses[1])
        return carry

    if dil == 1:
        residue(0, 0)
    else:
        lax.fori_loop(0, dil, residue, 0, unroll=2)


def _dilated_attention_group(qkv_all, gi, cos_t, sin_t, gq, gqs, gk, gks, bd, perm, bsz, seq, dil):
    ntok = bsz * seq
    hrows = BLOCK * dil
    tt = max(1024, hrows)
    tq = tt // dil
    nblk = seq // tt
    hpb = tt // hrows
    nhalo = seq // hrows

    def halo_blk(b, j):
        return b * nhalo + jnp.maximum(j * hpb - 1, 0)

    const = lambda b, j: (0, 0)
    return pl.pallas_call(
        functools.partial(_attn_body, dil=dil, tq=tq),
        out_shape=(jax.ShapeDtypeStruct((ATT_OUT // LANES, ntok, LANES), F32),
                   jax.ShapeDtypeStruct((ATT_OUT // LANES, ntok, LANES), F32)),
        grid=(bsz, nblk),
        in_specs=[pl.BlockSpec((tt, QKV_W), lambda b, j: (b * nblk + j, gi)),
                  pl.BlockSpec((hrows, QKV_W), lambda b, j: (halo_blk(b, j), gi)),
                  pl.BlockSpec((tt, LANES), lambda b, j: (b * nblk + j, 0)),
                  pl.BlockSpec((tt, LANES), lambda b, j: (b * nblk + j, 0)),
                  pl.BlockSpec((hrows, LANES), lambda b, j: (halo_blk(b, j), 0)),
                  pl.BlockSpec((hrows, LANES), lambda b, j: (halo_blk(b, j), 0)),
                  pl.BlockSpec((1, ATT_OUT), const), pl.BlockSpec((1, ATT_OUT), const),
                  pl.BlockSpec((1, ATT_OUT), const), pl.BlockSpec((1, ATT_OUT), const),
                  pl.BlockSpec((ATT_OUT, ATT_OUT), const), pl.BlockSpec((ATT_OUT, ATT_OUT), const)],
        out_specs=(pl.BlockSpec((ATT_OUT // LANES, tt, LANES), lambda b, j: (0, b * nblk + j, 0)),
                   pl.BlockSpec((ATT_OUT // LANES, tt, LANES), lambda b, j: (0, b * nblk + j, 0))),
        scratch_shapes=[pltpu.VMEM((QKV_W // LANES, hrows + tt, LANES), F32),
                        pltpu.VMEM((tq, ATT_OUT), BF16),
                        pltpu.VMEM((BLOCK + tq, ATT_OUT), BF16),
                        pltpu.VMEM((BLOCK + tq, ATT_OUT), BF16)],
        compiler_params=_cparams(("parallel", "arbitrary")),
        name=f"dilated_attn{dil}",
    )(qkv_all, qkv_all, cos_t, sin_t, cos_t, sin_t, gq, gqs, gk, gks, bd, perm)


def _ssm_body(u_ref, ktab_ref, w_ref, v_ref, lam_ref, dvec_ref, y_ref, m_ref, sloc_ref, sin_ref,
              *, bsz, cpb):
    tc = SSM_CHUNK
    row = lax.broadcasted_iota(I32, (tc, tc), 0)
    col = lax.broadcasted_iota(I32, (tc, tc), 1)
    causal = col >= row

    def build(cp, carry):
        r0 = pl.multiple_of(cp * tc, tc)
        for c in range(SSM_GROUP):
            kv = ktab_ref[0, pl.ds(cp * SSM_GROUP + c, 1), :]
            tile = pltpu.roll(jnp.broadcast_to(kv, (tc, tc)), 0, 1, stride=1, stride_axis=0)
            m_ref[pl.ds(r0, tc), c * tc:(c + 1) * tc] = jnp.where(causal, tile, 0.0).astype(BF16)
        return carry

    lax.fori_loop(0, SSM_GROUP, build, 0)

    u = jnp.concatenate([u_ref[c] for c in range(SSM_GROUP)], axis=1)
    y = _dot(u, m_ref[...])
    sloc_ref[...] = _dot(u, w_ref[0])

    a1 = lam_ref[0, 0:1, :]
    a2 = lam_ref[0, 1:2, :]
    s = jnp.zeros((bsz, 2 * SSM_STATE), F32)
    sin_ref[pl.ds(0, bsz, stride=cpb), :] = s
    for k in range(1, cpb):
        s = a1 * s + a2 * pltpu.roll(s, SSM_STATE, 1) + sloc_ref[pl.ds(k - 1, bsz, stride=cpb), :]
        sin_ref[pl.ds(k, bsz, stride=cpb), :] = s

    y = y + _dot(sin_ref[...].astype(BF16), v_ref[0]) + dvec_ref[0] * u.astype(F32)
    y = jax.nn.gelu(y, approximate=True)
    for c in range(SSM_GROUP):
        y_ref[c] = y[:, c * tc:(c + 1) * tc].astype(BF16)


def _ssm_operators(a_re, a_im, log_dt, b_re, b_im, c_re, c_im, d_skip):
    tc = SSM_CHUNK
    lam = lax.complex(a_re.astype(F32), a_im.astype(F32))
    dt = jnp.exp(log_dt.astype(F32))[:, None]
    lam_dt = lam * dt
    lam_bar = jnp.exp(lam_dt)
    b = lax.complex(b_re.astype(F32), b_im.astype(F32))
    b_bar = ((lam_bar - 1.0) / lam)[..., None] * b
    c = lax.complex(c_re.astype(F32), c_im.astype(F32))
    k = jnp.arange(tc + 1, dtype=F32)
    pw = jnp.exp(lam_dt[:, None, :] * k[None, :, None])
    ktab = jnp.einsum('gcp,gkp,gpd->gdck', c, pw[:, :tc], b_bar).real
    ktab = ktab.reshape(SSM_GROUPS, SSM_GROUP * SSM_GROUP, tc).astype(F32)
    wc = jnp.einsum('gjp,gpd->gdjp', pw[:, tc - 1::-1][:, :tc], b_bar)
    wc = wc.reshape(SSM_GROUPS, SSM_GROUP * tc, SSM_STATE)
    w = jnp.concatenate([wc.real, wc.imag], axis=-1).astype(BF16)
    vc = jnp.einsum('gcp,gtp->gpct', c, pw[:, 1:tc + 1]).reshape(SSM_GROUPS, SSM_STATE, SSM_GROUP * tc)
    v = jnp.concatenate([vc.real, -vc.imag], axis=1).astype(BF16)
    lt = pw[:, tc]
    lam_rows = jnp.stack([jnp.concatenate([lt.real, lt.real], -1),
                          jnp.concatenate([-lt.imag, lt.imag], -1)], axis=1).astype(F32)
    dvec = jnp.repeat(d_skip.astype(F32).reshape(SSM_GROUPS, SSM_GROUP), tc, axis=1)
    return ktab, w, v, lam_rows, dvec.reshape(SSM_GROUPS, 1, SSM_GROUP * tc)


def _ssm_scan(ut, ops, bsz, seq):
    ktab, w, v, lam_rows, dvec = ops
    ntok = bsz * seq
    tc = SSM_CHUNK
    nch = ntok // tc
    u3 = ut.reshape(SSM_WIDTH, nch, tc)
    gmap = lambda g: (g, 0, 0)
    y3 = pl.pallas_call(
        functools.partial(_ssm_body, bsz=bsz, cpb=seq // tc),
        out_shape=jax.ShapeDtypeStruct((SSM_WIDTH, nch, tc), BF16),
        grid=(SSM_GROUPS,),
        in_specs=[pl.BlockSpec((SSM_GROUP, nch, tc), gmap),
                  pl.BlockSpec((1, SSM_GROUP * SSM_GROUP, tc), gmap),
                  pl.BlockSpec((1, SSM_GROUP * tc, 2 * SSM_STATE), gmap),
                  pl.BlockSpec((1, 2 * SSM_STATE, SSM_GROUP * tc), gmap),
                  pl.BlockSpec((1, 2, 2 * SSM_STATE), gmap),
                  pl.BlockSpec((1, 1, SSM_GROUP * tc), gmap)],
        out_specs=pl.BlockSpec((SSM_GROUP, nch, tc), gmap),
        scratch_shapes=[pltpu.VMEM((SSM_GROUP * tc, SSM_GROUP * tc), BF16),
                        pltpu.VMEM((nch, 2 * SSM_STATE), F32),
                        pltpu.VMEM((nch, 2 * SSM_STATE), F32)],
        compiler_params=_cparams(("parallel",)),
        name="ssm_scan",
    )(u3, ktab, w, v, lam_rows, dvec)
    return y3.reshape(SSM_WIDTH, ntok)


def _memkv_body(mem_ref, g_ref, w_ref, gk_ref, k_ref, v_ref):
    h = _rms(mem_ref[...], g_ref[...]).astype(BF16)
    kv = _dot(h, w_ref[...])
    for hd in range(X_HEADS):
        cols = slice(hd * X_HEAD_DIM, (hd + 1) * X_HEAD_DIM)
        k_ref[:, cols] = _rms(kv[:, cols], gk_ref[...]).astype(BF16)
    v_ref[...] = kv[:, X_WIDTH:].astype(BF16)


def _memory_kv(mem2d, g, w_kv, gk):
    m = mem2d.shape[0]
    tm = MEM_LEN
    return pl.pallas_call(
        _memkv_body,
        out_shape=(jax.ShapeDtypeStruct((m, X_WIDTH), BF16), jax.ShapeDtypeStruct((m, X_WIDTH), BF16)),
        grid=(m // tm,),
        in_specs=[pl.BlockSpec((tm, D_MODEL), lambda i: (i, 0)),
                  pl.BlockSpec((1, D_MODEL), lambda i: (0, 0)),
                  pl.BlockSpec((D_MODEL, 2 * X_WIDTH), lambda i: (0, 0)),
                  pl.BlockSpec((1, X_HEAD_DIM), lambda i: (0, 0))],
        out_specs=(pl.BlockSpec((tm, X_WIDTH), lambda i: (i, 0)),
                   pl.BlockSpec((tm, X_WIDTH), lambda i: (i, 0))),
        compiler_params=_cparams(("parallel",)),
        name="memory_kv",
    )(mem2d, g, w_kv, gk)


def _pack_bf16_pairs(x):
    c = x.shape[1] // 2
    bits = pltpu.bitcast(x.astype(BF16).astype(F32), U32)
    return (bits[:, :c] & jnp.uint32(0xFFFF0000)) | (bits[:, c:] >> 16)


def _unpack_bf16_pairs(p):
    hi = pltpu.bitcast(p & jnp.uint32(0xFFFF0000), F32)
    lo = pltpu.bitcast(p << 16, F32)
    return jnp.concatenate([hi, lo], axis=1)


def _merge_body(*refs, moe):
    (x_ref, memq_ref, gate_ref, yt_ref, o0_ref, l0_ref, o1_ref, l1_ref, o2_ref, l2_ref,
     km_ref, vm_ref, wglut_ref, wso_ref, wao_ref, wmo_ref, wo_ref, gmq_ref, gffn_ref) = refs[:19]
    if moe:
        wrh_ref, wrl_ref, x1_ref, hp_ref, ri_ref, rw_ref = refs[19:]
    else:
        x1_ref, h2_ref = refs[19:]

    ga = _dot(wglut_ref[...], yt_ref[...])
    glu = ga[:SSM_WIDTH] * jax.nn.sigmoid(ga[SSM_WIDTH:])
    y_ssm = _dot(jnp.transpose(glu).astype(BF16), wso_ref[...])

    planes = []
    for p in range(ATT_OUT // LANES):
        l0, l1, l2 = l0_ref[p], l1_ref[p], l2_ref[p]
        mx = jnp.maximum(jnp.maximum(l0, l1), l2)
        e0, e1, e2 = jnp.exp(l0 - mx), jnp.exp(l1 - mx), jnp.exp(l2 - mx)
        planes.append((e0 * o0_ref[p] + e1 * o1_ref[p] + e2 * o2_ref[p]) / (e0 + e1 + e2))
    y_att = _dot(jnp.concatenate(planes, axis=1).astype(BF16), wao_ref[...])

    heads = []
    for hd in range(X_HEADS):
        cols = slice(hd * X_HEAD_DIM, (hd + 1) * X_HEAD_DIM)
        q = (_rms(memq_ref[:, cols].astype(F32), gmq_ref[...]) * (X_HEAD_DIM ** -0.5)).astype(BF16)
        sc = _dot_nt(q, km_ref[:, cols])
        m = jnp.max(sc, axis=-1, keepdims=True)
        pr = jnp.exp(sc - m)
        den = jnp.sum(pr, axis=-1, keepdims=True)
        heads.append(_dot(pr.astype(BF16), vm_ref[:, cols]) / den)
    y_mem = _dot(jnp.concatenate(heads, axis=1).astype(BF16), wmo_ref[...])

    d = D_MODEL
    merged = (jax.nn.sigmoid(gate_ref[:, 0:d].astype(F32)) * y_ssm
              + jax.nn.sigmoid(gate_ref[:, d:2 * d].astype(F32)) * y_att
              + jax.nn.sigmoid(gate_ref[:, 2 * d:3 * d].astype(F32)) * y_mem)
    x1 = x_ref[...] + _dot(merged.astype(BF16), wo_ref[...])
    x1_ref[...] = x1
    h2 = _rms(x1, gffn_ref[...])
    if not moe:
        h2_ref[...] = h2.astype(BF16)
        return

    hp_ref[...] = _pack_bf16_pairs(h2)
    hi = h2.astype(BF16)
    lo = (h2 - hi.astype(F32)).astype(BF16)
    logits = _dot(hi, wrh_ref[...]) + _dot(hi, wrl_ref[...]) + _dot(lo, wrh_ref[...])
    lane = lax.broadcasted_iota(I32, logits.shape, 1)
    lg = jnp.where(lane < N_EXPERTS, logits, -jnp.inf)
    v1 = jnp.max(lg, axis=-1, keepdims=True)
    i1 = jnp.min(jnp.where(lg == v1, lane, LANES), axis=-1, keepdims=True)
    lg2 = jnp.where(lane == i1, -jnp.inf, lg)
    v2 = jnp.max(lg2, axis=-1, keepdims=True)
    i2 = jnp.min(jnp.where(lg2 == v2, lane, LANES), axis=-1, keepdims=True)
    e = jnp.exp(v2 - v1)
    ri_ref[...] = jnp.where(lane == 0, i1, jnp.where(lane == 1, i2, 0))
    rw_ref[...] = jnp.where(lane == 0, 1.0 / (1.0 + e), jnp.where(lane == 1, e / (1.0 + e), 0.0))


def _merge(x2d, memq, gates, yt, att, kmem, vmem, wts, bsz, seq, router=None):
    n = x2d.shape[0]
    tm = 512
    tpb = seq // tm
    moe = router is not None
    row = lambda i: (i, 0)
    const = lambda i: (0, 0)
    in_specs = [pl.BlockSpec((tm, D_MODEL), row),
                pl.BlockSpec((tm, X_WIDTH), row),
                pl.BlockSpec((tm, 3 * D_MODEL), row),
                pl.BlockSpec((SSM_WIDTH, tm), lambda i: (0, i))]
    in_specs += [pl.BlockSpec((ATT_OUT // LANES, tm, LANES), lambda i: (0, i, 0))] * 6
    in_specs += [pl.BlockSpec((MEM_LEN, X_WIDTH), lambda i: (i // tpb, 0))] * 2
    wglut, wso, wao, wmo, wo, gmq, gffn = wts
    in_specs += [pl.BlockSpec(w.shape, const) for w in (wglut, wso, wao, wmo, wo, gmq, gffn)]
    args = [x2d, memq, gates, yt, *att, kmem, vmem, wglut, wso, wao, wmo, wo, gmq, gffn]
    if moe:
        in_specs += [pl.BlockSpec(router[0].shape, const)] * 2
        args += list(router)
        out_shape = (jax.ShapeDtypeStruct((n, D_MODEL), F32),
                     jax.ShapeDtypeStruct((n, D_MODEL // 2), U32),
                     jax.ShapeDtypeStruct((n, LANES), I32),
                     jax.ShapeDtypeStruct((n, LANES), F32))
        out_specs = (pl.BlockSpec((tm, D_MODEL), row), pl.BlockSpec((tm, D_MODEL // 2), row),
                     pl.BlockSpec((tm, LANES), row), pl.BlockSpec((tm, LANES), row))
    else:
        out_shape = (jax.ShapeDtypeStruct((n, D_MODEL), F32), jax.ShapeDtypeStruct((n, D_MODEL), BF16))
        out_specs = (pl.BlockSpec((tm, D_MODEL), row), pl.BlockSpec((tm, D_MODEL), row))
    return pl.pallas_call(
        functools.partial(_merge_body, moe=moe),
        out_shape=out_shape,
        grid=(n // tm,),
        in_specs=in_specs,
        out_specs=out_specs,
        compiler_params=_cparams(("parallel",)),
        name="merge_moe" if moe else "merge_dense",
    )(*args)


def _ffn_body(h_ref, x_ref, wg_ref, wu_ref, wd_ref, o_ref):
    f = pl.program_id(1)
    h = h_ref[...]
    a = _dot(h, wg_ref[...])
    act = (a * jax.nn.sigmoid(a) * _dot(h, wu_ref[...])).astype(BF16)
    part = _dot(act, wd_ref[...])

    @pl.when(f == 0)
    def _():
        o_ref[...] = x_ref[...] + part

    @pl.when(f > 0)
    def _():
        o_ref[...] += part


def _dense_ffn(h2, x1, wg, wu, wd):
    n = h2.shape[0]
    tm = 512
    tf = D_FF // 2
    return pl.pallas_call(
        _ffn_body,
        out_shape=jax.ShapeDtypeStruct((n, D_MODEL), F32),
        grid=(n // tm, D_FF // tf),
        in_specs=[pl.BlockSpec((tm, D_MODEL), lambda i, f: (i, 0)),
                  pl.BlockSpec((tm, D_MODEL), lambda i, f: (i, 0)),
                  pl.BlockSpec((D_MODEL, tf), lambda i, f: (0, f)),
                  pl.BlockSpec((D_MODEL, tf), lambda i, f: (0, f)),
                  pl.BlockSpec((tf, D_MODEL), lambda i, f: (f, 0))],
        out_specs=pl.BlockSpec((tm, D_MODEL), lambda i, f: (i, 0)),
        compiler_params=_cparams(("parallel", "arbitrary")),
        name="dense_ffn",
    )(h2, x1, wg, wu, wd)


MOE_TM = 512
PLAN_TB = 512


def _moe_rows(ntok):
    return 2 * ntok + N_EXPERTS * MOE_TM


def _plan_body(ri_ref, rank_ref, cnt_ref, carry_ref):
    i = pl.program_id(0)

    @pl.when(i == 0)
    def _():
        carry_ref[...] = jnp.zeros_like(carry_ref)

    ri = ri_ref[...]
    lane = lax.broadcasted_iota(I32, ri.shape, 1)
    e1 = ri[:, 0:1]
    e2 = ri[:, 1:2]
    oh = (jnp.where(lane < N_EXPERTS, e1, e2 + N_EXPERTS) == lane) & (lane < 2 * N_EXPERTS)
    ohf = jnp.where(oh, 1.0, 0.0)
    tr = lax.broadcasted_iota(I32, (PLAN_TB, PLAN_TB), 0)
    tcol = lax.broadcasted_iota(I32, (PLAN_TB, PLAN_TB), 1)
    tri = jnp.where(tcol < tr, 1.0, 0.0).astype(BF16)
    excl = _dot(tri, ohf.astype(BF16)) + carry_ref[...]
    mine = jnp.where(oh, excl, 0.0)
    r0 = jnp.sum(jnp.where(lane < N_EXPERTS, mine, 0.0), axis=-1, keepdims=True)
    r1 = jnp.sum(jnp.where(lane >= N_EXPERTS, mine, 0.0), axis=-1, keepdims=True)
    rank_ref[...] = jnp.where(lane == 0, r0, jnp.where(lane == 1, r1, 0.0))
    carry_ref[...] += jnp.sum(ohf, axis=0, keepdims=True)
    cnt_ref[...] = carry_ref[...]


def _moe_plan(ri):
    n = ri.shape[0]
    return pl.pallas_call(
        _plan_body,
        out_shape=(jax.ShapeDtypeStruct((n, LANES), F32), jax.ShapeDtypeStruct((1, LANES), F32)),
        grid=(n // PLAN_TB,),
        in_specs=[pl.BlockSpec((PLAN_TB, LANES), lambda i: (i, 0))],
        out_specs=(pl.BlockSpec((PLAN_TB, LANES), lambda i: (i, 0)),
                   pl.BlockSpec((1, LANES), lambda i: (0, 0))),
        scratch_shapes=[pltpu.VMEM((1, LANES), F32)],
        compiler_params=_cparams(("arbitrary",)),
        name="moe_plan",
    )(ri)


DISPATCH_TB = 512


def _row_copy(src, dst, s, d, sem):
    return pltpu.make_async_copy(src.at[pl.ds(s, 1)], dst.at[pl.ds(d, 1)], sem)


def _dispatch_body(pos_ref, h_ref, xs_in_ref, xs_ref, sem):
    del xs_in_ref
    base = pl.program_id(0) * DISPATCH_TB

    def issue(t, carry):
        tok = base + t
        _row_copy(h_ref, xs_ref, t, pos_ref[2 * tok], sem).start()
        _row_copy(h_ref, xs_ref, t, pos_ref[2 * tok + 1], sem).start()
        return carry

    lax.fori_loop(0, DISPATCH_TB, issue, 0, unroll=8)

    def drain(t, carry):
        _row_copy(h_ref, xs_ref, 0, 0, sem).wait()
        _row_copy(h_ref, xs_ref, 0, 0, sem).wait()
        return carry

    lax.fori_loop(0, DISPATCH_TB, drain, 0)


def _moe_dispatch(pos, hp):
    n, c = hp.shape
    xs0 = jnp.zeros((_moe_rows(n), c), U32)
    return pl.pallas_call(
        _dispatch_body,
        out_shape=jax.ShapeDtypeStruct(xs0.shape, U32),
        grid_spec=pltpu.PrefetchScalarGridSpec(
            num_scalar_prefetch=1,
            grid=(n // DISPATCH_TB,),
            in_specs=[pl.BlockSpec((DISPATCH_TB, c), lambda i, p: (i, 0)),
                      pl.BlockSpec(memory_space=pl.ANY)],
            out_specs=pl.BlockSpec(memory_space=pl.ANY),
            scratch_shapes=[pltpu.SemaphoreType.DMA(())]),
        input_output_aliases={2: 0},
        compiler_params=_cparams(("arbitrary",)),
        name="moe_dispatch",
    )(pos, hp, xs0)


COMBINE_TB = 256


def _combine_body(pos_ref, x_ref, rw_ref, ys_ref, o_ref, buf, sem):
    base = pl.program_id(0) * COMBINE_TB

    def issue(t, carry):
        tok = base + t
        _row_copy(ys_ref, buf.at[0], pos_ref[2 * tok], t, sem).start()
        _row_copy(ys_ref, buf.at[1], pos_ref[2 * tok + 1], t, sem).start()
        return carry

    lax.fori_loop(0, COMBINE_TB, issue, 0, unroll=8)

    def drain(t, carry):
        _row_copy(ys_ref, buf.at[0], 0, 0, sem).wait()
        _row_copy(ys_ref, buf.at[1], 0, 0, sem).wait()
        return carry

    lax.fori_loop(0, COMBINE_TB, drain, 0)
    rw = rw_ref[...]
    o_ref[...] = (x_ref[...] + rw[:, 0:1] * _unpack_bf16_pairs(buf[0])
                  + rw[:, 1:2] * _unpack_bf16_pairs(buf[1]))


def _moe_combine(pos, x1, rw, ys):
    n = x1.shape[0]
    c = ys.shape[1]
    return pl.pallas_call(
        _combine_body,
        out_shape=jax.ShapeDtypeStruct((n, D_MODEL), F32),
        grid_spec=pltpu.PrefetchScalarGridSpec(
            num_scalar_prefetch=1,
            grid=(n // COMBINE_TB,),
            in_specs=[pl.BlockSpec((COMBINE_TB, D_MODEL), lambda i, p: (i, 0)),
                      pl.BlockSpec((COMBINE_TB, LANES), lambda i, p: (i, 0)),
                      pl.BlockSpec(memory_space=pl.ANY)],
            out_specs=pl.BlockSpec((COMBINE_TB, D_MODEL), lambda i, p: (i, 0)),
            scratch_shapes=[pltpu.VMEM((2, COMBINE_TB, c), U32), pltpu.SemaphoreType.DMA(())]),
        compiler_params=_cparams(("arbitrary",)),
        name="moe_combine",
    )(pos, x1, rw, ys)


MOE_TF = 896


def _experts_body(te_ref, nv_ref, xs_ref, wg_ref, wu_ref, wd_ref, ys_ref, xb_ref, acc_ref):
    i = pl.program_id(0)
    f = pl.program_id(1)
    nf = pl.num_programs(1)

    @pl.when(i < nv_ref[0])
    def _():
        @pl.when(f == 0)
        def _():
            xb_ref[...] = _unpack_bf16_pairs(xs_ref[...]).astype(BF16)

        h = xb_ref[...]
        a = _dot(h, wg_ref[0])
        act = (a * jax.nn.sigmoid(a) * _dot(h, wu_ref[0])).astype(BF16)
        part = _dot(act, wd_ref[0])

        @pl.when(f == 0)
        def _():
            acc_ref[...] = part

        @pl.when(f > 0)
        def _():
            acc_ref[...] += part

        @pl.when(f == nf - 1)
        def _():
            ys_ref[...] = _pack_bf16_pairs(acc_ref[...])

    @pl.when((i >= nv_ref[0]) & (f == nf - 1))
    def _():
        ys_ref[...] = jnp.zeros_like(ys_ref)


def _moe_experts(tile_expert, n_valid, xs, wg, wu, wd):
    rows, c = xs.shape
    nt = rows // MOE_TM
    nf = D_FF_EXPERT // MOE_TF

    def tile(i, nv):
        return jnp.minimum(i, nv[0] - 1)

    def fblk(i, f, nv):
        return jnp.where(i < nv[0], f, nf - 1)

    return pl.pallas_call(
        _experts_body,
        out_shape=jax.ShapeDtypeStruct((rows, c), U32),
        grid_spec=pltpu.PrefetchScalarGridSpec(
            num_scalar_prefetch=2,
            grid=(nt, nf),
            in_specs=[pl.BlockSpec((MOE_TM, c), lambda i, f, te, nv: (tile(i, nv), 0)),
                      pl.BlockSpec((1, D_MODEL, MOE_TF),
                                   lambda i, f, te, nv: (te[tile(i, nv)], 0, fblk(i, f, nv))),
                      pl.BlockSpec((1, D_MODEL, MOE_TF),
                                   lambda i, f, te, nv: (te[tile(i, nv)], 0, fblk(i, f, nv))),
                      pl.BlockSpec((1, MOE_TF, D_MODEL),
                                   lambda i, f, te, nv: (te[tile(i, nv)], fblk(i, f, nv), 0))],
            out_specs=pl.BlockSpec((MOE_TM, c), lambda i, f, te, nv: (i, 0)),
            scratch_shapes=[pltpu.VMEM((MOE_TM, D_MODEL), BF16), pltpu.VMEM((MOE_TM, D_MODEL), F32)]),
        compiler_params=_cparams(("arbitrary", "arbitrary")),
        name="moe_experts",
    )(tile_expert, n_valid, xs, wg, wu, wd)


def _moe_ffn(x1, hp, ri, rw, wg, wu, wd):
    rank, cnt = _moe_plan(ri)
    c0 = cnt[0, :N_EXPERTS].astype(I32)
    c1 = cnt[0, N_EXPERTS:2 * N_EXPERTS].astype(I32)
    padded = ((c0 + c1 + MOE_TM - 1) // MOE_TM) * MOE_TM
    ends = jnp.cumsum(padded)
    off = ends - padded
    e1, e2 = ri[:, 0], ri[:, 1]
    pos0 = off[e1] + rank[:, 0].astype(I32)
    pos1 = off[e2] + c0[e2] + rank[:, 1].astype(I32)
    pos = jnp.stack([pos0, pos1], axis=1).reshape(-1)
    nt = _moe_rows(x1.shape[0]) // MOE_TM
    tile_start = jnp.arange(nt, dtype=I32) * MOE_TM
    tile_expert = jnp.minimum(jnp.sum(tile_start[:, None] >= ends[None, :], axis=1),
                              N_EXPERTS - 1).astype(I32)
    n_valid = (ends[-1:] // MOE_TM).astype(I32)
    xs = _moe_dispatch(pos, hp)
    ys = _moe_experts(tile_expert, n_valid, xs, wg, wu, wd)
    return _moe_combine(pos, x1, rw, ys)


def _head_consts():
    lane = jnp.arange(ATT_OUT)
    bd = jnp.where((lane[:, None] // ATT_HEAD_DIM) == (lane[None, :] // ATT_HEAD_DIM),
                   1.0 / ATT_HEAD_DIM, 0.0).astype(BF16)
    perm = (lane[:, None] == (lane[None, :] ^ (ATT_HEAD_DIM // 2))).astype(BF16)
    return bd, perm


def _head_gains(g):
    full = jnp.tile(g.astype(F32), ATT_SLOTS).reshape(1, ATT_OUT)
    half = ATT_HEAD_DIM // 2
    swapped = jnp.tile(jnp.concatenate([g[half:], g[:half]]).astype(F32), ATT_SLOTS).reshape(1, ATT_OUT)
    return full, swapped


def kernel(x, mem, positions, norm_mix, w_in, ssm_a_re, ssm_a_im, ssm_log_dt, ssm_b_re, ssm_b_im,
           ssm_c_re, ssm_c_im, ssm_d, ssm_w_glu, w_ssm_out, att_q_norm, att_k_norm, w_att_out,
           norm_mem, w_mem_kv, mem_q_norm, mem_k_norm, w_mem_out, w_o, norm_ffn, ffn_w_gate,
           ffn_w_up, ffn_w_down, moe_w_router, moe_w_gate, moe_w_up, moe_w_down):
    bsz, seq, d = x.shape
    ntok = bsz * seq
    depth = w_in.shape[0]
    c0 = SSM_WIDTH
    c1 = c0 + ATT_WIDTH
    c2 = c1 + ATT_WIDTH
    c3 = c2 + ATT_WIDTH
    c4 = c3 + X_WIDTH
    cos_t, sin_t = _rope_tables(positions)
    bd, perm = _head_consts()
    mem2d = mem.reshape(bsz * MEM_LEN, d)
    x2d = x.reshape(ntok, d)
    for i in range(depth):
        wi = w_in[i]
        g_mix = norm_mix[i].reshape(1, d)

        def qkv_cols(gi):
            sl = slice(gi * ATT_OUT, (gi + 1) * ATT_OUT)
            return jnp.concatenate([wi[:, c0:c1][:, sl], wi[:, c1:c2][:, sl], wi[:, c2:c3][:, sl]], axis=1)

        wmain = jnp.concatenate([qkv_cols(0), qkv_cols(1), qkv_cols(2), wi[:, c3:c4], wi[:, c4:]],
                                axis=1).astype(BF16)
        wut = jnp.transpose(wi[:, :c0]).astype(BF16)
        ut, qkv, memq, gates = _inproj_main(x2d, g_mix, wut, wmain)

        ops = _ssm_operators(ssm_a_re[i], ssm_a_im[i], ssm_log_dt[i], ssm_b_re[i], ssm_b_im[i],
                             ssm_c_re[i], ssm_c_im[i], ssm_d[i])
        yt = _ssm_scan(ut, ops, bsz, seq)

        gq, gqs = _head_gains(att_q_norm[i])
        gk, gks = _head_gains(att_k_norm[i])
        att = []
        for gi, (_, dil) in enumerate(DIL_PAIRS):
            att.extend(_dilated_attention_group(qkv, gi, cos_t, sin_t, gq, gqs, gk, gks, bd, perm,
                                                bsz, seq, dil))

        kmem, vmem = _memory_kv(mem2d, norm_mem[i].reshape(1, d), w_mem_kv[i].astype(BF16),
                                mem_k_norm[i].reshape(1, X_HEAD_DIM))

        wts = (jnp.transpose(ssm_w_glu[i]).astype(BF16), w_ssm_out[i].astype(BF16),
               w_att_out[i].astype(BF16), w_mem_out[i].astype(BF16), w_o[i].astype(BF16),
               mem_q_norm[i].reshape(1, X_HEAD_DIM), norm_ffn[i].reshape(1, d))
        j = i // 2
        if i % 2 == 0:
            x1, h2 = _merge(x2d, memq, gates, yt, att, kmem, vmem, wts, bsz, seq)
            x2d = _dense_ffn(h2, x1, ffn_w_gate[j].astype(BF16), ffn_w_up[j].astype(BF16),
                             ffn_w_down[j].astype(BF16))
        else:
            wr = jnp.zeros((d, LANES), F32).at[:, :N_EXPERTS].set(moe_w_router[j])
            wr_hi = wr.astype(BF16)
            wr_lo = (wr - wr_hi.astype(F32)).astype(BF16)
            x1, hp, ri, rw = _merge(x2d, memq, gates, yt, att, kmem, vmem, wts, bsz, seq,
                                    router=(wr_hi, wr_lo))
            x2d = _moe_ffn(x1, hp, ri, rw, moe_w_gate[j].astype(BF16), moe_w_up[j].astype(BF16),
                           moe_w_down[j].astype(BF16))
    return x2d.reshape(bsz, seq, d)
```

```python
import functools
import math

import jax
import jax.numpy as jnp
from jax import lax
from jax.experimental import pallas as pl
from jax.experimental.pallas import tpu as pltpu

F32 = jnp.float32
BF16 = jnp.bfloat16
I32 = jnp.int32
U32 = jnp.uint32

EPS = 1e-6
D_MODEL = 1024
MEM_LEN = 256
SSM_WIDTH = 512
SSM_GROUP = 16
SSM_GROUPS = 32
SSM_STATE = 64
ATT_HEAD_DIM = 64
ATT_SLOTS = 4
DIL_PAIRS = ((128, 1), (512, 4), (2048, 16))
ATT_WIDTH = 768
ATT_OUT = 256
BLOCK = 128
ROPE_THETA = 10000.0
X_HEADS = 4
X_HEAD_DIM = 128
X_WIDTH = 512
D_FF = 2816
N_EXPERTS = 8
D_FF_EXPERT = 3584

LANES = 128
SSM_CHUNK = 128
QKV_W = 3 * ATT_OUT
ZMAIN_W = 3 * QKV_W + X_WIDTH + 3 * D_MODEL
VMEM_LIMIT = 56 * 1024 * 1024


def _cparams(sem, vmem=VMEM_LIMIT):
    return pltpu.CompilerParams(dimension_semantics=sem, vmem_limit_bytes=vmem)


def _rms(x, g):
    ms = jnp.mean(x * x, axis=-1, keepdims=True)
    return x * lax.rsqrt(ms + EPS) * g


def _dot(a, b):
    return jnp.dot(a, b, preferred_element_type=F32)


def _dot_nt(a, b):
    return lax.dot_general(a, b, (((1,), (1,)), ((), ())), preferred_element_type=F32)


def _rope_body(pos_ref, inv_ref, sgn_ref, cos_ref, sin_ref):
    ang = pos_ref[...].astype(F32) * inv_ref[...]
    cos_ref[...] = jnp.cos(ang)
    sin_ref[...] = jnp.sin(ang) * sgn_ref[...]


def _rope_tables(positions):
    n = positions.size
    half = ATT_HEAD_DIM // 2
    inv = ROPE_THETA ** (-jnp.arange(half, dtype=F32) / half)
    inv_row = jnp.tile(inv, LANES // half).reshape(1, LANES)
    lane = jnp.arange(LANES)
    sgn_row = jnp.where((lane % ATT_HEAD_DIM) < half, -1.0, 1.0).astype(F32).reshape(1, LANES)
    tm = 2048
    return pl.pallas_call(
        _rope_body,
        out_shape=(jax.ShapeDtypeStruct((n, LANES), F32), jax.ShapeDtypeStruct((n, LANES), F32)),
        grid=(n // tm,),
        in_specs=[pl.BlockSpec((tm, 1), lambda i: (i, 0)),
                  pl.BlockSpec((1, LANES), lambda i: (0, 0)),
                  pl.BlockSpec((1, LANES), lambda i: (0, 0))],
        out_specs=(pl.BlockSpec((tm, LANES), lambda i: (i, 0)),
                   pl.BlockSpec((tm, LANES), lambda i: (i, 0))),
        compiler_params=_cparams(("parallel",)),
        name="rope_tables",
    )(positions.reshape(n, 1), inv_row, sgn_row)


def _col_chunks(width, step=512):
    return [(c, min(step, width - c)) for c in range(0, width, step)]


def _inproj_main_body(x_ref, g_ref, wut_ref, w_ref, ut_ref, qkv_ref, memq_ref, gate_ref):
    h = _rms(x_ref[...], g_ref[...]).astype(BF16)
    ut_ref[...] = _dot_nt(wut_ref[...], h).astype(BF16)
    col = 0
    for ref in (qkv_ref, memq_ref, gate_ref):
        for c, w in _col_chunks(ref.shape[1]):
            ref[:, c:c + w] = _dot(h, w_ref[:, col + c:col + c + w]).astype(BF16)
        col += ref.shape[1]


def _inproj_main(x2d, g, wut, wmain):
    n = x2d.shape[0]
    tm = 512
    row = lambda i: (i, 0)
    const = lambda i: (0, 0)
    widths = (3 * QKV_W, X_WIDTH, 3 * D_MODEL)
    return pl.pallas_call(
        _inproj_main_body,
        out_shape=(jax.ShapeDtypeStruct((SSM_WIDTH, n), BF16),)
        + tuple(jax.ShapeDtypeStruct((n, w), BF16) for w in widths),
        grid=(n // tm,),
        in_specs=[pl.BlockSpec((tm, D_MODEL), row),
                  pl.BlockSpec((1, D_MODEL), const),
                  pl.BlockSpec((SSM_WIDTH, D_MODEL), const),
                  pl.BlockSpec((D_MODEL, ZMAIN_W), const)],
        out_specs=(pl.BlockSpec((SSM_WIDTH, tm), lambda i: (0, i)),)
        + tuple(pl.BlockSpec((tm, w), row) for w in widths),
        compiler_params=_cparams(("parallel",)),
        name="inproj_main",
    )(x2d, g, wut, wmain)


def _qk_prep(xf, cos2, sin2, g, gs, bd, perm, scale):
    xb = xf.astype(BF16)
    ms = _dot((xf * xf).astype(BF16), bd)
    xs = _dot(xb, perm)
    y = lax.rsqrt(ms + EPS) * scale * (xf * (g * cos2) + xs * (gs * sin2))
    return y.astype(BF16)


def _attn_body(qkv_ref, halo_ref, cos_ref, sin_ref, cosh_ref, sinh_ref, gq_ref, gqs_ref,
               gk_ref, gks_ref, bd_ref, perm_ref, o_ref, lse_ref, sbuf, qbuf, kbuf, vbuf, *, dil, tq):
    j = pl.program_id(1)
    hrows = BLOCK * dil
    bd = bd_ref[...]
    perm = perm_ref[...]
    lane = lax.broadcasted_iota(I32, (BLOCK, LANES), 1)
    low = lane < ATT_HEAD_DIM
    qi = lax.broadcasted_iota(I32, (BLOCK, 2 * BLOCK), 0) + BLOCK
    ki = lax.broadcasted_iota(I32, (BLOCK, 2 * BLOCK), 1)
    off = qi - ki
    band = (off >= 0) & (off <= BLOCK)
    band_first = band & ((ki >= BLOCK) | (j > 0))
    for c in range(QKV_W // LANES):
        sbuf[c, 0:hrows, :] = halo_ref[:, c * LANES:(c + 1) * LANES].astype(F32)
        sbuf[c, hrows:, :] = qkv_ref[:, c * LANES:(c + 1) * LANES].astype(F32)

    def rows(start, size):
        return pl.ds(start, size, stride=dil) if dil > 1 else pl.ds(start, size)

    def planes(first, sel):
        return jnp.concatenate([sbuf[first, sel, :], sbuf[first + 1, sel, :]], axis=1)

    def prep(r, carry):
        cos = cos_ref[rows(r, tq), :]
        sin = sin_ref[rows(r, tq), :]
        cosh = cosh_ref[rows(r, BLOCK), :]
        sinh = sinh_ref[rows(r, BLOCK), :]
        cos2 = jnp.concatenate([cos, cos], axis=1)
        sin2 = jnp.concatenate([sin, sin], axis=1)
        cosh2 = jnp.concatenate([cosh, cosh], axis=1)
        sinh2 = jnp.concatenate([sinh, sinh], axis=1)
        cur = rows(hrows + r, tq)
        hal = rows(r, BLOCK)
        qbuf[r] = _qk_prep(planes(0, cur), cos2, sin2, gq_ref[...], gqs_ref[...],
                           bd, perm, ATT_HEAD_DIM ** -0.5)
        kbuf[r, 0:BLOCK, :] = _qk_prep(planes(2, hal), cosh2, sinh2,
                                       gk_ref[...], gks_ref[...], bd, perm, 1.0)
        kbuf[r, BLOCK:, :] = _qk_prep(planes(2, cur), cos2, sin2,
                                      gk_ref[...], gks_ref[...], bd, perm, 1.0)
        vbuf[r, 0:BLOCK, :] = planes(4, hal).astype(BF16)
        vbuf[r, BLOCK:, :] = planes(4, cur).astype(BF16)
        return carry

    if dil == 1:
        prep(0, 0)
    else:
        lax.fori_loop(0, dil, prep, 0, unroll=2)

    for r in range(dil):
        for s in range(tq // BLOCK):
            row0 = s * BLOCK
            valid = band_first if s == 0 else band
            dst = rows(r + row0 * dil, BLOCK)
            for p in range(ATT_OUT // LANES):
                cols = slice(p * LANES, (p + 1) * LANES)
                qp = qbuf[r, row0:row0 + BLOCK, cols]
                kp = kbuf[r, row0:row0 + 2 * BLOCK, cols]
                vp = vbuf[r, row0:row0 + 2 * BLOCK, cols]
                outs, lses = [], []
                for h in range(2):
                    qm = jnp.where(low if h == 0 else ~low, qp, jnp.zeros_like(qp))
                    sc = jnp.where(valid, _dot_nt(qm, kp), -1e30)
                    m = jnp.max(sc, axis=-1, keepdims=True)
                    pr = jnp.exp(sc - m)
                    den = jnp.sum(pr, axis=-1, keepdims=True)
                    outs.append(_dot(pr.astype(BF16), vp) / den)
                    lses.append(m + jnp.log(den))
                o_ref[p, dst, :] = jnp.where(low, outs[0], outs[1])
                lse_ref[p, dst, :] = jnp.where(low, lses[0], lses[1])


def _dilated_attention_group(qkv_all, gi, cos_t, sin_t, gq, gqs, gk, gks, bd, perm, bsz, seq, dil):
    ntok = bsz * seq
    hrows = BLOCK * dil
    tt = max(1024, hrows)
    tq = tt // dil
    nblk = seq // tt
    hpb = tt // hrows
    nhalo = seq // hrows

    def halo_blk(b, j):
        return b * nhalo + jnp.maximum(j * hpb - 1, 0)

    const = lambda b, j: (0, 0)
    return pl.pallas_call(
        functools.partial(_attn_body, dil=dil, tq=tq),
        out_shape=(jax.ShapeDtypeStruct((ATT_OUT // LANES, ntok, LANES), F32),
                   jax.ShapeDtypeStruct((ATT_OUT // LANES, ntok, LANES), F32)),
        grid=(bsz, nblk),
        in_specs=[pl.BlockSpec((tt, QKV_W), lambda b, j: (b * nblk + j, gi)),
                  pl.BlockSpec((hrows, QKV_W), lambda b, j: (halo_blk(b, j), gi)),
                  pl.BlockSpec((tt, LANES), lambda b, j: (b * nblk + j, 0)),
                  pl.BlockSpec((tt, LANES), lambda b, j: (b * nblk + j, 0)),
                  pl.BlockSpec((hrows, LANES), lambda b, j: (halo_blk(b, j), 0)),
                  pl.BlockSpec((hrows, LANES), lambda b, j: (halo_blk(b, j), 0)),
                  pl.BlockSpec((1, ATT_OUT), const), pl.BlockSpec((1, ATT_OUT), const),
                  pl.BlockSpec((1, ATT_OUT), const), pl.BlockSpec((1, ATT_OUT), const),
                  pl.BlockSpec((ATT_OUT, ATT_OUT), const), pl.BlockSpec((ATT_OUT, ATT_OUT), const)],
        out_specs=(pl.BlockSpec((ATT_OUT // LANES, tt, LANES), lambda b, j: (0, b * nblk + j, 0)),
                   pl.BlockSpec((ATT_OUT // LANES, tt, LANES), lambda b, j: (0, b * nblk + j, 0))),
        scratch_shapes=[pltpu.VMEM((QKV_W // LANES, hrows + tt, LANES), F32),
                        pltpu.VMEM((dil, tq, ATT_OUT), BF16),
                        pltpu.VMEM((dil, BLOCK + tq, ATT_OUT), BF16),
                        pltpu.VMEM((dil, BLOCK + tq, ATT_OUT), BF16)],
        compiler_params=_cparams(("parallel", "arbitrary")),
        name=f"dilated_attn{dil}",
    )(qkv_all, qkv_all, cos_t, sin_t, cos_t, sin_t, gq, gqs, gk, gks, bd, perm)


def _ssm_body(u_ref, ktab_ref, w_ref, v_ref, lam_ref, dvec_ref, y_ref, m_ref, m2_ref, sloc_ref,
              ssw_ref, sin_ref, yin_ref, *, bsz, cpb):
    tc = SSM_CHUNK
    row = lax.broadcasted_iota(I32, (tc, tc), 0)
    col = lax.broadcasted_iota(I32, (tc, tc), 1)
    causal = col >= row

    u = jnp.concatenate([u_ref[c] for c in range(SSM_GROUP)], axis=1)
    sloc = _dot(u, w_ref[0])
    sloc_ref[...] = sloc
    ssw_ref[...] = pltpu.roll(sloc, SSM_STATE, 1)

    a1 = lam_ref[0, 0:1, :]
    a2 = lam_ref[0, 1:2, :]
    s = jnp.zeros((bsz, 2 * SSM_STATE), F32)
    sw = s
    sin_ref[pl.ds(0, bsz, stride=cpb), :] = s
    for k in range(1, cpb):
        prev = pl.ds(k - 1, bsz, stride=cpb)
        s, sw = (a1 * s + a2 * sw + sloc_ref[prev, :], a1 * sw - a2 * s + ssw_ref[prev, :])
        sin_ref[pl.ds(k, bsz, stride=cpb), :] = s

    y_in = _dot(sin_ref[...].astype(BF16), v_ref[0]) + dvec_ref[0] * u.astype(F32)
    nblk = SSM_GROUP // 2
    for cb in range(nblk):
        yin_ref[cb] = y_in[:, 2 * cb * tc:2 * (cb + 1) * tc]

    def build(cb, dst):
        for cp in range(SSM_GROUP):
            for h in range(2):
                kv = ktab_ref[0, pl.ds(cp * SSM_GROUP + 2 * cb + h, 1), :]
                tile = pltpu.roll(jnp.broadcast_to(kv, (tc, tc)), 0, 1, stride=1, stride_axis=0)
                dst[cp * tc:(cp + 1) * tc, h * tc:(h + 1) * tc] = (
                    jnp.where(causal, tile, 0.0).astype(BF16))

    def multiply(cb, src):
        y = jax.nn.gelu(_dot(u, src[...]) + yin_ref[cb], approximate=True)
        y_ref[2 * cb] = y[:, :tc].astype(BF16)
        y_ref[2 * cb + 1] = y[:, tc:].astype(BF16)

    build(0, m_ref)

    def pair(i, carry):
        build(2 * i + 1, m2_ref)
        multiply(2 * i, m_ref)
        build(jnp.minimum(2 * i + 2, nblk - 1), m_ref)
        multiply(2 * i + 1, m2_ref)
        return carry

    lax.fori_loop(0, nblk // 2, pair, 0)


def _ssm_operators(a_re, a_im, log_dt, b_re, b_im, c_re, c_im, d_skip):
    tc = SSM_CHUNK
    lam = lax.complex(a_re.astype(F32), a_im.astype(F32))
    dt = jnp.exp(log_dt.astype(F32))[:, None]
    lam_dt = lam * dt
    lam_bar = jnp.exp(lam_dt)
    b = lax.complex(b_re.astype(F32), b_im.astype(F32))
    b_bar = ((lam_bar - 1.0) / lam)[..., None] * b
    c = lax.complex(c_re.astype(F32), c_im.astype(F32))
    k = jnp.arange(tc + 1, dtype=F32)
    pw = jnp.exp(lam_dt[:, None, :] * k[None, :, None])
    ktab = jnp.einsum('gcp,gkp,gpd->gdck', c, pw[:, :tc], b_bar).real
    ktab = ktab.reshape(SSM_GROUPS, SSM_GROUP * SSM_GROUP, tc).astype(F32)
    wc = jnp.einsum('gjp,gpd->gdjp', pw[:, tc - 1::-1][:, :tc], b_bar)
    wc = wc.reshape(SSM_GROUPS, SSM_GROUP * tc, SSM_STATE)
    w = jnp.concatenate([wc.real, wc.imag], axis=-1).astype(BF16)
    vc = jnp.einsum('gcp,gtp->gpct', c, pw[:, 1:tc + 1]).reshape(SSM_GROUPS, SSM_STATE, SSM_GROUP * tc)
    v = jnp.concatenate([vc.real, -vc.imag], axis=1).astype(BF16)
    lt = pw[:, tc]
    lam_rows = jnp.stack([jnp.concatenate([lt.real, lt.real], -1),
                          jnp.concatenate([-lt.imag, lt.imag], -1)], axis=1).astype(F32)
    dvec = jnp.repeat(d_skip.astype(F32).reshape(SSM_GROUPS, SSM_GROUP), tc, axis=1)
    return ktab, w, v, lam_rows, dvec.reshape(SSM_GROUPS, 1, SSM_GROUP * tc)


def _ssm_scan(ut, ops, bsz, seq):
    ktab, w, v, lam_rows, dvec = ops
    ntok = bsz * seq
    tc = SSM_CHUNK
    nch = ntok // tc
    u3 = ut.reshape(SSM_WIDTH, nch, tc)
    gmap = lambda g: (g, 0, 0)
    y3 = pl.pallas_call(
        functools.partial(_ssm_body, bsz=bsz, cpb=seq // tc),
        out_shape=jax.ShapeDtypeStruct((SSM_WIDTH, nch, tc), BF16),
        grid=(SSM_GROUPS,),
        in_specs=[pl.BlockSpec((SSM_GROUP, nch, tc), gmap),
                  pl.BlockSpec((1, SSM_GROUP * SSM_GROUP, tc), gmap),
                  pl.BlockSpec((1, SSM_GROUP * tc, 2 * SSM_STATE), gmap),
                  pl.BlockSpec((1, 2 * SSM_STATE, SSM_GROUP * tc), gmap),
                  pl.BlockSpec((1, 2, 2 * SSM_STATE), gmap),
                  pl.BlockSpec((1, 1, SSM_GROUP * tc), gmap)],
        out_specs=pl.BlockSpec((SSM_GROUP, nch, tc), gmap),
        scratch_shapes=[pltpu.VMEM((SSM_GROUP * tc, 2 * tc), BF16),
                        pltpu.VMEM((SSM_GROUP * tc, 2 * tc), BF16),
                        pltpu.VMEM((nch, 2 * SSM_STATE), F32),
                        pltpu.VMEM((nch, 2 * SSM_STATE), F32),
                        pltpu.VMEM((nch, 2 * SSM_STATE), F32),
                        pltpu.VMEM((SSM_GROUP // 2, nch, 2 * tc), F32)],
        compiler_params=_cparams(("parallel",)),
        name="ssm_scan",
    )(u3, ktab, w, v, lam_rows, dvec)
    return y3.reshape(SSM_WIDTH, ntok)


def _memkv_body(mem_ref, g_ref, w_ref, gk_ref, k_ref, v_ref):
    h = _rms(mem_ref[...], g_ref[...]).astype(BF16)
    kv = _dot(h, w_ref[...])
    for hd in range(X_HEADS):
        cols = slice(hd * X_HEAD_DIM, (hd + 1) * X_HEAD_DIM)
        k_ref[:, cols] = _rms(kv[:, cols], gk_ref[...]).astype(BF16)
    v_ref[...] = kv[:, X_WIDTH:].astype(BF16)


def _memory_kv(mem2d, g, w_kv, gk):
    m = mem2d.shape[0]
    tm = MEM_LEN
    return pl.pallas_call(
        _memkv_body,
        out_shape=(jax.ShapeDtypeStruct((m, X_WIDTH), BF16), jax.ShapeDtypeStruct((m, X_WIDTH), BF16)),
        grid=(m // tm,),
        in_specs=[pl.BlockSpec((tm, D_MODEL), lambda i: (i, 0)),
                  pl.BlockSpec((1, D_MODEL), lambda i: (0, 0)),
                  pl.BlockSpec((D_MODEL, 2 * X_WIDTH), lambda i: (0, 0)),
                  pl.BlockSpec((1, X_HEAD_DIM), lambda i: (0, 0))],
        out_specs=(pl.BlockSpec((tm, X_WIDTH), lambda i: (i, 0)),
                   pl.BlockSpec((tm, X_WIDTH), lambda i: (i, 0))),
        compiler_params=_cparams(("parallel",)),
        name="memory_kv",
    )(mem2d, g, w_kv, gk)


def _pack_bf16_pairs(x):
    c = x.shape[1] // 2
    bits = pltpu.bitcast(x.astype(BF16).astype(F32), U32)
    return (bits[:, :c] & jnp.uint32(0xFFFF0000)) | (bits[:, c:] >> 16)


def _unpack_bf16_pairs(p):
    hi = pltpu.bitcast(p & jnp.uint32(0xFFFF0000), F32)
    lo = pltpu.bitcast(p << 16, F32)
    return jnp.concatenate([hi, lo], axis=1)


def _merge_body(*refs, moe):
    (x_ref, memq_ref, gate_ref, yt_ref, o0_ref, l0_ref, o1_ref, l1_ref, o2_ref, l2_ref,
     km_ref, vm_ref, wglut_ref, wso_ref, wao_ref, wmo_ref, wo_ref, gmq_ref, gffn_ref) = refs[:19]
    if moe:
        wrh_ref, wrl_ref, x1_ref, hp_ref, ri_ref, rw_ref = refs[19:]
    else:
        x1_ref, h2_ref = refs[19:]

    ga = _dot(wglut_ref[...], yt_ref[...])
    glu = ga[:SSM_WIDTH] * jax.nn.sigmoid(ga[SSM_WIDTH:])
    y_ssm = _dot(jnp.transpose(glu).astype(BF16), wso_ref[...])

    planes = []
    for p in range(ATT_OUT // LANES):
        l0, l1, l2 = l0_ref[p], l1_ref[p], l2_ref[p]
        mx = jnp.maximum(jnp.maximum(l0, l1), l2)
        e0, e1, e2 = jnp.exp(l0 - mx), jnp.exp(l1 - mx), jnp.exp(l2 - mx)
        planes.append((e0 * o0_ref[p] + e1 * o1_ref[p] + e2 * o2_ref[p]) / (e0 + e1 + e2))
    y_att = _dot(jnp.concatenate(planes, axis=1).astype(BF16), wao_ref[...])

    heads = []
    for hd in range(X_HEADS):
        cols = slice(hd * X_HEAD_DIM, (hd + 1) * X_HEAD_DIM)
        q = (_rms(memq_ref[:, cols].astype(F32), gmq_ref[...]) * (X_HEAD_DIM ** -0.5)).astype(BF16)
        sc = _dot_nt(q, km_ref[:, cols])
        m = jnp.max(sc, axis=-1, keepdims=True)
        pr = jnp.exp(sc - m)
        den = jnp.sum(pr, axis=-1, keepdims=True)
        heads.append(_dot(pr.astype(BF16), vm_ref[:, cols]) / den)
    y_mem = _dot(jnp.concatenate(heads, axis=1).astype(BF16), wmo_ref[...])

    d = D_MODEL
    merged = (jax.nn.sigmoid(gate_ref[:, 0:d].astype(F32)) * y_ssm
              + jax.nn.sigmoid(gate_ref[:, d:2 * d].astype(F32)) * y_att
              + jax.nn.sigmoid(gate_ref[:, 2 * d:3 * d].astype(F32)) * y_mem)
    x1 = x_ref[...] + _dot(merged.astype(BF16), wo_ref[...])
    x1_ref[...] = x1
    h2 = _rms(x1, gffn_ref[...])
    if not moe:
        h2_ref[...] = h2.astype(BF16)
        return

    hp_ref[...] = _pack_bf16_pairs(h2)
    hi = h2.astype(BF16)
    lo = (h2 - hi.astype(F32)).astype(BF16)
    logits = _dot(hi, wrh_ref[...]) + _dot(hi, wrl_ref[...]) + _dot(lo, wrh_ref[...])
    lane = lax.broadcasted_iota(I32, logits.shape, 1)
    lg = jnp.where(lane < N_EXPERTS, logits, -jnp.inf)
    v1 = jnp.max(lg, axis=-1, keepdims=True)
    i1 = jnp.min(jnp.where(lg == v1, lane, LANES), axis=-1, keepdims=True)
    lg2 = jnp.where(lane == i1, -jnp.inf, lg)
    v2 = jnp.max(lg2, axis=-1, keepdims=True)
    i2 = jnp.min(jnp.where(lg2 == v2, lane, LANES), axis=-1, keepdims=True)
    e = jnp.exp(v2 - v1)
    ri_ref[...] = jnp.where(lane == 0, i1, jnp.where(lane == 1, i2, 0))
    rw_ref[...] = jnp.where(lane == 0, 1.0 / (1.0 + e), jnp.where(lane == 1, e / (1.0 + e), 0.0))


def _merge(x2d, memq, gates, yt, att, kmem, vmem, wts, bsz, seq, router=None):
    n = x2d.shape[0]
    tm = 512
    tpb = seq // tm
    moe = router is not None
    row = lambda i: (i, 0)
    const = lambda i: (0, 0)
    in_specs = [pl.BlockSpec((tm, D_MODEL), row),
                pl.BlockSpec((tm, X_WIDTH), row),
                pl.BlockSpec((tm, 3 * D_MODEL), row),
                pl.BlockSpec((SSM_WIDTH, tm), lambda i: (0, i))]
    in_specs += [pl.BlockSpec((ATT_OUT // LANES, tm, LANES), lambda i: (0, i, 0))] * 6
    in_specs += [pl.BlockSpec((MEM_LEN, X_WIDTH), lambda i: (i // tpb, 0))] * 2
    wglut, wso, wao, wmo, wo, gmq, gffn = wts
    in_specs += [pl.BlockSpec(w.shape, const) for w in (wglut, wso, wao, wmo, wo, gmq, gffn)]
    args = [x2d, memq, gates, yt, *att, kmem, vmem, wglut, wso, wao, wmo, wo, gmq, gffn]
    if moe:
        in_specs += [pl.BlockSpec(router[0].shape, const)] * 2
        args += list(router)
        out_shape = (jax.ShapeDtypeStruct((n, D_MODEL), F32),
                     jax.ShapeDtypeStruct((n, D_MODEL // 2), U32),
                     jax.ShapeDtypeStruct((n, LANES), I32),
                     jax.ShapeDtypeStruct((n, LANES), F32))
        out_specs = (pl.BlockSpec((tm, D_MODEL), row), pl.BlockSpec((tm, D_MODEL // 2), row),
                     pl.BlockSpec((tm, LANES), row), pl.BlockSpec((tm, LANES), row))
    else:
        out_shape = (jax.ShapeDtypeStruct((n, D_MODEL), F32), jax.ShapeDtypeStruct((n, D_MODEL), BF16))
        out_specs = (pl.BlockSpec((tm, D_MODEL), row), pl.BlockSpec((tm, D_MODEL), row))
    return pl.pallas_call(
        functools.partial(_merge_body, moe=moe),
        out_shape=out_shape,
        grid=(n // tm,),
        in_specs=in_specs,
        out_specs=out_specs,
        compiler_params=_cparams(("parallel",)),
        name="merge_moe" if moe else "merge_dense",
    )(*args)


FFN_CHUNK = 768


def _ffn_body(h_ref, x_ref, wg_ref, wu_ref, wd_ref, o_ref):
    h = h_ref[...]
    acc = x_ref[...]
    for c, w in _col_chunks(D_FF, FFN_CHUNK):
        a = _dot(h, wg_ref[:, c:c + w])
        act = (a * jax.nn.sigmoid(a) * _dot(h, wu_ref[:, c:c + w])).astype(BF16)
        acc = acc + _dot(act, wd_ref[c:c + w, :])
    o_ref[...] = acc


def _dense_ffn(h2, x1, wg, wu, wd):
    n = h2.shape[0]
    tm = 512
    row = lambda i: (i, 0)
    resident = lambda shape: pl.BlockSpec(shape, lambda i: (0, 0), pipeline_mode=pl.Buffered(1))
    return pl.pallas_call(
        _ffn_body,
        out_shape=jax.ShapeDtypeStruct((n, D_MODEL), F32),
        grid=(n // tm,),
        in_specs=[pl.BlockSpec((tm, D_MODEL), row),
                  pl.BlockSpec((tm, D_MODEL), row),
                  resident((D_MODEL, D_FF)), resident((D_MODEL, D_FF)), resident((D_FF, D_MODEL))],
        out_specs=pl.BlockSpec((tm, D_MODEL), row),
        compiler_params=_cparams(("parallel",)),
        name="dense_ffn",
    )(h2, x1, wg, wu, wd)


MOE_TM = 512
PLAN_TB = 512


def _moe_rows(ntok):
    return 2 * ntok + N_EXPERTS * MOE_TM


def _plan_body(ri_ref, rank_ref, cnt_ref, carry_ref):
    i = pl.program_id(0)

    @pl.when(i == 0)
    def _():
        carry_ref[...] = jnp.zeros_like(carry_ref)

    ri = ri_ref[...]
    lane = lax.broadcasted_iota(I32, ri.shape, 1)
    e1 = ri[:, 0:1]
    e2 = ri[:, 1:2]
    oh = (jnp.where(lane < N_EXPERTS, e1, e2 + N_EXPERTS) == lane) & (lane < 2 * N_EXPERTS)
    ohf = jnp.where(oh, 1.0, 0.0)
    tr = lax.broadcasted_iota(I32, (PLAN_TB, PLAN_TB), 0)
    tcol = lax.broadcasted_iota(I32, (PLAN_TB, PLAN_TB), 1)
    tri = jnp.where(tcol < tr, 1.0, 0.0).astype(BF16)
    excl = _dot(tri, ohf.astype(BF16)) + carry_ref[...]
    mine = jnp.where(oh, excl, 0.0)
    r0 = jnp.sum(jnp.where(lane < N_EXPERTS, mine, 0.0), axis=-1, keepdims=True)
    r1 = jnp.sum(jnp.where(lane >= N_EXPERTS, mine, 0.0), axis=-1, keepdims=True)
    rank_ref[...] = jnp.where(lane == 0, r0, jnp.where(lane == 1, r1, 0.0))
    carry_ref[...] += jnp.sum(ohf, axis=0, keepdims=True)
    cnt_ref[...] = carry_ref[...]


def _moe_plan(ri):
    n = ri.shape[0]
    return pl.pallas_call(
        _plan_body,
        out_shape=(jax.ShapeDtypeStruct((n, LANES), F32), jax.ShapeDtypeStruct((1, LANES), F32)),
        grid=(n // PLAN_TB,),
        in_specs=[pl.BlockSpec((PLAN_TB, LANES), lambda i: (i, 0))],
        out_specs=(pl.BlockSpec((PLAN_TB, LANES), lambda i: (i, 0)),
                   pl.BlockSpec((1, LANES), lambda i: (0, 0))),
        scratch_shapes=[pltpu.VMEM((1, LANES), F32)],
        compiler_params=_cparams(("arbitrary",)),
        name="moe_plan",
    )(ri)


DISPATCH_TB = 512


def _row_copy(src, dst, s, d, sem):
    return pltpu.make_async_copy(src.at[pl.ds(s, 1)], dst.at[pl.ds(d, 1)], sem)


def _dispatch_body(pos_ref, h_ref, xs_in_ref, xs_ref, sem):
    del xs_in_ref
    base = pl.program_id(0) * DISPATCH_TB

    def issue(t, carry):
        tok = base + t
        _row_copy(h_ref, xs_ref, t, pos_ref[2 * tok], sem).start(priority=0)
        _row_copy(h_ref, xs_ref, t, pos_ref[2 * tok + 1], sem).start(priority=1)
        return carry

    lax.fori_loop(0, DISPATCH_TB, issue, 0, unroll=8)

    for _ in range(2):
        pltpu.make_async_copy(h_ref, xs_ref.at[pl.ds(0, DISPATCH_TB)], sem).wait()


def _moe_dispatch(pos, hp):
    n, c = hp.shape
    xs0 = jnp.zeros((_moe_rows(n), c), U32)
    return pl.pallas_call(
        _dispatch_body,
        out_shape=jax.ShapeDtypeStruct(xs0.shape, U32),
        grid_spec=pltpu.PrefetchScalarGridSpec(
            num_scalar_prefetch=1,
            grid=(n // DISPATCH_TB,),
            in_specs=[pl.BlockSpec((DISPATCH_TB, c), lambda i, p: (i, 0)),
                      pl.BlockSpec(memory_space=pl.ANY)],
            out_specs=pl.BlockSpec(memory_space=pl.ANY),
            scratch_shapes=[pltpu.SemaphoreType.DMA(())]),
        input_output_aliases={2: 0},
        compiler_params=_cparams(("arbitrary",)),
        name="moe_dispatch",
    )(pos, hp, xs0)


COMBINE_TB = 256


def _combine_body(pos_ref, x_ref, rw_ref, ys_ref, o_ref, buf, sem):
    base = pl.program_id(0) * COMBINE_TB

    def issue(t, carry):
        tok = base + t
        _row_copy(ys_ref, buf.at[0], pos_ref[2 * tok], t, sem).start(priority=0)
        _row_copy(ys_ref, buf.at[1], pos_ref[2 * tok + 1], t, sem).start(priority=1)
        return carry

    lax.fori_loop(0, COMBINE_TB, issue, 0, unroll=8)

    for k in range(2):
        pltpu.make_async_copy(ys_ref.at[pl.ds(0, COMBINE_TB)], buf.at[k], sem).wait()
    rw = rw_ref[...]
    o_ref[...] = (x_ref[...] + rw[:, 0:1] * _unpack_bf16_pairs(buf[0])
                  + rw[:, 1:2] * _unpack_bf16_pairs(buf[1]))


def _moe_combine(pos, x1, rw, ys):
    n = x1.shape[0]
    c = ys.shape[1]
    return pl.pallas_call(
        _combine_body,
        out_shape=jax.ShapeDtypeStruct((n, D_MODEL), F32),
        grid_spec=pltpu.PrefetchScalarGridSpec(
            num_scalar_prefetch=1,
            grid=(n // COMBINE_TB,),
            in_specs=[pl.BlockSpec((COMBINE_TB, D_MODEL), lambda i, p: (i, 0)),
                      pl.BlockSpec((COMBINE_TB, LANES), lambda i, p: (i, 0)),
                      pl.BlockSpec(memory_space=pl.ANY)],
            out_specs=pl.BlockSpec((COMBINE_TB, D_MODEL), lambda i, p: (i, 0)),
            scratch_shapes=[pltpu.VMEM((2, COMBINE_TB, c), U32), pltpu.SemaphoreType.DMA(())]),
        compiler_params=_cparams(("arbitrary",)),
        name="moe_combine",
    )(pos, x1, rw, ys)


MOE_TF = 1792


def _experts_body(te_ref, nv_ref, xs_ref, wg_ref, wu_ref, wd_ref, ys_ref, xb_ref, acc_ref):
    i = pl.program_id(0)
    f = pl.program_id(1)
    nf = pl.num_programs(1)

    @pl.when(i < nv_ref[0])
    def _():
        @pl.when(f == 0)
        def _():
            xb_ref[...] = _unpack_bf16_pairs(xs_ref[...]).astype(BF16)

        h = xb_ref[...]
        part = None
        for c, w in _col_chunks(MOE_TF, 1024):
            a = _dot(h, wg_ref[0, :, c:c + w])
            act = (a * jax.nn.sigmoid(a) * _dot(h, wu_ref[0, :, c:c + w])).astype(BF16)
            pc = _dot(act, wd_ref[0, c:c + w, :])
            part = pc if part is None else part + pc

        @pl.when(f == 0)
        def _():
            acc_ref[...] = part

        @pl.when(f > 0)
        def _():
            acc_ref[...] += part

        @pl.when(f == nf - 1)
        def _():
            ys_ref[...] = _pack_bf16_pairs(acc_ref[...])

    @pl.when((i >= nv_ref[0]) & (f == nf - 1))
    def _():
        ys_ref[...] = jnp.zeros_like(ys_ref)


def _moe_experts(tile_expert, n_valid, xs, wg, wu, wd):
    rows, c = xs.shape
    nt = rows // MOE_TM
    nf = D_FF_EXPERT // MOE_TF

    def tile(i, nv):
        return jnp.minimum(i, nv[0] - 1)

    def fblk(i, f, nv):
        return jnp.where(i < nv[0], f, nf - 1)

    return pl.pallas_call(
        _experts_body,
        out_shape=jax.ShapeDtypeStruct((rows, c), U32),
        grid_spec=pltpu.PrefetchScalarGridSpec(
            num_scalar_prefetch=2,
            grid=(nt, nf),
            in_specs=[pl.BlockSpec((MOE_TM, c), lambda i, f, te, nv: (tile(i, nv), 0)),
                      pl.BlockSpec((1, D_MODEL, MOE_TF),
                                   lambda i, f, te, nv: (te[tile(i, nv)], 0, fblk(i, f, nv))),
                      pl.BlockSpec((1, D_MODEL, MOE_TF),
                                   lambda i, f, te, nv: (te[tile(i, nv)], 0, fblk(i, f, nv))),
                      pl.BlockSpec((1, MOE_TF, D_MODEL),
                                   lambda i, f, te, nv: (te[tile(i, nv)], fblk(i, f, nv), 0))],
            out_specs=pl.BlockSpec((MOE_TM, c), lambda i, f, te, nv: (i, 0)),
            scratch_shapes=[pltpu.VMEM((MOE_TM, D_MODEL), BF16), pltpu.VMEM((MOE_TM, D_MODEL), F32)]),
        compiler_params=_cparams(("arbitrary", "arbitrary")),
        name="moe_experts",
    )(tile_expert, n_valid, xs, wg, wu, wd)


def _moe_ffn(x1, hp, ri, rw, wg, wu, wd):
    rank, cnt = _moe_plan(ri)
    c0 = cnt[0, :N_EXPERTS].astype(I32)
    c1 = cnt[0, N_EXPERTS:2 * N_EXPERTS].astype(I32)
    padded = ((c0 + c1 + MOE_TM - 1) // MOE_TM) * MOE_TM
    ends = jnp.cumsum(padded)
    off = ends - padded
    e1, e2 = ri[:, 0], ri[:, 1]
    pos0 = off[e1] + rank[:, 0].astype(I32)
    pos1 = off[e2] + c0[e2] + rank[:, 1].astype(I32)
    pos = jnp.stack([pos0, pos1], axis=1).reshape(-1)
    nt = _moe_rows(x1.shape[0]) // MOE_TM
    tile_start = jnp.arange(nt, dtype=I32) * MOE_TM
    tile_expert = jnp.minimum(jnp.sum(tile_start[:, None] >= ends[None, :], axis=1),
                              N_EXPERTS - 1).astype(I32)
    n_valid = (ends[-1:] // MOE_TM).astype(I32)
    xs = _moe_dispatch(pos, hp)
    ys = _moe_experts(tile_expert, n_valid, xs, wg, wu, wd)
    return _moe_combine(pos, x1, rw, ys)


def _head_consts():
    lane = jnp.arange(ATT_OUT)
    bd = jnp.where((lane[:, None] // ATT_HEAD_DIM) == (lane[None, :] // ATT_HEAD_DIM),
                   1.0 / ATT_HEAD_DIM, 0.0).astype(BF16)
    perm = (lane[:, None] == (lane[None, :] ^ (ATT_HEAD_DIM // 2))).astype(BF16)
    return bd, perm


def _head_gains(g):
    full = jnp.tile(g.astype(F32), ATT_SLOTS).reshape(1, ATT_OUT)
    half = ATT_HEAD_DIM // 2
    swapped = jnp.tile(jnp.concatenate([g[half:], g[:half]]).astype(F32), ATT_SLOTS).reshape(1, ATT_OUT)
    return full, swapped


def kernel(x, mem, positions, norm_mix, w_in, ssm_a_re, ssm_a_im, ssm_log_dt, ssm_b_re, ssm_b_im,
           ssm_c_re, ssm_c_im, ssm_d, ssm_w_glu, w_ssm_out, att_q_norm, att_k_norm, w_att_out,
           norm_mem, w_mem_kv, mem_q_norm, mem_k_norm, w_mem_out, w_o, norm_ffn, ffn_w_gate,
           ffn_w_up, ffn_w_down, moe_w_router, moe_w_gate, moe_w_up, moe_w_down):
    bsz, seq, d = x.shape
    ntok = bsz * seq
    depth = w_in.shape[0]
    c0 = SSM_WIDTH
    c1 = c0 + ATT_WIDTH
    c2 = c1 + ATT_WIDTH
    c3 = c2 + ATT_WIDTH
    c4 = c3 + X_WIDTH
    cos_t, sin_t = _rope_tables(positions)
    bd, perm = _head_consts()
    mem2d = mem.reshape(bsz * MEM_LEN, d)
    x2d = x.reshape(ntok, d)
    for i in range(depth):
        wi = w_in[i]
        g_mix = norm_mix[i].reshape(1, d)

        def qkv_cols(gi):
            sl = slice(gi * ATT_OUT, (gi + 1) * ATT_OUT)
            return jnp.concatenate([wi[:, c0:c1][:, sl], wi[:, c1:c2][:, sl], wi[:, c2:c3][:, sl]], axis=1)

        wmain = jnp.concatenate([qkv_cols(0), qkv_cols(1), qkv_cols(2), wi[:, c3:c4], wi[:, c4:]],
                                axis=1).astype(BF16)
        wut = jnp.transpose(wi[:, :c0]).astype(BF16)
        ut, qkv, memq, gates = _inproj_main(x2d, g_mix, wut, wmain)

        ops = _ssm_operators(ssm_a_re[i], ssm_a_im[i], ssm_log_dt[i], ssm_b_re[i], ssm_b_im[i],
                             ssm_c_re[i], ssm_c_im[i], ssm_d[i])
        yt = _ssm_scan(ut, ops, bsz, seq)

        gq, gqs = _head_gains(att_q_norm[i])
        gk, gks = _head_gains(att_k_norm[i])
        att = []
        for gi, (_, dil) in enumerate(DIL_PAIRS):
            att.extend(_dilated_attention_group(qkv, gi, cos_t, sin_t, gq, gqs, gk, gks, bd, perm,
                                                bsz, seq, dil))

        kmem, vmem = _memory_kv(mem2d, norm_mem[i].reshape(1, d), w_mem_kv[i].astype(BF16),
                                mem_k_norm[i].reshape(1, X_HEAD_DIM))

        wts = (jnp.transpose(ssm_w_glu[i]).astype(BF16), w_ssm_out[i].astype(BF16),
               w_att_out[i].astype(BF16), w_mem_out[i].astype(BF16), w_o[i].astype(BF16),
               mem_q_norm[i].reshape(1, X_HEAD_DIM), norm_ffn[i].reshape(1, d))
        j = i // 2
        if i % 2 == 0:
            x1, h2 = _merge(x2d, memq, gates, yt, att, kmem, vmem, wts, bsz, seq)
            x2d = _dense_ffn(h2, x1, ffn_w_gate[j].astype(BF16), ffn_w_up[j].astype(BF16),
                             ffn_w_down[j].astype(BF16))
        else:
            wr = jnp.zeros((d, LANES), F32).at[:, :N_EXPERTS].set(moe_w_router[j])
            wr_hi = wr.astype(BF16)
            wr_lo = (wr - wr_hi.astype(F32)).astype(BF16)
            x1, hp, ri, rw = _merge(x2d, memq, gates, yt, att, kmem, vmem, wts, bsz, seq,
                                    router=(wr_hi, wr_lo))
            x2d = _moe_ffn(x1, hp, ri, rw, moe_w_gate[j].astype(BF16), moe_w_up[j].astype(BF16),
                           moe_w_down[j].astype(BF16))
    return x2d.reshape(bsz, seq, d)
```

```python
import functools
import math

import jax
import jax.numpy as jnp
from jax import lax
from jax.experimental import pallas as pl
from jax.experimental.pallas import tpu as pltpu

F32 = jnp.float32
BF16 = jnp.bfloat16
I32 = jnp.int32
U32 = jnp.uint32

EPS = 1e-6
D_MODEL = 1024
MEM_LEN = 256
SSM_WIDTH = 512
SSM_GROUP = 16
SSM_GROUPS = 32
SSM_STATE = 64
ATT_HEAD_DIM = 64
ATT_SLOTS = 4
DIL_PAIRS = ((128, 1), (512, 4), (2048, 16))
ATT_WIDTH = 768
ATT_OUT = 256
BLOCK = 128
ROPE_THETA = 10000.0
X_HEADS = 4
X_HEAD_DIM = 128
X_WIDTH = 512
D_FF = 2816
N_EXPERTS = 8
D_FF_EXPERT = 3584

LANES = 128
SSM_CHUNK = 128
QKV_W = 3 * ATT_OUT
VMEM_LIMIT = 56 * 1024 * 1024


def _cparams(sem, vmem=VMEM_LIMIT):
    return pltpu.CompilerParams(dimension_semantics=sem, vmem_limit_bytes=vmem)


def _rms(x, g):
    ms = jnp.mean(x * x, axis=-1, keepdims=True)
    return x * lax.rsqrt(ms + EPS) * g


def _dot(a, b):
    return jnp.dot(a, b, preferred_element_type=F32)


def _dot_nt(a, b):
    return lax.dot_general(a, b, (((1,), (1,)), ((), ())), preferred_element_type=F32)


def _rope_body(pos_ref, inv_ref, sgn_ref, cos_ref, sin_ref):
    ang = pos_ref[...].astype(F32) * inv_ref[...]
    cos_ref[...] = jnp.cos(ang)
    sin_ref[...] = jnp.sin(ang) * sgn_ref[...]


def _rope_tables(positions):
    n = positions.size
    half = ATT_HEAD_DIM // 2
    inv = ROPE_THETA ** (-jnp.arange(half, dtype=F32) / half)
    inv_row = jnp.tile(inv, LANES // half).reshape(1, LANES)
    lane = jnp.arange(LANES)
    sgn_row = jnp.where((lane % ATT_HEAD_DIM) < half, -1.0, 1.0).astype(F32).reshape(1, LANES)
    tm = 2048
    return pl.pallas_call(
        _rope_body,
        out_shape=(jax.ShapeDtypeStruct((n, LANES), F32), jax.ShapeDtypeStruct((n, LANES), F32)),
        grid=(n // tm,),
        in_specs=[pl.BlockSpec((tm, 1), lambda i: (i, 0)),
                  pl.BlockSpec((1, LANES), lambda i: (0, 0)),
                  pl.BlockSpec((1, LANES), lambda i: (0, 0))],
        out_specs=(pl.BlockSpec((tm, LANES), lambda i: (i, 0)),
                   pl.BlockSpec((tm, LANES), lambda i: (i, 0))),
        compiler_params=_cparams(("parallel",)),
        name="rope_tables",
    )(positions.reshape(n, 1), inv_row, sgn_row)


def _col_chunks(width, step=512):
    return [(c, min(step, width - c)) for c in range(0, width, step)]


def _inproj_main_body(x_ref, g_ref, wut_ref, w_ref, ut_ref, qkv_ref, memq_ref, gate_ref):
    h = _rms(x_ref[...], g_ref[...]).astype(BF16)
    ut_ref[...] = _dot_nt(wut_ref[...], h).astype(BF16)
    for gi in range(len(DIL_PAIRS)):
        for j in range(3):
            src = SSM_WIDTH + j * ATT_WIDTH + gi * ATT_OUT
            dst = gi * QKV_W + j * ATT_OUT
            qkv_ref[:, dst:dst + ATT_OUT] = _dot(h, w_ref[:, src:src + ATT_OUT]).astype(BF16)
    col = SSM_WIDTH + 3 * ATT_WIDTH
    for ref in (memq_ref, gate_ref):
        for c, w in _col_chunks(ref.shape[1]):
            ref[:, c:c + w] = _dot(h, w_ref[:, col + c:col + c + w]).astype(BF16)
        col += ref.shape[1]


def _inproj_main(x2d, g, wut, wmain):
    n = x2d.shape[0]
    tm = 512
    row = lambda i: (i, 0)
    const = lambda i: (0, 0)
    widths = (3 * QKV_W, X_WIDTH, 3 * D_MODEL)
    return pl.pallas_call(
        _inproj_main_body,
        out_shape=(jax.ShapeDtypeStruct((SSM_WIDTH, n), BF16),)
        + tuple(jax.ShapeDtypeStruct((n, w), BF16) for w in widths),
        grid=(n // tm,),
        in_specs=[pl.BlockSpec((tm, D_MODEL), row),
                  pl.BlockSpec((1, D_MODEL), const),
                  pl.BlockSpec((SSM_WIDTH, D_MODEL), const, pipeline_mode=pl.Buffered(1)),
                  pl.BlockSpec(wmain.shape, const, pipeline_mode=pl.Buffered(1))],
        out_specs=(pl.BlockSpec((SSM_WIDTH, tm), lambda i: (0, i)),)
        + tuple(pl.BlockSpec((tm, w), row) for w in widths),
        compiler_params=_cparams(("parallel",)),
        name="inproj_main",
    )(x2d, g, wut, wmain)


def _qk_prep(xf, cos2, sin2, g, gs, bd, perm, scale):
    xb = xf.astype(BF16)
    ms = _dot((xf * xf).astype(BF16), bd)
    xs = _dot(xb, perm)
    y = lax.rsqrt(ms + EPS) * scale * (xf * (g * cos2) + xs * (gs * sin2))
    return y.astype(BF16)


def _attn_body(qkv_ref, halo_ref, cos_ref, sin_ref, cosh_ref, sinh_ref, gq_ref, gqs_ref,
               gk_ref, gks_ref, bd_ref, perm_ref, o_ref, lse_ref, sbuf, qbuf, kbuf, vbuf, *, dil, tq):
    j = pl.program_id(1)
    hrows = BLOCK * dil
    bd = bd_ref[...]
    perm = perm_ref[...]
    lane = lax.broadcasted_iota(I32, (BLOCK, LANES), 1)
    low = lane < ATT_HEAD_DIM
    qi = lax.broadcasted_iota(I32, (BLOCK, 2 * BLOCK), 0) + BLOCK
    ki = lax.broadcasted_iota(I32, (BLOCK, 2 * BLOCK), 1)
    off = qi - ki
    band = (off >= 0) & (off <= BLOCK)
    band_first = band & ((ki >= BLOCK) | (j > 0))
    for c in range(QKV_W // LANES):
        sbuf[c, 0:hrows, :] = halo_ref[:, c * LANES:(c + 1) * LANES].astype(F32)
        sbuf[c, hrows:, :] = qkv_ref[:, c * LANES:(c + 1) * LANES].astype(F32)

    def rows(start, size):
        return pl.ds(start, size, stride=dil) if dil > 1 else pl.ds(start, size)

    def planes(first, sel):
        return jnp.concatenate([sbuf[first, sel, :], sbuf[first + 1, sel, :]], axis=1)

    def prep(r, carry):
        cos = cos_ref[rows(r, tq), :]
        sin = sin_ref[rows(r, tq), :]
        cosh = cosh_ref[rows(r, BLOCK), :]
        sinh = sinh_ref[rows(r, BLOCK), :]
        cos2 = jnp.concatenate([cos, cos], axis=1)
        sin2 = jnp.concatenate([sin, sin], axis=1)
        cosh2 = jnp.concatenate([cosh, cosh], axis=1)
        sinh2 = jnp.concatenate([sinh, sinh], axis=1)
        cur = rows(hrows + r, tq)
        hal = rows(r, BLOCK)
        qbuf[r] = _qk_prep(planes(0, cur), cos2, sin2, gq_ref[...], gqs_ref[...],
                           bd, perm, ATT_HEAD_DIM ** -0.5)
        kbuf[r, 0:BLOCK, :] = _qk_prep(planes(2, hal), cosh2, sinh2,
                                       gk_ref[...], gks_ref[...], bd, perm, 1.0)
        kbuf[r, BLOCK:, :] = _qk_prep(planes(2, cur), cos2, sin2,
                                      gk_ref[...], gks_ref[...], bd, perm, 1.0)
        vbuf[r, 0:BLOCK, :] = planes(4, hal).astype(BF16)
        vbuf[r, BLOCK:, :] = planes(4, cur).astype(BF16)
        return carry

    if dil == 1:
        prep(0, 0)
    else:
        lax.fori_loop(0, dil, prep, 0, unroll=2)

    for r in range(dil):
        for s in range(tq // BLOCK):
            row0 = s * BLOCK
            valid = band_first if s == 0 else band
            dst = rows(r + row0 * dil, BLOCK)
            for p in range(ATT_OUT // LANES):
                cols = slice(p * LANES, (p + 1) * LANES)
                qp = qbuf[r, row0:row0 + BLOCK, cols]
                kp = kbuf[r, row0:row0 + 2 * BLOCK, cols]
                vp = vbuf[r, row0:row0 + 2 * BLOCK, cols]
                outs, lses = [], []
                for h in range(2):
                    qm = jnp.where(low if h == 0 else ~low, qp, jnp.zeros_like(qp))
                    sc = jnp.where(valid, _dot_nt(qm, kp), -1e30)
                    m = jnp.max(sc, axis=-1, keepdims=True)
                    pr = jnp.exp(sc - m)
                    den = jnp.sum(pr, axis=-1, keepdims=True)
                    outs.append(_dot(pr.astype(BF16), vp) / den)
                    lses.append(m + jnp.log(den))
                o_ref[p, dst, :] = jnp.where(low, outs[0], outs[1])
                lse_ref[p, dst, :] = jnp.where(low, lses[0], lses[1])


def _dilated_attention_group(qkv_all, gi, cos_t, sin_t, gq, gqs, gk, gks, bd, perm, bsz, seq, dil):
    ntok = bsz * seq
    hrows = BLOCK * dil
    tt = max(1024, hrows)
    tq = tt // dil
    nblk = seq // tt
    hpb = tt // hrows
    nhalo = seq // hrows

    def halo_blk(b, j):
        return b * nhalo + jnp.maximum(j * hpb - 1, 0)

    const = lambda b, j: (0, 0)
    return pl.pallas_call(
        functools.partial(_attn_body, dil=dil, tq=tq),
        out_shape=(jax.ShapeDtypeStruct((ATT_OUT // LANES, ntok, LANES), F32),
                   jax.ShapeDtypeStruct((ATT_OUT // LANES, ntok, LANES), F32)),
        grid=(bsz, nblk),
        in_specs=[pl.BlockSpec((tt, QKV_W), lambda b, j: (b * nblk + j, gi)),
                  pl.BlockSpec((hrows, QKV_W), lambda b, j: (halo_blk(b, j), gi)),
                  pl.BlockSpec((tt, LANES), lambda b, j: (b * nblk + j, 0)),
                  pl.BlockSpec((tt, LANES), lambda b, j: (b * nblk + j, 0)),
                  pl.BlockSpec((hrows, LANES), lambda b, j: (halo_blk(b, j), 0)),
                  pl.BlockSpec((hrows, LANES), lambda b, j: (halo_blk(b, j), 0)),
                  pl.BlockSpec((1, ATT_OUT), const), pl.BlockSpec((1, ATT_OUT), const),
                  pl.BlockSpec((1, ATT_OUT), const), pl.BlockSpec((1, ATT_OUT), const),
                  pl.BlockSpec((ATT_OUT, ATT_OUT), const), pl.BlockSpec((ATT_OUT, ATT_OUT), const)],
        out_specs=(pl.BlockSpec((ATT_OUT // LANES, tt, LANES), lambda b, j: (0, b * nblk + j, 0)),
                   pl.BlockSpec((ATT_OUT // LANES, tt, LANES), lambda b, j: (0, b * nblk + j, 0))),
        scratch_shapes=[pltpu.VMEM((QKV_W // LANES, hrows + tt, LANES), F32),
                        pltpu.VMEM((dil, tq, ATT_OUT), BF16),
                        pltpu.VMEM((dil, BLOCK + tq, ATT_OUT), BF16),
                        pltpu.VMEM((dil, BLOCK + tq, ATT_OUT), BF16)],
        compiler_params=_cparams(("parallel", "arbitrary")),
        name=f"dilated_attn{dil}",
    )(qkv_all, qkv_all, cos_t, sin_t, cos_t, sin_t, gq, gqs, gk, gks, bd, perm)


def _ssm_body(u_ref, ktab_ref, w_ref, v_ref, lam_ref, dvec_ref, y_ref, m_ref, m2_ref, sloc_ref,
              ssw_ref, sin_ref, yin_ref, *, bsz, cpb):
    tc = SSM_CHUNK
    row = lax.broadcasted_iota(I32, (tc, tc), 0)
    col = lax.broadcasted_iota(I32, (tc, tc), 1)
    causal = col >= row

    u = jnp.concatenate([u_ref[c] for c in range(SSM_GROUP)], axis=1)
    sloc = _dot(u, w_ref[0])
    sloc_ref[...] = sloc
    ssw_ref[...] = pltpu.roll(sloc, SSM_STATE, 1)

    a1 = lam_ref[0, 0:1, :]
    a2 = lam_ref[0, 1:2, :]
    s = jnp.zeros((bsz, 2 * SSM_STATE), F32)
    sw = s
    sin_ref[pl.ds(0, bsz, stride=cpb), :] = s
    for k in range(1, cpb):
        prev = pl.ds(k - 1, bsz, stride=cpb)
        s, sw = (a1 * s + a2 * sw + sloc_ref[prev, :], a1 * sw - a2 * s + ssw_ref[prev, :])
        sin_ref[pl.ds(k, bsz, stride=cpb), :] = s

    y_in = _dot(sin_ref[...].astype(BF16), v_ref[0]) + dvec_ref[0] * u.astype(F32)
    nblk = SSM_GROUP // 2
    for cb in range(nblk):
        yin_ref[cb] = y_in[:, 2 * cb * tc:2 * (cb + 1) * tc]

    def build(cb, dst):
        for cp in range(SSM_GROUP):
            for h in range(2):
                kv = ktab_ref[0, pl.ds(cp * SSM_GROUP + 2 * cb + h, 1), :]
                tile = pltpu.roll(jnp.broadcast_to(kv, (tc, tc)), 0, 1, stride=1, stride_axis=0)
                dst[cp * tc:(cp + 1) * tc, h * tc:(h + 1) * tc] = (
                    jnp.where(causal, tile, 0.0).astype(BF16))

    def multiply(cb, src):
        y = jax.nn.gelu(_dot(u, src[...]) + yin_ref[cb], approximate=True)
        y_ref[2 * cb] = y[:, :tc].astype(BF16)
        y_ref[2 * cb + 1] = y[:, tc:].astype(BF16)

    build(0, m_ref)

    def pair(i, carry):
        build(2 * i + 1, m2_ref)
        multiply(2 * i, m_ref)
        build(jnp.minimum(2 * i + 2, nblk - 1), m_ref)
        multiply(2 * i + 1, m2_ref)
        return carry

    lax.fori_loop(0, nblk // 2, pair, 0)


def _ssm_operators(a_re, a_im, log_dt, b_re, b_im, c_re, c_im, d_skip):
    tc = SSM_CHUNK
    lam = lax.complex(a_re.astype(F32), a_im.astype(F32))
    dt = jnp.exp(log_dt.astype(F32))[:, None]
    lam_dt = lam * dt
    lam_bar = jnp.exp(lam_dt)
    b = lax.complex(b_re.astype(F32), b_im.astype(F32))
    b_bar = ((lam_bar - 1.0) / lam)[..., None] * b
    c = lax.complex(c_re.astype(F32), c_im.astype(F32))
    k = jnp.arange(tc + 1, dtype=F32)
    pw = jnp.exp(lam_dt[:, None, :] * k[None, :, None])
    ktab = jnp.einsum('gcp,gkp,gpd->gdck', c, pw[:, :tc], b_bar).real
    ktab = ktab.reshape(SSM_GROUPS, SSM_GROUP * SSM_GROUP, tc).astype(F32)
    wc = jnp.einsum('gjp,gpd->gdjp', pw[:, tc - 1::-1][:, :tc], b_bar)
    wc = wc.reshape(SSM_GROUPS, SSM_GROUP * tc, SSM_STATE)
    w = jnp.concatenate([wc.real, wc.imag], axis=-1).astype(BF16)
    vc = jnp.einsum('gcp,gtp->gpct', c, pw[:, 1:tc + 1]).reshape(SSM_GROUPS, SSM_STATE, SSM_GROUP * tc)
    v = jnp.concatenate([vc.real, -vc.imag], axis=1).astype(BF16)
    lt = pw[:, tc]
    lam_rows = jnp.stack([jnp.concatenate([lt.real, lt.real], -1),
                          jnp.concatenate([-lt.imag, lt.imag], -1)], axis=1).astype(F32)
    dvec = jnp.repeat(d_skip.astype(F32).reshape(SSM_GROUPS, SSM_GROUP), tc, axis=1)
    return ktab, w, v, lam_rows, dvec.reshape(SSM_GROUPS, 1, SSM_GROUP * tc)


def _ssm_scan(ut, ops, bsz, seq):
    ktab, w, v, lam_rows, dvec = ops
    ntok = bsz * seq
    tc = SSM_CHUNK
    nch = ntok // tc
    u3 = ut.reshape(SSM_WIDTH, nch, tc)
    gmap = lambda g: (g, 0, 0)
    y3 = pl.pallas_call(
        functools.partial(_ssm_body, bsz=bsz, cpb=seq // tc),
        out_shape=jax.ShapeDtypeStruct((SSM_WIDTH, nch, tc), BF16),
        grid=(SSM_GROUPS,),
        in_specs=[pl.BlockSpec((SSM_GROUP, nch, tc), gmap),
                  pl.BlockSpec((1, SSM_GROUP * SSM_GROUP, tc), gmap),
                  pl.BlockSpec((1, SSM_GROUP * tc, 2 * SSM_STATE), gmap),
                  pl.BlockSpec((1, 2 * SSM_STATE, SSM_GROUP * tc), gmap),
                  pl.BlockSpec((1, 2, 2 * SSM_STATE), gmap),
                  pl.BlockSpec((1, 1, SSM_GROUP * tc), gmap)],
        out_specs=pl.BlockSpec((SSM_GROUP, nch, tc), gmap),
        scratch_shapes=[pltpu.VMEM((SSM_GROUP * tc, 2 * tc), BF16),
                        pltpu.VMEM((SSM_GROUP * tc, 2 * tc), BF16),
                        pltpu.VMEM((nch, 2 * SSM_STATE), F32),
                        pltpu.VMEM((nch, 2 * SSM_STATE), F32),
                        pltpu.VMEM((nch, 2 * SSM_STATE), F32),
                        pltpu.VMEM((SSM_GROUP // 2, nch, 2 * tc), F32)],
        compiler_params=_cparams(("parallel",)),
        name="ssm_scan",
    )(u3, ktab, w, v, lam_rows, dvec)
    return y3.reshape(SSM_WIDTH, ntok)


def _memkv_body(mem_ref, g_ref, w_ref, gk_ref, k_ref, v_ref):
    h = _rms(mem_ref[...], g_ref[...]).astype(BF16)
    kv = _dot(h, w_ref[...])
    for hd in range(X_HEADS):
        cols = slice(hd * X_HEAD_DIM, (hd + 1) * X_HEAD_DIM)
        k_ref[:, cols] = _rms(kv[:, cols], gk_ref[...]).astype(BF16)
    v_ref[...] = kv[:, X_WIDTH:].astype(BF16)


def _memory_kv(mem2d, g, w_kv, gk):
    m = mem2d.shape[0]
    tm = MEM_LEN
    return pl.pallas_call(
        _memkv_body,
        out_shape=(jax.ShapeDtypeStruct((m, X_WIDTH), BF16), jax.ShapeDtypeStruct((m, X_WIDTH), BF16)),
        grid=(m // tm,),
        in_specs=[pl.BlockSpec((tm, D_MODEL), lambda i: (i, 0)),
                  pl.BlockSpec((1, D_MODEL), lambda i: (0, 0)),
                  pl.BlockSpec((D_MODEL, 2 * X_WIDTH), lambda i: (0, 0)),
                  pl.BlockSpec((1, X_HEAD_DIM), lambda i: (0, 0))],
        out_specs=(pl.BlockSpec((tm, X_WIDTH), lambda i: (i, 0)),
                   pl.BlockSpec((tm, X_WIDTH), lambda i: (i, 0))),
        compiler_params=_cparams(("parallel",)),
        name="memory_kv",
    )(mem2d, g, w_kv, gk)


def _pack_bf16_pairs(x):
    c = x.shape[1] // 2
    bits = pltpu.bitcast(x.astype(BF16).astype(F32), U32)
    return (bits[:, :c] & jnp.uint32(0xFFFF0000)) | (bits[:, c:] >> 16)


def _unpack_bf16_pairs(p):
    hi = pltpu.bitcast(p & jnp.uint32(0xFFFF0000), F32)
    lo = pltpu.bitcast(p << 16, F32)
    return jnp.concatenate([hi, lo], axis=1)


def _merge_body(*refs, moe):
    (x_ref, memq_ref, gate_ref, yt_ref, o0_ref, l0_ref, o1_ref, l1_ref, o2_ref, l2_ref,
     km_ref, vm_ref, wglut_ref, wso_ref, wao_ref, wmo_ref, wo_ref, gmq_ref, gffn_ref) = refs[:19]
    if moe:
        wrh_ref, wrl_ref, x1_ref, hp_ref, ri_ref, rw_ref = refs[19:]
    else:
        x1_ref, h2_ref = refs[19:]

    ga = _dot(wglut_ref[...], yt_ref[...])
    glu = ga[:SSM_WIDTH] * jax.nn.sigmoid(ga[SSM_WIDTH:])
    y_ssm = _dot(jnp.transpose(glu).astype(BF16), wso_ref[...])

    planes = []
    for p in range(ATT_OUT // LANES):
        l0, l1, l2 = l0_ref[p], l1_ref[p], l2_ref[p]
        mx = jnp.maximum(jnp.maximum(l0, l1), l2)
        e0, e1, e2 = jnp.exp(l0 - mx), jnp.exp(l1 - mx), jnp.exp(l2 - mx)
        planes.append((e0 * o0_ref[p] + e1 * o1_ref[p] + e2 * o2_ref[p]) / (e0 + e1 + e2))
    y_att = _dot(jnp.concatenate(planes, axis=1).astype(BF16), wao_ref[...])

    heads = []
    for hd in range(X_HEADS):
        cols = slice(hd * X_HEAD_DIM, (hd + 1) * X_HEAD_DIM)
        q = (_rms(memq_ref[:, cols].astype(F32), gmq_ref[...]) * (X_HEAD_DIM ** -0.5)).astype(BF16)
        sc = _dot_nt(q, km_ref[:, cols])
        m = jnp.max(sc, axis=-1, keepdims=True)
        pr = jnp.exp(sc - m)
        den = jnp.sum(pr, axis=-1, keepdims=True)
        heads.append(_dot(pr.astype(BF16), vm_ref[:, cols]) / den)
    y_mem = _dot(jnp.concatenate(heads, axis=1).astype(BF16), wmo_ref[...])

    d = D_MODEL
    merged = (jax.nn.sigmoid(gate_ref[:, 0:d].astype(F32)) * y_ssm
              + jax.nn.sigmoid(gate_ref[:, d:2 * d].astype(F32)) * y_att
              + jax.nn.sigmoid(gate_ref[:, 2 * d:3 * d].astype(F32)) * y_mem)
    x1 = x_ref[...] + _dot(merged.astype(BF16), wo_ref[...])
    x1_ref[...] = x1
    h2 = _rms(x1, gffn_ref[...])
    if not moe:
        h2_ref[...] = h2.astype(BF16)
        return

    hp_ref[...] = _pack_bf16_pairs(h2)
    hi = h2.astype(BF16)
    lo = (h2 - hi.astype(F32)).astype(BF16)
    logits = _dot(hi, wrh_ref[...]) + _dot(hi, wrl_ref[...]) + _dot(lo, wrh_ref[...])
    lane = lax.broadcasted_iota(I32, logits.shape, 1)
    lg = jnp.where(lane < N_EXPERTS, logits, -jnp.inf)
    v1 = jnp.max(lg, axis=-1, keepdims=True)
    i1 = jnp.min(jnp.where(lg == v1, lane, LANES), axis=-1, keepdims=True)
    lg2 = jnp.where(lane == i1, -jnp.inf, lg)
    v2 = jnp.max(lg2, axis=-1, keepdims=True)
    i2 = jnp.min(jnp.where(lg2 == v2, lane, LANES), axis=-1, keepdims=True)
    e = jnp.exp(v2 - v1)
    ri_ref[...] = jnp.where(lane == 0, i1, jnp.where(lane == 1, i2, 0))
    rw_ref[...] = jnp.where(lane == 0, 1.0 / (1.0 + e), jnp.where(lane == 1, e / (1.0 + e), 0.0))


def _merge(x2d, memq, gates, yt, att, kmem, vmem, wts, bsz, seq, router=None):
    n = x2d.shape[0]
    tm = 512
    tpb = seq // tm
    moe = router is not None
    row = lambda i: (i, 0)
    const = lambda i: (0, 0)
    in_specs = [pl.BlockSpec((tm, D_MODEL), row),
                pl.BlockSpec((tm, X_WIDTH), row),
                pl.BlockSpec((tm, 3 * D_MODEL), row),
                pl.BlockSpec((SSM_WIDTH, tm), lambda i: (0, i))]
    in_specs += [pl.BlockSpec((ATT_OUT // LANES, tm, LANES), lambda i: (0, i, 0))] * 6
    in_specs += [pl.BlockSpec((MEM_LEN, X_WIDTH), lambda i: (i // tpb, 0))] * 2
    wglut, wso, wao, wmo, wo, gmq, gffn = wts
    in_specs += [pl.BlockSpec(w.shape, const) for w in (wglut, wso, wao, wmo, wo, gmq, gffn)]
    args = [x2d, memq, gates, yt, *att, kmem, vmem, wglut, wso, wao, wmo, wo, gmq, gffn]
    if moe:
        in_specs += [pl.BlockSpec(router[0].shape, const)] * 2
        args += list(router)
        out_shape = (jax.ShapeDtypeStruct((n, D_MODEL), F32),
                     jax.ShapeDtypeStruct((n, D_MODEL // 2), U32),
                     jax.ShapeDtypeStruct((n, LANES), I32),
                     jax.ShapeDtypeStruct((n, LANES), F32))
        out_specs = (pl.BlockSpec((tm, D_MODEL), row), pl.BlockSpec((tm, D_MODEL // 2), row),
                     pl.BlockSpec((tm, LANES), row), pl.BlockSpec((tm, LANES), row))
    else:
        out_shape = (jax.ShapeDtypeStruct((n, D_MODEL), F32), jax.ShapeDtypeStruct((n, D_MODEL), BF16))
        out_specs = (pl.BlockSpec((tm, D_MODEL), row), pl.BlockSpec((tm, D_MODEL), row))
    return pl.pallas_call(
        functools.partial(_merge_body, moe=moe),
        out_shape=out_shape,
        grid=(n // tm,),
        in_specs=in_specs,
        out_specs=out_specs,
        compiler_params=_cparams(("parallel",)),
        name="merge_moe" if moe else "merge_dense",
    )(*args)


FFN_CHUNK = 768


def _ffn_body(h_ref, x_ref, wg_ref, wu_ref, wd_ref, o_ref):
    h = h_ref[...]
    acc = x_ref[...]
    for c, w in _col_chunks(D_FF, FFN_CHUNK):
        a = _dot(h, wg_ref[:, c:c + w])
        act = (a * jax.nn.sigmoid(a) * _dot(h, wu_ref[:, c:c + w])).astype(BF16)
        acc = acc + _dot(act, wd_ref[c:c + w, :])
    o_ref[...] = acc


def _dense_ffn(h2, x1, wg, wu, wd):
    n = h2.shape[0]
    tm = 512
    row = lambda i: (i, 0)
    resident = lambda shape: pl.BlockSpec(shape, lambda i: (0, 0), pipeline_mode=pl.Buffered(1))
    return pl.pallas_call(
        _ffn_body,
        out_shape=jax.ShapeDtypeStruct((n, D_MODEL), F32),
        grid=(n // tm,),
        in_specs=[pl.BlockSpec((tm, D_MODEL), row),
                  pl.BlockSpec((tm, D_MODEL), row),
                  resident((D_MODEL, D_FF)), resident((D_MODEL, D_FF)), resident((D_FF, D_MODEL))],
        out_specs=pl.BlockSpec((tm, D_MODEL), row),
        compiler_params=_cparams(("parallel",)),
        name="dense_ffn",
    )(h2, x1, wg, wu, wd)


MOE_TM = 512
PLAN_TB = 512


def _moe_rows(ntok):
    return 2 * ntok + N_EXPERTS * MOE_TM


def _plan_body(ri_ref, rank_ref, cnt_ref, carry_ref):
    i = pl.program_id(0)

    @pl.when(i == 0)
    def _():
        carry_ref[...] = jnp.zeros_like(carry_ref)

    ri = ri_ref[...]
    lane = lax.broadcasted_iota(I32, ri.shape, 1)
    e1 = ri[:, 0:1]
    e2 = ri[:, 1:2]
    oh = (jnp.where(lane < N_EXPERTS, e1, e2 + N_EXPERTS) == lane) & (lane < 2 * N_EXPERTS)
    ohf = jnp.where(oh, 1.0, 0.0)
    tr = lax.broadcasted_iota(I32, (PLAN_TB, PLAN_TB), 0)
    tcol = lax.broadcasted_iota(I32, (PLAN_TB, PLAN_TB), 1)
    tri = jnp.where(tcol < tr, 1.0, 0.0).astype(BF16)
    excl = _dot(tri, ohf.astype(BF16)) + carry_ref[...]
    mine = jnp.where(oh, excl, 0.0)
    r0 = jnp.sum(jnp.where(lane < N_EXPERTS, mine, 0.0), axis=-1, keepdims=True)
    r1 = jnp.sum(jnp.where(lane >= N_EXPERTS, mine, 0.0), axis=-1, keepdims=True)
    rank_ref[...] = jnp.where(lane == 0, r0, jnp.where(lane == 1, r1, 0.0))
    carry_ref[...] += jnp.sum(ohf, axis=0, keepdims=True)
    cnt_ref[...] = carry_ref[...]


def _moe_plan(ri):
    n = ri.shape[0]
    return pl.pallas_call(
        _plan_body,
        out_shape=(jax.ShapeDtypeStruct((n, LANES), F32), jax.ShapeDtypeStruct((1, LANES), F32)),
        grid=(n // PLAN_TB,),
        in_specs=[pl.BlockSpec((PLAN_TB, LANES), lambda i: (i, 0))],
        out_specs=(pl.BlockSpec((PLAN_TB, LANES), lambda i: (i, 0)),
                   pl.BlockSpec((1, LANES), lambda i: (0, 0))),
        scratch_shapes=[pltpu.VMEM((1, LANES), F32)],
        compiler_params=_cparams(("arbitrary",)),
        name="moe_plan",
    )(ri)


DISPATCH_TB = 512


SUBLANES = 8


def _wait_rows(hbm_ref, nrows, sem):
    whole = hbm_ref.at[pl.ds(0, nrows)]
    pltpu.make_async_copy(whole, whole, sem).wait()


def _dispatch_body(pos_ref, h_ref, xs_in_ref, xs_ref, sem):
    del xs_in_ref
    base = pl.program_id(0) * DISPATCH_TB

    def issue(g, carry):
        for u in range(SUBLANES):
            tok = base + g * SUBLANES + u
            for k in range(2):
                pltpu.make_async_copy(h_ref.at[g, pl.ds(u, 1)],
                                      xs_ref.at[pl.ds(pos_ref[2 * tok + k], 1)], sem).start(priority=k)
        return carry

    lax.fori_loop(0, DISPATCH_TB // SUBLANES, issue, 0)

    for _ in range(2):
        _wait_rows(xs_ref, DISPATCH_TB, sem)


def _moe_dispatch(pos, hp):
    n, c = hp.shape
    xs0 = jnp.zeros((_moe_rows(n), c), U32)
    hp = hp.reshape(n // SUBLANES, SUBLANES, c)
    return pl.pallas_call(
        _dispatch_body,
        out_shape=jax.ShapeDtypeStruct(xs0.shape, U32),
        grid_spec=pltpu.PrefetchScalarGridSpec(
            num_scalar_prefetch=1,
            grid=(n // DISPATCH_TB,),
            in_specs=[pl.BlockSpec((DISPATCH_TB // SUBLANES, SUBLANES, c), lambda i, p: (i, 0, 0)),
                      pl.BlockSpec(memory_space=pl.ANY)],
            out_specs=pl.BlockSpec(memory_space=pl.ANY),
            scratch_shapes=[pltpu.SemaphoreType.DMA(())]),
        input_output_aliases={2: 0},
        compiler_params=_cparams(("arbitrary",)),
        name="moe_dispatch",
    )(pos, hp, xs0)


COMBINE_TB = 256


def _combine_body(pos_ref, x_ref, rw_ref, ys_ref, o_ref, buf, sem):
    base = pl.program_id(0) * COMBINE_TB

    def issue(g, carry):
        for u in range(SUBLANES):
            tok = base + g * SUBLANES + u
            for k in range(2):
                pltpu.make_async_copy(ys_ref.at[pl.ds(pos_ref[2 * tok + k], 1)],
                                      buf.at[k, g, pl.ds(u, 1)], sem).start(priority=k)
        return carry

    lax.fori_loop(0, COMBINE_TB // SUBLANES, issue, 0)
    for _ in range(2):
        _wait_rows(ys_ref, COMBINE_TB, sem)
    rw = rw_ref[...]
    rows = lambda k: _unpack_bf16_pairs(buf[k].reshape(COMBINE_TB, buf.shape[-1]))
    o_ref[...] = x_ref[...] + rw[:, 0:1] * rows(0) + rw[:, 1:2] * rows(1)


def _moe_combine(pos, x1, rw, ys):
    n = x1.shape[0]
    c = ys.shape[1]
    return pl.pallas_call(
        _combine_body,
        out_shape=jax.ShapeDtypeStruct((n, D_MODEL), F32),
        grid_spec=pltpu.PrefetchScalarGridSpec(
            num_scalar_prefetch=1,
            grid=(n // COMBINE_TB,),
            in_specs=[pl.BlockSpec((COMBINE_TB, D_MODEL), lambda i, p: (i, 0)),
                      pl.BlockSpec((COMBINE_TB, LANES), lambda i, p: (i, 0)),
                      pl.BlockSpec(memory_space=pl.ANY)],
            out_specs=pl.BlockSpec((COMBINE_TB, D_MODEL), lambda i, p: (i, 0)),
            scratch_shapes=[pltpu.VMEM((2, COMBINE_TB // SUBLANES, SUBLANES, c), U32),
                            pltpu.SemaphoreType.DMA(())]),
        compiler_params=_cparams(("arbitrary",)),
        name="moe_combine",
    )(pos, x1, rw, ys)


MOE_TF = 1792


def _experts_body(te_ref, nv_ref, xs_ref, wg_ref, wu_ref, wd_ref, ys_ref, xb_ref, acc_ref):
    i = pl.program_id(0)
    f = pl.program_id(1)
    nf = pl.num_programs(1)

    @pl.when(i < nv_ref[0])
    def _():
        @pl.when(f == 0)
        def _():
            xb_ref[...] = _unpack_bf16_pairs(xs_ref[...]).astype(BF16)

        h = xb_ref[...]
        part = None
        for c, w in _col_chunks(MOE_TF, 1024):
            a = _dot(h, wg_ref[0, :, c:c + w])
            act = (a * jax.nn.sigmoid(a) * _dot(h, wu_ref[0, :, c:c + w])).astype(BF16)
            pc = _dot(act, wd_ref[0, c:c + w, :])
            part = pc if part is None else part + pc

        @pl.when(f == 0)
        def _():
            acc_ref[...] = part

        @pl.when(f > 0)
        def _():
            acc_ref[...] += part

        @pl.when(f == nf - 1)
        def _():
            ys_ref[...] = _pack_bf16_pairs(acc_ref[...])

    @pl.when((i >= nv_ref[0]) & (f == nf - 1))
    def _():
        ys_ref[...] = jnp.zeros_like(ys_ref)


def _moe_experts(tile_expert, n_valid, xs, wg, wu, wd):
    rows, c = xs.shape
    nt = rows // MOE_TM
    nf = D_FF_EXPERT // MOE_TF

    def tile(i, nv):
        return jnp.minimum(i, nv[0] - 1)

    def fblk(i, f, nv):
        return jnp.where(i < nv[0], f, nf - 1)

    return pl.pallas_call(
        _experts_body,
        out_shape=jax.ShapeDtypeStruct((rows, c), U32),
        grid_spec=pltpu.PrefetchScalarGridSpec(
            num_scalar_prefetch=2,
            grid=(nt, nf),
            in_specs=[pl.BlockSpec((MOE_TM, c), lambda i, f, te, nv: (tile(i, nv), 0)),
                      pl.BlockSpec((1, D_MODEL, MOE_TF),
                                   lambda i, f, te, nv: (te[tile(i, nv)], 0, fblk(i, f, nv))),
                      pl.BlockSpec((1, D_MODEL, MOE_TF),
                                   lambda i, f, te, nv: (te[tile(i, nv)], 0, fblk(i, f, nv))),
                      pl.BlockSpec((1, MOE_TF, D_MODEL),
                                   lambda i, f, te, nv: (te[tile(i, nv)], fblk(i, f, nv), 0))],
            out_specs=pl.BlockSpec((MOE_TM, c), lambda i, f, te, nv: (i, 0)),
            scratch_shapes=[pltpu.VMEM((MOE_TM, D_MODEL), BF16), pltpu.VMEM((MOE_TM, D_MODEL), F32)]),
        compiler_params=_cparams(("arbitrary", "arbitrary")),
        name="moe_experts",
    )(tile_expert, n_valid, xs, wg, wu, wd)


def _moe_ffn(x1, hp, ri, rw, wg, wu, wd):
    rank, cnt = _moe_plan(ri)
    c0 = cnt[0, :N_EXPERTS].astype(I32)
    c1 = cnt[0, N_EXPERTS:2 * N_EXPERTS].astype(I32)
    padded = ((c0 + c1 + MOE_TM - 1) // MOE_TM) * MOE_TM
    ends = jnp.cumsum(padded)
    off = ends - padded
    e1, e2 = ri[:, 0], ri[:, 1]
    pos0 = off[e1] + rank[:, 0].astype(I32)
    pos1 = off[e2] + c0[e2] + rank[:, 1].astype(I32)
    pos = jnp.stack([pos0, pos1], axis=1).reshape(-1)
    nt = _moe_rows(x1.shape[0]) // MOE_TM
    tile_start = jnp.arange(nt, dtype=I32) * MOE_TM
    tile_expert = jnp.minimum(jnp.sum(tile_start[:, None] >= ends[None, :], axis=1),
                              N_EXPERTS - 1).astype(I32)
    n_valid = (ends[-1:] // MOE_TM).astype(I32)
    xs = _moe_dispatch(pos, hp)
    ys = _moe_experts(tile_expert, n_valid, xs, wg, wu, wd)
    return _moe_combine(pos, x1, rw, ys)


def _head_consts():
    lane = jnp.arange(ATT_OUT)
    bd = jnp.where((lane[:, None] // ATT_HEAD_DIM) == (lane[None, :] // ATT_HEAD_DIM),
                   1.0 / ATT_HEAD_DIM, 0.0).astype(BF16)
    perm = (lane[:, None] == (lane[None, :] ^ (ATT_HEAD_DIM // 2))).astype(BF16)
    return bd, perm


def _head_gains(g):
    full = jnp.tile(g.astype(F32), ATT_SLOTS).reshape(1, ATT_OUT)
    half = ATT_HEAD_DIM // 2
    swapped = jnp.tile(jnp.concatenate([g[half:], g[:half]]).astype(F32), ATT_SLOTS).reshape(1, ATT_OUT)
    return full, swapped


def kernel(x, mem, positions, norm_mix, w_in, ssm_a_re, ssm_a_im, ssm_log_dt, ssm_b_re, ssm_b_im,
           ssm_c_re, ssm_c_im, ssm_d, ssm_w_glu, w_ssm_out, att_q_norm, att_k_norm, w_att_out,
           norm_mem, w_mem_kv, mem_q_norm, mem_k_norm, w_mem_out, w_o, norm_ffn, ffn_w_gate,
           ffn_w_up, ffn_w_down, moe_w_router, moe_w_gate, moe_w_up, moe_w_down):
    bsz, seq, d = x.shape
    ntok = bsz * seq
    depth = w_in.shape[0]
    cos_t, sin_t = _rope_tables(positions)
    bd, perm = _head_consts()
    mem2d = mem.reshape(bsz * MEM_LEN, d)
    x2d = x.reshape(ntok, d)
    for i in range(depth):
        wi = w_in[i]
        g_mix = norm_mix[i].reshape(1, d)

        wut = jnp.transpose(wi[:, :SSM_WIDTH]).astype(BF16)
        ut, qkv, memq, gates = _inproj_main(x2d, g_mix, wut, wi.astype(BF16))

        ops = _ssm_operators(ssm_a_re[i], ssm_a_im[i], ssm_log_dt[i], ssm_b_re[i], ssm_b_im[i],
                             ssm_c_re[i], ssm_c_im[i], ssm_d[i])
        yt = _ssm_scan(ut, ops, bsz, seq)

        gq, gqs = _head_gains(att_q_norm[i])
        gk, gks = _head_gains(att_k_norm[i])
        att = []
        for gi, (_, dil) in enumerate(DIL_PAIRS):
            att.extend(_dilated_attention_group(qkv, gi, cos_t, sin_t, gq, gqs, gk, gks, bd, perm,
                                                bsz, seq, dil))

        kmem, vmem = _memory_kv(mem2d, norm_mem[i].reshape(1, d), w_mem_kv[i].astype(BF16),
                                mem_k_norm[i].reshape(1, X_HEAD_DIM))

        wts = (jnp.transpose(ssm_w_glu[i]).astype(BF16), w_ssm_out[i].astype(BF16),
               w_att_out[i].astype(BF16), w_mem_out[i].astype(BF16), w_o[i].astype(BF16),
               mem_q_norm[i].reshape(1, X_HEAD_DIM), norm_ffn[i].reshape(1, d))
        j = i // 2
        if i % 2 == 0:
            x1, h2 = _merge(x2d, memq, gates, yt, att, kmem, vmem, wts, bsz, seq)
            x2d = _dense_ffn(h2, x1, ffn_w_gate[j].astype(BF16), ffn_w_up[j].astype(BF16),
                             ffn_w_down[j].astype(BF16))
        else:
            wr = jnp.zeros((d, LANES), F32).at[:, :N_EXPERTS].set(moe_w_router[j])
            wr_hi = wr.astype(BF16)
            wr_lo = (wr - wr_hi.astype(F32)).astype(BF16)
            x1, hp, ri, rw = _merge(x2d, memq, gates, yt, att, kmem, vmem, wts, bsz, seq,
                                    router=(wr_hi, wr_lo))
            x2d = _moe_ffn(x1, hp, ri, rw, moe_w_gate[j].astype(BF16), moe_w_up[j].astype(BF16),
                           moe_w_down[j].astype(BF16))
    return x2d.reshape(bsz, seq, d)
```

```python
import functools
import math

import jax
import jax.numpy as jnp
from jax import lax
from jax.experimental import pallas as pl
from jax.experimental.pallas import tpu as pltpu

F32 = jnp.float32
BF16 = jnp.bfloat16
I32 = jnp.int32
U32 = jnp.uint32

EPS = 1e-6
D_MODEL = 1024
MEM_LEN = 256
SSM_WIDTH = 512
SSM_GROUP = 16
SSM_GROUPS = 32
SSM_STATE = 64
ATT_HEAD_DIM = 64
ATT_SLOTS = 4
DIL_PAIRS = ((128, 1), (512, 4), (2048, 16))
ATT_WIDTH = 768
ATT_OUT = 256
BLOCK = 128
ROPE_THETA = 10000.0
X_HEADS = 4
X_HEAD_DIM = 128
X_WIDTH = 512
D_FF = 2816
N_EXPERTS = 8
D_FF_EXPERT = 3584

LANES = 128
SSM_CHUNK = 128
QKV_W = 3 * ATT_OUT
VMEM_LIMIT = 56 * 1024 * 1024


def _cparams(sem, vmem=VMEM_LIMIT):
    return pltpu.CompilerParams(dimension_semantics=sem, vmem_limit_bytes=vmem)


def _rms(x, g):
    ms = jnp.mean(x * x, axis=-1, keepdims=True)
    return x * lax.rsqrt(ms + EPS) * g


def _sigmoid(x):
    return 0.5 * jnp.tanh(0.5 * x) + 0.5


def _dot(a, b):
    return jnp.dot(a, b, preferred_element_type=F32)


def _dot_nt(a, b):
    return lax.dot_general(a, b, (((1,), (1,)), ((), ())), preferred_element_type=F32)


def _rope_body(pos_ref, inv_ref, sgn_ref, cos_ref, sin_ref):
    ang = pos_ref[...].astype(F32) * inv_ref[...]
    cos_ref[...] = jnp.cos(ang)
    sin_ref[...] = jnp.sin(ang) * sgn_ref[...]


def _rope_tables(positions):
    n = positions.size
    half = ATT_HEAD_DIM // 2
    inv = ROPE_THETA ** (-jnp.arange(half, dtype=F32) / half)
    inv_row = jnp.tile(inv, LANES // half).reshape(1, LANES)
    lane = jnp.arange(LANES)
    sgn_row = jnp.where((lane % ATT_HEAD_DIM) < half, -1.0, 1.0).astype(F32).reshape(1, LANES)
    tm = 2048
    return pl.pallas_call(
        _rope_body,
        out_shape=(jax.ShapeDtypeStruct((n, LANES), F32), jax.ShapeDtypeStruct((n, LANES), F32)),
        grid=(n // tm,),
        in_specs=[pl.BlockSpec((tm, 1), lambda i: (i, 0)),
                  pl.BlockSpec((1, LANES), lambda i: (0, 0)),
                  pl.BlockSpec((1, LANES), lambda i: (0, 0))],
        out_specs=(pl.BlockSpec((tm, LANES), lambda i: (i, 0)),
                   pl.BlockSpec((tm, LANES), lambda i: (i, 0))),
        compiler_params=_cparams(("parallel",)),
        name="rope_tables",
    )(positions.reshape(n, 1), inv_row, sgn_row)


def _col_chunks(width, step=512):
    return [(c, min(step, width - c)) for c in range(0, width, step)]


def _inproj_main_body(x_ref, g_ref, wut_ref, w_ref, ut_ref, qkv_ref, memq_ref, gate_ref):
    h = _rms(x_ref[...], g_ref[...]).astype(BF16)
    ut_ref[...] = _dot_nt(wut_ref[...], h).astype(BF16)
    for gi in range(len(DIL_PAIRS)):
        for j in range(3):
            src = SSM_WIDTH + j * ATT_WIDTH + gi * ATT_OUT
            dst = gi * QKV_W + j * ATT_OUT
            qkv_ref[:, dst:dst + ATT_OUT] = _dot(h, w_ref[:, src:src + ATT_OUT]).astype(BF16)
    col = SSM_WIDTH + 3 * ATT_WIDTH
    for ref in (memq_ref, gate_ref):
        for c, w in _col_chunks(ref.shape[1]):
            ref[:, c:c + w] = _dot(h, w_ref[:, col + c:col + c + w]).astype(BF16)
        col += ref.shape[1]


def _inproj_main(x2d, g, wut, wmain):
    n = x2d.shape[0]
    tm = 512
    row = lambda i: (i, 0)
    const = lambda i: (0, 0)
    widths = (3 * QKV_W, X_WIDTH, 3 * D_MODEL)
    return pl.pallas_call(
        _inproj_main_body,
        out_shape=(jax.ShapeDtypeStruct((SSM_WIDTH, n), BF16),)
        + tuple(jax.ShapeDtypeStruct((n, w), BF16) for w in widths),
        grid=(n // tm,),
        in_specs=[pl.BlockSpec((tm, D_MODEL), row),
                  pl.BlockSpec((1, D_MODEL), const),
                  pl.BlockSpec((SSM_WIDTH, D_MODEL), const, pipeline_mode=pl.Buffered(1)),
                  pl.BlockSpec(wmain.shape, const, pipeline_mode=pl.Buffered(1))],
        out_specs=(pl.BlockSpec((SSM_WIDTH, tm), lambda i: (0, i)),)
        + tuple(pl.BlockSpec((tm, w), row) for w in widths),
        compiler_params=_cparams(("parallel",)),
        name="inproj_main",
    )(x2d, g, wut, wmain)


def _qk_prep(xf, cos2, sin2, g, gs, bd, perm, scale):
    xb = xf.astype(BF16)
    ms = _dot((xf * xf).astype(BF16), bd)
    xs = _dot(xb, perm)
    y = lax.rsqrt(ms + EPS) * scale * (xf * (g * cos2) + xs * (gs * sin2))
    return y.astype(BF16)


def _attn_body(qkv_ref, halo_ref, cos_ref, sin_ref, cosh_ref, sinh_ref, gq_ref, gqs_ref,
               gk_ref, gks_ref, bd_ref, perm_ref, o_ref, lse_ref, sbuf, qbuf, kbuf, vbuf, *, dil, tq):
    j = pl.program_id(1)
    hrows = BLOCK * dil
    bd = bd_ref[...]
    perm = perm_ref[...]
    lane = lax.broadcasted_iota(I32, (BLOCK, LANES), 1)
    low = lane < ATT_HEAD_DIM
    qi = lax.broadcasted_iota(I32, (BLOCK, 2 * BLOCK), 0) + BLOCK
    ki = lax.broadcasted_iota(I32, (BLOCK, 2 * BLOCK), 1)
    off = qi - ki
    band = (off >= 0) & (off <= BLOCK)
    band_first = band & ((ki >= BLOCK) | (j > 0))
    for c in range(QKV_W // LANES):
        sbuf[c, 0:hrows, :] = halo_ref[:, c * LANES:(c + 1) * LANES].astype(F32)
        sbuf[c, hrows:, :] = qkv_ref[:, c * LANES:(c + 1) * LANES].astype(F32)

    def rows(start, size):
        return pl.ds(start, size, stride=dil) if dil > 1 else pl.ds(start, size)

    def planes(first, sel):
        return jnp.concatenate([sbuf[first, sel, :], sbuf[first + 1, sel, :]], axis=1)

    def prep(r, carry):
        cos = cos_ref[rows(r, tq), :]
        sin = sin_ref[rows(r, tq), :]
        cosh = cosh_ref[rows(r, BLOCK), :]
        sinh = sinh_ref[rows(r, BLOCK), :]
        cos2 = jnp.concatenate([cos, cos], axis=1)
        sin2 = jnp.concatenate([sin, sin], axis=1)
        cosh2 = jnp.concatenate([cosh, cosh], axis=1)
        sinh2 = jnp.concatenate([sinh, sinh], axis=1)
        cur = rows(hrows + r, tq)
        hal = rows(r, BLOCK)
        qbuf[r] = _qk_prep(planes(0, cur), cos2, sin2, gq_ref[...], gqs_ref[...],
                           bd, perm, ATT_HEAD_DIM ** -0.5)
        kbuf[r, 0:BLOCK, :] = _qk_prep(planes(2, hal), cosh2, sinh2,
                                       gk_ref[...], gks_ref[...], bd, perm, 1.0)
        kbuf[r, BLOCK:, :] = _qk_prep(planes(2, cur), cos2, sin2,
                                      gk_ref[...], gks_ref[...], bd, perm, 1.0)
        vbuf[r, 0:BLOCK, :] = planes(4, hal).astype(BF16)
        vbuf[r, BLOCK:, :] = planes(4, cur).astype(BF16)
        return carry

    if dil == 1:
        prep(0, 0)
    else:
        lax.fori_loop(0, dil, prep, 0, unroll=2)

    for r in range(dil):
        for s in range(tq // BLOCK):
            row0 = s * BLOCK
            valid = band_first if s == 0 else band
            dst = rows(r + row0 * dil, BLOCK)
            for p in range(ATT_OUT // LANES):
                cols = slice(p * LANES, (p + 1) * LANES)
                qp = qbuf[r, row0:row0 + BLOCK, cols]
                kp = kbuf[r, row0:row0 + 2 * BLOCK, cols]
                vp = vbuf[r, row0:row0 + 2 * BLOCK, cols]
                outs, lses = [], []
                for h in range(2):
                    qm = jnp.where(low if h == 0 else ~low, qp, jnp.zeros_like(qp))
                    sc = jnp.where(valid, _dot_nt(qm, kp), -1e30)
                    m = jnp.max(sc, axis=-1, keepdims=True)
                    pr = jnp.exp(sc - m)
                    den = jnp.sum(pr, axis=-1, keepdims=True)
                    outs.append(_dot(pr.astype(BF16), vp) / den)
                    lses.append(m + jnp.log(den))
                o_ref[p, dst, :] = jnp.where(low, outs[0], outs[1])
                lse_ref[p, dst, :] = jnp.where(low, lses[0], lses[1])


def _dilated_attention_group(qkv_all, gi, cos_t, sin_t, gq, gqs, gk, gks, bd, perm, bsz, seq, dil):
    ntok = bsz * seq
    hrows = BLOCK * dil
    tt = max(1024, hrows)
    tq = tt // dil
    nblk = seq // tt
    hpb = tt // hrows
    nhalo = seq // hrows

    def halo_blk(b, j):
        return b * nhalo + jnp.maximum(j * hpb - 1, 0)

    const = lambda b, j: (0, 0)
    return pl.pallas_call(
        functools.partial(_attn_body, dil=dil, tq=tq),
        out_shape=(jax.ShapeDtypeStruct((ATT_OUT // LANES, ntok, LANES), F32),
                   jax.ShapeDtypeStruct((ATT_OUT // LANES, ntok, LANES), F32)),
        grid=(bsz, nblk),
        in_specs=[pl.BlockSpec((tt, QKV_W), lambda b, j: (b * nblk + j, gi)),
                  pl.BlockSpec((hrows, QKV_W), lambda b, j: (halo_blk(b, j), gi)),
                  pl.BlockSpec((tt, LANES), lambda b, j: (b * nblk + j, 0)),
                  pl.BlockSpec((tt, LANES), lambda b, j: (b * nblk + j, 0)),
                  pl.BlockSpec((hrows, LANES), lambda b, j: (halo_blk(b, j), 0)),
                  pl.BlockSpec((hrows, LANES), lambda b, j: (halo_blk(b, j), 0)),
                  pl.BlockSpec((1, ATT_OUT), const), pl.BlockSpec((1, ATT_OUT), const),
                  pl.BlockSpec((1, ATT_OUT), const), pl.BlockSpec((1, ATT_OUT), const),
                  pl.BlockSpec((ATT_OUT, ATT_OUT), const), pl.BlockSpec((ATT_OUT, ATT_OUT), const)],
        out_specs=(pl.BlockSpec((ATT_OUT // LANES, tt, LANES), lambda b, j: (0, b * nblk + j, 0)),
                   pl.BlockSpec((ATT_OUT // LANES, tt, LANES), lambda b, j: (0, b * nblk + j, 0))),
        scratch_shapes=[pltpu.VMEM((QKV_W // LANES, hrows + tt, LANES), F32),
                        pltpu.VMEM((dil, tq, ATT_OUT), BF16),
                        pltpu.VMEM((dil, BLOCK + tq, ATT_OUT), BF16),
                        pltpu.VMEM((dil, BLOCK + tq, ATT_OUT), BF16)],
        compiler_params=_cparams(("parallel", "arbitrary")),
        name=f"dilated_attn{dil}",
    )(qkv_all, qkv_all, cos_t, sin_t, cos_t, sin_t, gq, gqs, gk, gks, bd, perm)


def _ssm_body(u_ref, ktab_ref, w_ref, v_ref, lam_ref, dvec_ref, y_ref, m_ref, m2_ref, sloc_ref,
              ssw_ref, sin_ref, yin_ref, *, bsz, cpb):
    tc = SSM_CHUNK
    row = lax.broadcasted_iota(I32, (tc, tc), 0)
    col = lax.broadcasted_iota(I32, (tc, tc), 1)
    causal = col >= row

    u = jnp.concatenate([u_ref[c] for c in range(SSM_GROUP)], axis=1)
    sloc = _dot(u, w_ref[0])
    sloc_ref[...] = sloc
    ssw_ref[...] = pltpu.roll(sloc, SSM_STATE, 1)

    a1 = lam_ref[0, 0:1, :]
    a2 = lam_ref[0, 1:2, :]
    s = jnp.zeros((bsz, 2 * SSM_STATE), F32)
    sw = s
    sin_ref[pl.ds(0, bsz, stride=cpb), :] = s
    for k in range(1, cpb):
        prev = pl.ds(k - 1, bsz, stride=cpb)
        s, sw = (a1 * s + a2 * sw + sloc_ref[prev, :], a1 * sw - a2 * s + ssw_ref[prev, :])
        sin_ref[pl.ds(k, bsz, stride=cpb), :] = s

    s_in = sin_ref[...].astype(BF16)
    nblk = SSM_GROUP // 2

    def y_in(c):
        return _dot(s_in, v_ref[0, c]) + dvec_ref[0, :, c * tc:(c + 1) * tc] * u_ref[c].astype(F32)

    for cb in range(nblk):
        yin_ref[cb] = jnp.concatenate([y_in(2 * cb), y_in(2 * cb + 1)], axis=1)

    def build(cb, dst):
        for cp in range(SSM_GROUP):
            for h in range(2):
                kv = ktab_ref[0, pl.ds(cp * SSM_GROUP + 2 * cb + h, 1), :]
                tile = pltpu.roll(jnp.broadcast_to(kv, (tc, tc)), 0, 1, stride=1, stride_axis=0)
                dst[cp * tc:(cp + 1) * tc, h * tc:(h + 1) * tc] = (
                    jnp.where(causal, tile, 0.0).astype(BF16))

    def multiply(cb, src):
        y = jax.nn.gelu(_dot(u, src[...]) + yin_ref[cb], approximate=True)
        y_ref[2 * cb] = y[:, :tc].astype(BF16)
        y_ref[2 * cb + 1] = y[:, tc:].astype(BF16)

    build(0, m_ref)

    def pair(i, carry):
        build(2 * i + 1, m2_ref)
        multiply(2 * i, m_ref)
        build(jnp.minimum(2 * i + 2, nblk - 1), m_ref)
        multiply(2 * i + 1, m2_ref)
        return carry

    lax.fori_loop(0, nblk // 2, pair, 0)


def _ssm_operators(a_re, a_im, log_dt, b_re, b_im, c_re, c_im, d_skip):
    tc = SSM_CHUNK
    lam = lax.complex(a_re.astype(F32), a_im.astype(F32))
    dt = jnp.exp(log_dt.astype(F32))[:, None]
    lam_dt = lam * dt
    lam_bar = jnp.exp(lam_dt)
    b = lax.complex(b_re.astype(F32), b_im.astype(F32))
    b_bar = ((lam_bar - 1.0) / lam)[..., None] * b
    c = lax.complex(c_re.astype(F32), c_im.astype(F32))
    k = jnp.arange(tc + 1, dtype=F32)
    pw = jnp.exp(lam_dt[:, None, :] * k[None, :, None])
    ktab = jnp.einsum('gcp,gkp,gpd->gdck', c, pw[:, :tc], b_bar).real
    ktab = ktab.reshape(SSM_GROUPS, SSM_GROUP * SSM_GROUP, tc).astype(F32)
    wc = jnp.einsum('gjp,gpd->gdjp', pw[:, tc - 1::-1][:, :tc], b_bar)
    wc = wc.reshape(SSM_GROUPS, SSM_GROUP * tc, SSM_STATE)
    w = jnp.concatenate([wc.real, wc.imag], axis=-1).astype(BF16)
    vc = jnp.einsum('gcp,gtp->gcpt', c, pw[:, 1:tc + 1])
    v = jnp.concatenate([vc.real, -vc.imag], axis=2).astype(BF16)
    lt = pw[:, tc]
    lam_rows = jnp.stack([jnp.concatenate([lt.real, lt.real], -1),
                          jnp.concatenate([-lt.imag, lt.imag], -1)], axis=1).astype(F32)
    dvec = jnp.repeat(d_skip.astype(F32).reshape(SSM_GROUPS, SSM_GROUP), tc, axis=1)
    return ktab, w, v, lam_rows, dvec.reshape(SSM_GROUPS, 1, SSM_GROUP * tc)


def _ssm_scan(ut, ops, bsz, seq):
    ktab, w, v, lam_rows, dvec = ops
    ntok = bsz * seq
    tc = SSM_CHUNK
    nch = ntok // tc
    u3 = ut.reshape(SSM_WIDTH, nch, tc)
    gmap = lambda g: (g, 0, 0)
    y3 = pl.pallas_call(
        functools.partial(_ssm_body, bsz=bsz, cpb=seq // tc),
        out_shape=jax.ShapeDtypeStruct((SSM_WIDTH, nch, tc), BF16),
        grid=(SSM_GROUPS,),
        in_specs=[pl.BlockSpec((SSM_GROUP, nch, tc), gmap),
                  pl.BlockSpec((1, SSM_GROUP * SSM_GROUP, tc), gmap),
                  pl.BlockSpec((1, SSM_GROUP * tc, 2 * SSM_STATE), gmap),
                  pl.BlockSpec((1, SSM_GROUP, 2 * SSM_STATE, tc), lambda g: (g, 0, 0, 0)),
                  pl.BlockSpec((1, 2, 2 * SSM_STATE), gmap),
                  pl.BlockSpec((1, 1, SSM_GROUP * tc), gmap)],
        out_specs=pl.BlockSpec((SSM_GROUP, nch, tc), gmap),
        scratch_shapes=[pltpu.VMEM((SSM_GROUP * tc, 2 * tc), BF16),
                        pltpu.VMEM((SSM_GROUP * tc, 2 * tc), BF16),
                        pltpu.VMEM((nch, 2 * SSM_STATE), F32),
                        pltpu.VMEM((nch, 2 * SSM_STATE), F32),
                        pltpu.VMEM((nch, 2 * SSM_STATE), F32),
                        pltpu.VMEM((SSM_GROUP // 2, nch, 2 * tc), F32)],
        compiler_params=_cparams(("parallel",)),
        name="ssm_scan",
    )(u3, ktab, w, v, lam_rows, dvec)
    return y3.reshape(SSM_WIDTH, ntok)


def _memkv_body(mem_ref, g_ref, w_ref, gk_ref, k_ref, v_ref):
    h = _rms(mem_ref[...], g_ref[...]).astype(BF16)
    kv = _dot(h, w_ref[...])
    for hd in range(X_HEADS):
        cols = slice(hd * X_HEAD_DIM, (hd + 1) * X_HEAD_DIM)
        k_ref[:, cols] = _rms(kv[:, cols], gk_ref[...]).astype(BF16)
    v_ref[...] = kv[:, X_WIDTH:].astype(BF16)


def _memory_kv(mem2d, g, w_kv, gk):
    m = mem2d.shape[0]
    tm = MEM_LEN
    return pl.pallas_call(
        _memkv_body,
        out_shape=(jax.ShapeDtypeStruct((m, X_WIDTH), BF16), jax.ShapeDtypeStruct((m, X_WIDTH), BF16)),
        grid=(m // tm,),
        in_specs=[pl.BlockSpec((tm, D_MODEL), lambda i: (i, 0)),
                  pl.BlockSpec((1, D_MODEL), lambda i: (0, 0)),
                  pl.BlockSpec((D_MODEL, 2 * X_WIDTH), lambda i: (0, 0)),
                  pl.BlockSpec((1, X_HEAD_DIM), lambda i: (0, 0))],
        out_specs=(pl.BlockSpec((tm, X_WIDTH), lambda i: (i, 0)),
                   pl.BlockSpec((tm, X_WIDTH), lambda i: (i, 0))),
        compiler_params=_cparams(("parallel",)),
        name="memory_kv",
    )(mem2d, g, w_kv, gk)


def _pack_bf16_pairs(x):
    c = x.shape[1] // 2
    bits = pltpu.bitcast(x.astype(BF16).astype(F32), U32)
    return (bits[:, :c] & jnp.uint32(0xFFFF0000)) | (bits[:, c:] >> 16)


def _unpack_bf16_pairs(p):
    hi = pltpu.bitcast(p & jnp.uint32(0xFFFF0000), F32)
    lo = pltpu.bitcast(p << 16, F32)
    return jnp.concatenate([hi, lo], axis=1)


def _merge_body(*refs, moe):
    (x_ref, memq_ref, gate_ref, yt_ref, o0_ref, l0_ref, o1_ref, l1_ref, o2_ref, l2_ref,
     km_ref, vm_ref, wglut_ref, wso_ref, wao_ref, wmo_ref, wo_ref, gmq_ref, gffn_ref) = refs[:19]
    if moe:
        wrh_ref, wrl_ref, x1_ref, hp_ref, ri_ref, rw_ref = refs[19:]
    else:
        x1_ref, h2_ref = refs[19:]

    ga = _dot(wglut_ref[...], yt_ref[...])
    glu = ga[:SSM_WIDTH] * _sigmoid(ga[SSM_WIDTH:])
    y_ssm = _dot(jnp.transpose(glu).astype(BF16), wso_ref[...])

    planes = []
    for p in range(ATT_OUT // LANES):
        l0, l1, l2 = l0_ref[p], l1_ref[p], l2_ref[p]
        mx = jnp.maximum(jnp.maximum(l0, l1), l2)
        e0, e1, e2 = jnp.exp(l0 - mx), jnp.exp(l1 - mx), jnp.exp(l2 - mx)
        planes.append((e0 * o0_ref[p] + e1 * o1_ref[p] + e2 * o2_ref[p]) / (e0 + e1 + e2))
    y_att = _dot(jnp.concatenate(planes, axis=1).astype(BF16), wao_ref[...])

    heads = []
    for hd in range(X_HEADS):
        cols = slice(hd * X_HEAD_DIM, (hd + 1) * X_HEAD_DIM)
        q = (_rms(memq_ref[:, cols].astype(F32), gmq_ref[...]) * (X_HEAD_DIM ** -0.5)).astype(BF16)
        sc = _dot_nt(q, km_ref[:, cols])
        m = jnp.max(sc, axis=-1, keepdims=True)
        pr = jnp.exp(sc - m)
        den = jnp.sum(pr, axis=-1, keepdims=True)
        heads.append(_dot(pr.astype(BF16), vm_ref[:, cols]) / den)
    y_mem = _dot(jnp.concatenate(heads, axis=1).astype(BF16), wmo_ref[...])

    d = D_MODEL
    merged = (_sigmoid(gate_ref[:, 0:d].astype(F32)) * y_ssm
              + _sigmoid(gate_ref[:, d:2 * d].astype(F32)) * y_att
              + _sigmoid(gate_ref[:, 2 * d:3 * d].astype(F32)) * y_mem)
    x1 = x_ref[...] + _dot(merged.astype(BF16), wo_ref[...])
    x1_ref[...] = x1
    h2 = _rms(x1, gffn_ref[...])
    if not moe:
        h2_ref[...] = h2.astype(BF16)
        return

    hp_ref[...] = _pack_bf16_pairs(h2)
    hi = h2.astype(BF16)
    lo = (h2 - hi.astype(F32)).astype(BF16)
    logits = _dot(hi, wrh_ref[...]) + _dot(hi, wrl_ref[...]) + _dot(lo, wrh_ref[...])
    lane = lax.broadcasted_iota(I32, logits.shape, 1)
    lg = jnp.where(lane < N_EXPERTS, logits, -jnp.inf)
    v1 = jnp.max(lg, axis=-1, keepdims=True)
    i1 = jnp.min(jnp.where(lg == v1, lane, LANES), axis=-1, keepdims=True)
    lg2 = jnp.where(lane == i1, -jnp.inf, lg)
    v2 = jnp.max(lg2, axis=-1, keepdims=True)
    i2 = jnp.min(jnp.where(lg2 == v2, lane, LANES), axis=-1, keepdims=True)
    e = jnp.exp(v2 - v1)
    ri_ref[...] = jnp.where(lane == 0, i1, jnp.where(lane == 1, i2, 0))
    rw_ref[...] = jnp.where(lane == 0, 1.0 / (1.0 + e), jnp.where(lane == 1, e / (1.0 + e), 0.0))


def _merge(x2d, memq, gates, yt, att, kmem, vmem, wts, bsz, seq, router=None):
    n = x2d.shape[0]
    tm = 512
    tpb = seq // tm
    moe = router is not None
    row = lambda i: (i, 0)
    const = lambda i: (0, 0)
    in_specs = [pl.BlockSpec((tm, D_MODEL), row),
                pl.BlockSpec((tm, X_WIDTH), row),
                pl.BlockSpec((tm, 3 * D_MODEL), row),
                pl.BlockSpec((SSM_WIDTH, tm), lambda i: (0, i))]
    in_specs += [pl.BlockSpec((ATT_OUT // LANES, tm, LANES), lambda i: (0, i, 0))] * 6
    in_specs += [pl.BlockSpec((MEM_LEN, X_WIDTH), lambda i: (i // tpb, 0))] * 2
    wglut, wso, wao, wmo, wo, gmq, gffn = wts
    in_specs += [pl.BlockSpec(w.shape, const) for w in (wglut, wso, wao, wmo, wo, gmq, gffn)]
    args = [x2d, memq, gates, yt, *att, kmem, vmem, wglut, wso, wao, wmo, wo, gmq, gffn]
    if moe:
        in_specs += [pl.BlockSpec(router[0].shape, const)] * 2
        args += list(router)
        out_shape = (jax.ShapeDtypeStruct((n, D_MODEL), F32),
                     jax.ShapeDtypeStruct((n, D_MODEL // 2), U32),
                     jax.ShapeDtypeStruct((n, LANES), I32),
                     jax.ShapeDtypeStruct((n, LANES), F32))
        out_specs = (pl.BlockSpec((tm, D_MODEL), row), pl.BlockSpec((tm, D_MODEL // 2), row),
                     pl.BlockSpec((tm, LANES), row), pl.BlockSpec((tm, LANES), row))
    else:
        out_shape = (jax.ShapeDtypeStruct((n, D_MODEL), F32), jax.ShapeDtypeStruct((n, D_MODEL), BF16))
        out_specs = (pl.BlockSpec((tm, D_MODEL), row), pl.BlockSpec((tm, D_MODEL), row))
    return pl.pallas_call(
        functools.partial(_merge_body, moe=moe),
        out_shape=out_shape,
        grid=(n // tm,),
        in_specs=in_specs,
        out_specs=out_specs,
        compiler_params=_cparams(("parallel",)),
        name="merge_moe" if moe else "merge_dense",
    )(*args)


FFN_CHUNK = 768


def _ffn_body(h_ref, x_ref, wg_ref, wu_ref, wd_ref, o_ref):
    h = h_ref[...]
    acc = x_ref[...]
    for c, w in _col_chunks(D_FF, FFN_CHUNK):
        a = _dot(h, wg_ref[:, c:c + w])
        act = (a * _sigmoid(a) * _dot(h, wu_ref[:, c:c + w])).astype(BF16)
        acc = acc + _dot(act, wd_ref[c:c + w, :])
    o_ref[...] = acc


def _dense_ffn(h2, x1, wg, wu, wd):
    n = h2.shape[0]
    tm = 512
    row = lambda i: (i, 0)
    resident = lambda shape: pl.BlockSpec(shape, lambda i: (0, 0), pipeline_mode=pl.Buffered(1))
    return pl.pallas_call(
        _ffn_body,
        out_shape=jax.ShapeDtypeStruct((n, D_MODEL), F32),
        grid=(n // tm,),
        in_specs=[pl.BlockSpec((tm, D_MODEL), row),
                  pl.BlockSpec((tm, D_MODEL), row),
                  resident((D_MODEL, D_FF)), resident((D_MODEL, D_FF)), resident((D_FF, D_MODEL))],
        out_specs=pl.BlockSpec((tm, D_MODEL), row),
        compiler_params=_cparams(("parallel",)),
        name="dense_ffn",
    )(h2, x1, wg, wu, wd)


MOE_TM = 512
PLAN_TB = 512


def _moe_rows(ntok):
    return 2 * ntok + N_EXPERTS * MOE_TM


def _plan_body(ri_ref, rank_ref, cnt_ref, carry_ref):
    i = pl.program_id(0)

    @pl.when(i == 0)
    def _():
        carry_ref[...] = jnp.zeros_like(carry_ref)

    ri = ri_ref[...]
    lane = lax.broadcasted_iota(I32, ri.shape, 1)
    e1 = ri[:, 0:1]
    e2 = ri[:, 1:2]
    oh = (jnp.where(lane < N_EXPERTS, e1, e2 + N_EXPERTS) == lane) & (lane < 2 * N_EXPERTS)
    ohf = jnp.where(oh, 1.0, 0.0)
    tr = lax.broadcasted_iota(I32, (PLAN_TB, PLAN_TB), 0)
    tcol = lax.broadcasted_iota(I32, (PLAN_TB, PLAN_TB), 1)
    tri = jnp.where(tcol < tr, 1.0, 0.0).astype(BF16)
    excl = _dot(tri, ohf.astype(BF16)) + carry_ref[...]
    mine = jnp.where(oh, excl, 0.0)
    r0 = jnp.sum(jnp.where(lane < N_EXPERTS, mine, 0.0), axis=-1, keepdims=True)
    r1 = jnp.sum(jnp.where(lane >= N_EXPERTS, mine, 0.0), axis=-1, keepdims=True)
    rank_ref[...] = jnp.where(lane == 0, r0, jnp.where(lane == 1, r1, 0.0))
    carry_ref[...] += jnp.sum(ohf, axis=0, keepdims=True)
    cnt_ref[...] = carry_ref[...]


def _moe_plan(ri):
    n = ri.shape[0]
    return pl.pallas_call(
        _plan_body,
        out_shape=(jax.ShapeDtypeStruct((n, LANES), F32), jax.ShapeDtypeStruct((1, LANES), F32)),
        grid=(n // PLAN_TB,),
        in_specs=[pl.BlockSpec((PLAN_TB, LANES), lambda i: (i, 0))],
        out_specs=(pl.BlockSpec((PLAN_TB, LANES), lambda i: (i, 0)),
                   pl.BlockSpec((1, LANES), lambda i: (0, 0))),
        scratch_shapes=[pltpu.VMEM((1, LANES), F32)],
        compiler_params=_cparams(("arbitrary",)),
        name="moe_plan",
    )(ri)


DISPATCH_TB = 512


SUBLANES = 8


def _wait_rows(hbm_ref, nrows, sem):
    whole = hbm_ref.at[pl.ds(0, nrows)]
    pltpu.make_async_copy(whole, whole, sem).wait()


def _dispatch_body(pos_ref, h_ref, xs_in_ref, xs_ref, sem):
    del xs_in_ref
    base = pl.program_id(0) * DISPATCH_TB

    def issue(g, carry):
        for u in range(SUBLANES):
            tok = base + g * SUBLANES + u
            for k in range(2):
                pltpu.make_async_copy(h_ref.at[g, pl.ds(u, 1)],
                                      xs_ref.at[pl.ds(pos_ref[2 * tok + k], 1)], sem).start(priority=k)
        return carry

    lax.fori_loop(0, DISPATCH_TB // SUBLANES, issue, 0)

    for _ in range(2):
        _wait_rows(xs_ref, DISPATCH_TB, sem)


def _moe_dispatch(pos, hp):
    n, c = hp.shape
    xs0 = jnp.zeros((_moe_rows(n), c), U32)
    hp = hp.reshape(n // SUBLANES, SUBLANES, c)
    return pl.pallas_call(
        _dispatch_body,
        out_shape=jax.ShapeDtypeStruct(xs0.shape, U32),
        grid_spec=pltpu.PrefetchScalarGridSpec(
            num_scalar_prefetch=1,
            grid=(n // DISPATCH_TB,),
            in_specs=[pl.BlockSpec((DISPATCH_TB // SUBLANES, SUBLANES, c), lambda i, p: (i, 0, 0)),
                      pl.BlockSpec(memory_space=pl.ANY)],
            out_specs=pl.BlockSpec(memory_space=pl.ANY),
            scratch_shapes=[pltpu.SemaphoreType.DMA(())]),
        input_output_aliases={2: 0},
        compiler_params=_cparams(("arbitrary",)),
        name="moe_dispatch",
    )(pos, hp, xs0)


COMBINE_TB = 256


def _combine_body(pos_ref, x_ref, rw_ref, ys_ref, o_ref, buf, sem):
    base = pl.program_id(0) * COMBINE_TB

    def issue(g, carry):
        for u in range(SUBLANES):
            tok = base + g * SUBLANES + u
            for k in range(2):
                pltpu.make_async_copy(ys_ref.at[pl.ds(pos_ref[2 * tok + k], 1)],
                                      buf.at[k, g, pl.ds(u, 1)], sem).start(priority=k)
        return carry

    lax.fori_loop(0, COMBINE_TB // SUBLANES, issue, 0)
    for _ in range(2):
        _wait_rows(ys_ref, COMBINE_TB, sem)
    rw = rw_ref[...]
    rows = lambda k: _unpack_bf16_pairs(buf[k].reshape(COMBINE_TB, buf.shape[-1]))
    o_ref[...] = x_ref[...] + rw[:, 0:1] * rows(0) + rw[:, 1:2] * rows(1)


def _moe_combine(pos, x1, rw, ys):
    n = x1.shape[0]
    c = ys.shape[1]
    return pl.pallas_call(
        _combine_body,
        out_shape=jax.ShapeDtypeStruct((n, D_MODEL), F32),
        grid_spec=pltpu.PrefetchScalarGridSpec(
            num_scalar_prefetch=1,
            grid=(n // COMBINE_TB,),
            in_specs=[pl.BlockSpec((COMBINE_TB, D_MODEL), lambda i, p: (i, 0)),
                      pl.BlockSpec((COMBINE_TB, LANES), lambda i, p: (i, 0)),
                      pl.BlockSpec(memory_space=pl.ANY)],
            out_specs=pl.BlockSpec((COMBINE_TB, D_MODEL), lambda i, p: (i, 0)),
            scratch_shapes=[pltpu.VMEM((2, COMBINE_TB // SUBLANES, SUBLANES, c), U32),
                            pltpu.SemaphoreType.DMA(())]),
        compiler_params=_cparams(("arbitrary",)),
        name="moe_combine",
    )(pos, x1, rw, ys)


MOE_TF = 1792


def _experts_body(te_ref, nv_ref, xs_ref, wg_ref, wu_ref, wd_ref, ys_ref, xb_ref, acc_ref):
    i = pl.program_id(0)
    f = pl.program_id(1)
    nf = pl.num_programs(1)

    @pl.when(i < nv_ref[0])
    def _():
        @pl.when(f == 0)
        def _():
            xb_ref[...] = _unpack_bf16_pairs(xs_ref[...]).astype(BF16)

        h = xb_ref[...]
        part = None
        for c, w in _col_chunks(MOE_TF, 1024):
            a = _dot(h, wg_ref[0, :, c:c + w])
            act = (a * _sigmoid(a) * _dot(h, wu_ref[0, :, c:c + w])).astype(BF16)
            pc = _dot(act, wd_ref[0, c:c + w, :])
            part = pc if part is None else part + pc

        @pl.when(f == 0)
        def _():
            acc_ref[...] = part

        @pl.when(f > 0)
        def _():
            acc_ref[...] += part

        @pl.when(f == nf - 1)
        def _():
            ys_ref[...] = _pack_bf16_pairs(acc_ref[...])

    @pl.when((i >= nv_ref[0]) & (f == nf - 1))
    def _():
        ys_ref[...] = jnp.zeros_like(ys_ref)


def _moe_experts(tile_expert, n_valid, xs, wg, wu, wd):
    rows, c = xs.shape
    nt = rows // MOE_TM
    nf = D_FF_EXPERT // MOE_TF

    def tile(i, nv):
        return jnp.minimum(i, nv[0] - 1)

    def fblk(i, f, nv):
        return jnp.where(i < nv[0], f, nf - 1)

    return pl.pallas_call(
        _experts_body,
        out_shape=jax.ShapeDtypeStruct((rows, c), U32),
        grid_spec=pltpu.PrefetchScalarGridSpec(
            num_scalar_prefetch=2,
            grid=(nt, nf),
            in_specs=[pl.BlockSpec((MOE_TM, c), lambda i, f, te, nv: (tile(i, nv), 0)),
                      pl.BlockSpec((1, D_MODEL, MOE_TF),
                                   lambda i, f, te, nv: (te[tile(i, nv)], 0, fblk(i, f, nv))),
                      pl.BlockSpec((1, D_MODEL, MOE_TF),
                                   lambda i, f, te, nv: (te[tile(i, nv)], 0, fblk(i, f, nv))),
                      pl.BlockSpec((1, MOE_TF, D_MODEL),
                                   lambda i, f, te, nv: (te[tile(i, nv)], fblk(i, f, nv), 0))],
            out_specs=pl.BlockSpec((MOE_TM, c), lambda i, f, te, nv: (i, 0)),
            scratch_shapes=[pltpu.VMEM((MOE_TM, D_MODEL), BF16), pltpu.VMEM((MOE_TM, D_MODEL), F32)]),
        compiler_params=_cparams(("arbitrary", "arbitrary")),
        name="moe_experts",
    )(tile_expert, n_valid, xs, wg, wu, wd)


def _moe_ffn(x1, hp, ri, rw, wg, wu, wd):
    rank, cnt = _moe_plan(ri)
    c0 = cnt[0, :N_EXPERTS].astype(I32)
    c1 = cnt[0, N_EXPERTS:2 * N_EXPERTS].astype(I32)
    padded = ((c0 + c1 + MOE_TM - 1) // MOE_TM) * MOE_TM
    ends = jnp.cumsum(padded)
    off = ends - padded
    e1, e2 = ri[:, 0], ri[:, 1]
    pos0 = off[e1] + rank[:, 0].astype(I32)
    pos1 = off[e2] + c0[e2] + rank[:, 1].astype(I32)
    pos = jnp.stack([pos0, pos1], axis=1).reshape(-1)
    nt = _moe_rows(x1.shape[0]) // MOE_TM
    tile_start = jnp.arange(nt, dtype=I32) * MOE_TM
    tile_expert = jnp.minimum(jnp.sum(tile_start[:, None] >= ends[None, :], axis=1),
                              N_EXPERTS - 1).astype(I32)
    n_valid = (ends[-1:] // MOE_TM).astype(I32)
    xs = _moe_dispatch(pos, hp)
    ys = _moe_experts(tile_expert, n_valid, xs, wg, wu, wd)
    return _moe_combine(pos, x1, rw, ys)


def _head_consts():
    lane = jnp.arange(ATT_OUT)
    bd = jnp.where((lane[:, None] // ATT_HEAD_DIM) == (lane[None, :] // ATT_HEAD_DIM),
                   1.0 / ATT_HEAD_DIM, 0.0).astype(BF16)
    perm = (lane[:, None] == (lane[None, :] ^ (ATT_HEAD_DIM // 2))).astype(BF16)
    return bd, perm


def _head_gains(g):
    full = jnp.tile(g.astype(F32), ATT_SLOTS).reshape(1, ATT_OUT)
    half = ATT_HEAD_DIM // 2
    swapped = jnp.tile(jnp.concatenate([g[half:], g[:half]]).astype(F32), ATT_SLOTS).reshape(1, ATT_OUT)
    return full, swapped


def kernel(x, mem, positions, norm_mix, w_in, ssm_a_re, ssm_a_im, ssm_log_dt, ssm_b_re, ssm_b_im,
           ssm_c_re, ssm_c_im, ssm_d, ssm_w_glu, w_ssm_out, att_q_norm, att_k_norm, w_att_out,
           norm_mem, w_mem_kv, mem_q_norm, mem_k_norm, w_mem_out, w_o, norm_ffn, ffn_w_gate,
           ffn_w_up, ffn_w_down, moe_w_router, moe_w_gate, moe_w_up, moe_w_down):
    bsz, seq, d = x.shape
    ntok = bsz * seq
    depth = w_in.shape[0]
    cos_t, sin_t = _rope_tables(positions)
    bd, perm = _head_consts()
    mem2d = mem.reshape(bsz * MEM_LEN, d)
    x2d = x.reshape(ntok, d)
    for i in range(depth):
        wi = w_in[i]
        g_mix = norm_mix[i].reshape(1, d)

        wut = jnp.transpose(wi[:, :SSM_WIDTH]).astype(BF16)
        ut, qkv, memq, gates = _inproj_main(x2d, g_mix, wut, wi.astype(BF16))

        ops = _ssm_operators(ssm_a_re[i], ssm_a_im[i], ssm_log_dt[i], ssm_b_re[i], ssm_b_im[i],
                             ssm_c_re[i], ssm_c_im[i], ssm_d[i])
        yt = _ssm_scan(ut, ops, bsz, seq)

        gq, gqs = _head_gains(att_q_norm[i])
        gk, gks = _head_gains(att_k_norm[i])
        att = []
        for gi, (_, dil) in enumerate(DIL_PAIRS):
            att.extend(_dilated_attention_group(qkv, gi, cos_t, sin_t, gq, gqs, gk, gks, bd, perm,
                                                bsz, seq, dil))

        kmem, vmem = _memory_kv(mem2d, norm_mem[i].reshape(1, d), w_mem_kv[i].astype(BF16),
                                mem_k_norm[i].reshape(1, X_HEAD_DIM))

        wts = (jnp.transpose(ssm_w_glu[i]).astype(BF16), w_ssm_out[i].astype(BF16),
               w_att_out[i].astype(BF16), w_mem_out[i].astype(BF16), w_o[i].astype(BF16),
               mem_q_norm[i].reshape(1, X_HEAD_DIM), norm_ffn[i].reshape(1, d))
        j = i // 2
        if i % 2 == 0:
            x1, h2 = _merge(x2d, memq, gates, yt, att, kmem, vmem, wts, bsz, seq)
            x2d = _dense_ffn(h2, x1, ffn_w_gate[j].astype(BF16), ffn_w_up[j].astype(BF16),
                             ffn_w_down[j].astype(BF16))
        else:
            wr = jnp.zeros((d, LANES), F32).at[:, :N_EXPERTS].set(moe_w_router[j])
            wr_hi = wr.astype(BF16)
            wr_lo = (wr - wr_hi.astype(F32)).astype(BF16)
            x1, hp, ri, rw = _merge(x2d, memq, gates, yt, att, kmem, vmem, wts, bsz, seq,
                                    router=(wr_hi, wr_lo))
            x2d = _moe_ffn(x1, hp, ri, rw, moe_w_gate[j].astype(BF16), moe_w_up[j].astype(BF16),
                           moe_w_down[j].astype(BF16))
    return x2d.reshape(bsz, seq, d)
```

```python
import functools
import math

import jax
import jax.numpy as jnp
from jax import lax
from jax.experimental import pallas as pl
from jax.experimental.pallas import tpu as pltpu

F32 = jnp.float32
BF16 = jnp.bfloat16
I32 = jnp.int32
U32 = jnp.uint32

EPS = 1e-6
D_MODEL = 1024
MEM_LEN = 256
SSM_WIDTH = 512
SSM_GROUP = 16
SSM_GROUPS = 32
SSM_STATE = 64
ATT_HEAD_DIM = 64
ATT_SLOTS = 4
DIL_PAIRS = ((128, 1), (512, 4), (2048, 16))
ATT_WIDTH = 768
ATT_OUT = 256
BLOCK = 128
ROPE_THETA = 10000.0
X_HEADS = 4
X_HEAD_DIM = 128
X_WIDTH = 512
D_FF = 2816
N_EXPERTS = 8
D_FF_EXPERT = 3584

LANES = 128
SSM_CHUNK = 128
QKV_W = 3 * ATT_OUT
VMEM_LIMIT = 56 * 1024 * 1024


def _cparams(sem, vmem=VMEM_LIMIT):
    return pltpu.CompilerParams(dimension_semantics=sem, vmem_limit_bytes=vmem)


def _rms(x, g):
    ms = jnp.mean(x * x, axis=-1, keepdims=True)
    return x * lax.rsqrt(ms + EPS) * g


def _sigmoid(x):
    return 0.5 * jnp.tanh(0.5 * x) + 0.5


def _dot(a, b):
    return jnp.dot(a, b, preferred_element_type=F32)


def _dot_nt(a, b):
    return lax.dot_general(a, b, (((1,), (1,)), ((), ())), preferred_element_type=F32)


def _rope_body(pos_ref, inv_ref, sgn_ref, cos_ref, sin_ref):
    ang = pos_ref[...].astype(F32) * inv_ref[...]
    cos_ref[...] = jnp.cos(ang)
    sin_ref[...] = jnp.sin(ang) * sgn_ref[...]


def _rope_tables(positions):
    n = positions.size
    half = ATT_HEAD_DIM // 2
    inv = ROPE_THETA ** (-jnp.arange(half, dtype=F32) / half)
    inv_row = jnp.tile(inv, LANES // half).reshape(1, LANES)
    lane = jnp.arange(LANES)
    sgn_row = jnp.where((lane % ATT_HEAD_DIM) < half, -1.0, 1.0).astype(F32).reshape(1, LANES)
    tm = 2048
    return pl.pallas_call(
        _rope_body,
        out_shape=(jax.ShapeDtypeStruct((n, LANES), F32), jax.ShapeDtypeStruct((n, LANES), F32)),
        grid=(n // tm,),
        in_specs=[pl.BlockSpec((tm, 1), lambda i: (i, 0)),
                  pl.BlockSpec((1, LANES), lambda i: (0, 0)),
                  pl.BlockSpec((1, LANES), lambda i: (0, 0))],
        out_specs=(pl.BlockSpec((tm, LANES), lambda i: (i, 0)),
                   pl.BlockSpec((tm, LANES), lambda i: (i, 0))),
        compiler_params=_cparams(("parallel",)),
        name="rope_tables",
    )(positions.reshape(n, 1), inv_row, sgn_row)


def _col_chunks(width, step=512):
    return [(c, min(step, width - c)) for c in range(0, width, step)]


def _inproj_main_body(x_ref, g_ref, wut_ref, w_ref, ut_ref, qkv_ref, memq_ref, gate_ref):
    h = _rms(x_ref[...], g_ref[...]).astype(BF16)
    ut_ref[...] = _dot_nt(wut_ref[...], h).astype(BF16)
    for gi in range(len(DIL_PAIRS)):
        for j in range(3):
            src = SSM_WIDTH + j * ATT_WIDTH + gi * ATT_OUT
            dst = gi * QKV_W + j * ATT_OUT
            qkv_ref[:, dst:dst + ATT_OUT] = _dot(h, w_ref[:, src:src + ATT_OUT]).astype(BF16)
    col = SSM_WIDTH + 3 * ATT_WIDTH
    for ref in (memq_ref, gate_ref):
        for c, w in _col_chunks(ref.shape[1]):
            ref[:, c:c + w] = _dot(h, w_ref[:, col + c:col + c + w]).astype(BF16)
        col += ref.shape[1]


def _inproj_main(x2d, g, wut, wmain):
    n = x2d.shape[0]
    tm = 512
    row = lambda i: (i, 0)
    const = lambda i: (0, 0)
    widths = (3 * QKV_W, X_WIDTH, 3 * D_MODEL)
    return pl.pallas_call(
        _inproj_main_body,
        out_shape=(jax.ShapeDtypeStruct((SSM_WIDTH, n), BF16),)
        + tuple(jax.ShapeDtypeStruct((n, w), BF16) for w in widths),
        grid=(n // tm,),
        in_specs=[pl.BlockSpec((tm, D_MODEL), row),
                  pl.BlockSpec((1, D_MODEL), const),
                  pl.BlockSpec((SSM_WIDTH, D_MODEL), const, pipeline_mode=pl.Buffered(1)),
                  pl.BlockSpec(wmain.shape, const, pipeline_mode=pl.Buffered(1))],
        out_specs=(pl.BlockSpec((SSM_WIDTH, tm), lambda i: (0, i)),)
        + tuple(pl.BlockSpec((tm, w), row) for w in widths),
        compiler_params=_cparams(("parallel",)),
        name="inproj_main",
    )(x2d, g, wut, wmain)


def _qk_prep(xf, cos2, sin2, g, gs, bd, perm, scale):
    xb = xf.astype(BF16)
    ms = _dot((xf * xf).astype(BF16), bd)
    xs = _dot(xb, perm)
    y = lax.rsqrt(ms + EPS) * scale * (xf * (g * cos2) + xs * (gs * sin2))
    return y.astype(BF16)


def _attn_body(qkv_ref, cos_ref, sin_ref, gq_ref, gqs_ref, gk_ref, gks_ref, bd_ref, perm_ref,
               o_ref, lse_ref, sbuf, qbuf, kbuf, vbuf, *, dil, tq):
    j = pl.program_id(1)

    @pl.when(j == 0)
    def _():
        kbuf[:, 0:BLOCK, :] = jnp.zeros((dil, BLOCK, ATT_OUT), BF16)
        vbuf[:, 0:BLOCK, :] = jnp.zeros((dil, BLOCK, ATT_OUT), BF16)

    @pl.when(j > 0)
    def _():
        kbuf[:, 0:BLOCK, :] = kbuf[:, tq:tq + BLOCK, :]
        vbuf[:, 0:BLOCK, :] = vbuf[:, tq:tq + BLOCK, :]

    bd = bd_ref[...]
    perm = perm_ref[...]
    lane = lax.broadcasted_iota(I32, (BLOCK, LANES), 1)
    low = lane < ATT_HEAD_DIM
    qi = lax.broadcasted_iota(I32, (BLOCK, 2 * BLOCK), 0) + BLOCK
    ki = lax.broadcasted_iota(I32, (BLOCK, 2 * BLOCK), 1)
    off = qi - ki
    band = (off >= 0) & (off <= BLOCK)
    band_first = band & ((ki >= BLOCK) | (j > 0))
    for c in range(QKV_W // LANES):
        sbuf[c] = qkv_ref[:, c * LANES:(c + 1) * LANES].astype(F32)

    def rows(start, size):
        return pl.ds(start, size, stride=dil) if dil > 1 else pl.ds(start, size)

    def planes(first, sel):
        return jnp.concatenate([sbuf[first, sel, :], sbuf[first + 1, sel, :]], axis=1)

    def prep(r, carry):
        cos = cos_ref[rows(r, tq), :]
        sin = sin_ref[rows(r, tq), :]
        cos2 = jnp.concatenate([cos, cos], axis=1)
        sin2 = jnp.concatenate([sin, sin], axis=1)
        cur = rows(r, tq)
        qbuf[r] = _qk_prep(planes(0, cur), cos2, sin2, gq_ref[...], gqs_ref[...],
                           bd, perm, ATT_HEAD_DIM ** -0.5)
        kbuf[r, BLOCK:, :] = _qk_prep(planes(2, cur), cos2, sin2,
                                      gk_ref[...], gks_ref[...], bd, perm, 1.0)
        vbuf[r, BLOCK:, :] = planes(4, cur).astype(BF16)
        return carry

    if dil == 1:
        prep(0, 0)
    else:
        lax.fori_loop(0, dil, prep, 0, unroll=2)

    for r in range(dil):
        for s in range(tq // BLOCK):
            row0 = s * BLOCK
            valid = band_first if s == 0 else band
            dst = rows(r + row0 * dil, BLOCK)
            for p in range(ATT_OUT // LANES):
                cols = slice(p * LANES, (p + 1) * LANES)
                qp = qbuf[r, row0:row0 + BLOCK, cols]
                kp = kbuf[r, row0:row0 + 2 * BLOCK, cols]
                vp = vbuf[r, row0:row0 + 2 * BLOCK, cols]
                outs, lses = [], []
                for h in range(2):
                    qm = jnp.where(low if h == 0 else ~low, qp, jnp.zeros_like(qp))
                    sc = jnp.where(valid, _dot_nt(qm, kp), -1e30)
                    m = jnp.max(sc, axis=-1, keepdims=True)
                    pr = jnp.exp(sc - m)
                    den = jnp.sum(pr, axis=-1, keepdims=True)
                    outs.append(_dot(pr.astype(BF16), vp) / den)
                    lses.append(m + jnp.log(den))
                o_ref[p, dst, :] = jnp.where(low, outs[0], outs[1])
                lse_ref[p, dst, :] = jnp.where(low, lses[0], lses[1])


def _dilated_attention_group(qkv_all, gi, cos_t, sin_t, gq, gqs, gk, gks, bd, perm, bsz, seq, dil):
    ntok = bsz * seq
    tt = max(1024, BLOCK * dil)
    tq = tt // dil
    nblk = seq // tt
    const = lambda b, j: (0, 0)
    return pl.pallas_call(
        functools.partial(_attn_body, dil=dil, tq=tq),
        out_shape=(jax.ShapeDtypeStruct((ATT_OUT // LANES, ntok, LANES), F32),
                   jax.ShapeDtypeStruct((ATT_OUT // LANES, ntok, LANES), F32)),
        grid=(bsz, nblk),
        in_specs=[pl.BlockSpec((tt, QKV_W), lambda b, j: (b * nblk + j, gi)),
                  pl.BlockSpec((tt, LANES), lambda b, j: (b * nblk + j, 0)),
                  pl.BlockSpec((tt, LANES), lambda b, j: (b * nblk + j, 0)),
                  pl.BlockSpec((1, ATT_OUT), const), pl.BlockSpec((1, ATT_OUT), const),
                  pl.BlockSpec((1, ATT_OUT), const), pl.BlockSpec((1, ATT_OUT), const),
                  pl.BlockSpec((ATT_OUT, ATT_OUT), const), pl.BlockSpec((ATT_OUT, ATT_OUT), const)],
        out_specs=(pl.BlockSpec((ATT_OUT // LANES, tt, LANES), lambda b, j: (0, b * nblk + j, 0)),
                   pl.BlockSpec((ATT_OUT // LANES, tt, LANES), lambda b, j: (0, b * nblk + j, 0))),
        scratch_shapes=[pltpu.VMEM((QKV_W // LANES, tt, LANES), F32),
                        pltpu.VMEM((dil, tq, ATT_OUT), BF16),
                        pltpu.VMEM((dil, BLOCK + tq, ATT_OUT), BF16),
                        pltpu.VMEM((dil, BLOCK + tq, ATT_OUT), BF16)],
        compiler_params=_cparams(("arbitrary", "arbitrary")),
        name=f"dilated_attn{dil}",
    )(qkv_all, cos_t, sin_t, gq, gqs, gk, gks, bd, perm)


def _ssm_body(u_ref, ktab_ref, w_ref, v_ref, lam_ref, dvec_ref, y_ref, m_ref, m2_ref, sloc_ref,
              ssw_ref, sin_ref, yin_ref, *, bsz, cpb):
    tc = SSM_CHUNK
    row = lax.broadcasted_iota(I32, (tc, tc), 0)
    col = lax.broadcasted_iota(I32, (tc, tc), 1)
    causal = col >= row

    u = jnp.concatenate([u_ref[c] for c in range(SSM_GROUP)], axis=1)
    sloc = _dot(u, w_ref[0])
    sloc_ref[...] = sloc
    ssw_ref[...] = pltpu.roll(sloc, SSM_STATE, 1)

    a1 = lam_ref[0, 0:1, :]
    a2 = lam_ref[0, 1:2, :]
    s = jnp.zeros((bsz, 2 * SSM_STATE), F32)
    sw = s
    sin_ref[pl.ds(0, bsz, stride=cpb), :] = s
    for k in range(1, cpb):
        prev = pl.ds(k - 1, bsz, stride=cpb)
        s, sw = (a1 * s + a2 * sw + sloc_ref[prev, :], a1 * sw - a2 * s + ssw_ref[prev, :])
        sin_ref[pl.ds(k, bsz, stride=cpb), :] = s

    s_in = sin_ref[...].astype(BF16)
    nblk = SSM_GROUP // 2

    def y_in(c):
        return _dot(s_in, v_ref[0, c]) + dvec_ref[0, :, c * tc:(c + 1) * tc] * u_ref[c].astype(F32)

    for cb in range(nblk):
        yin_ref[cb] = jnp.concatenate([y_in(2 * cb), y_in(2 * cb + 1)], axis=1)

    def build(cb, dst):
        for cp in range(SSM_GROUP):
            for h in range(2):
                kv = ktab_ref[0, pl.ds(cp * SSM_GROUP + 2 * cb + h, 1), :]
                tile = pltpu.roll(jnp.broadcast_to(kv, (tc, tc)), 0, 1, stride=1, stride_axis=0)
                dst[cp * tc:(cp + 1) * tc, h * tc:(h + 1) * tc] = (
                    jnp.where(causal, tile, 0.0).astype(BF16))

    def multiply(cb, src):
        y = jax.nn.gelu(_dot(u, src[...]) + yin_ref[cb], approximate=True)
        y_ref[2 * cb] = y[:, :tc].astype(BF16)
        y_ref[2 * cb + 1] = y[:, tc:].astype(BF16)

    build(0, m_ref)

    def pair(i, carry):
        build(2 * i + 1, m2_ref)
        multiply(2 * i, m_ref)
        build(jnp.minimum(2 * i + 2, nblk - 1), m_ref)
        multiply(2 * i + 1, m2_ref)
        return carry

    lax.fori_loop(0, nblk // 2, pair, 0)


def _ssm_operators(a_re, a_im, log_dt, b_re, b_im, c_re, c_im, d_skip):
    tc = SSM_CHUNK
    lam = lax.complex(a_re.astype(F32), a_im.astype(F32))
    dt = jnp.exp(log_dt.astype(F32))[:, None]
    lam_dt = lam * dt
    lam_bar = jnp.exp(lam_dt)
    b = lax.complex(b_re.astype(F32), b_im.astype(F32))
    b_bar = ((lam_bar - 1.0) / lam)[..., None] * b
    c = lax.complex(c_re.astype(F32), c_im.astype(F32))
    k = jnp.arange(tc + 1, dtype=F32)
    pw = jnp.exp(lam_dt[:, None, :] * k[None, :, None])
    ktab = jnp.einsum('gcp,gkp,gpd->gdck', c, pw[:, :tc], b_bar).real
    ktab = ktab.reshape(SSM_GROUPS, SSM_GROUP * SSM_GROUP, tc).astype(F32)
    wc = jnp.einsum('gjp,gpd->gdjp', pw[:, tc - 1::-1][:, :tc], b_bar)
    wc = wc.reshape(SSM_GROUPS, SSM_GROUP * tc, SSM_STATE)
    w = jnp.concatenate([wc.real, wc.imag], axis=-1).astype(BF16)
    vc = jnp.einsum('gcp,gtp->gcpt', c, pw[:, 1:tc + 1])
    v = jnp.concatenate([vc.real, -vc.imag], axis=2).astype(BF16)
    lt = pw[:, tc]
    lam_rows = jnp.stack([jnp.concatenate([lt.real, lt.real], -1),
                          jnp.concatenate([-lt.imag, lt.imag], -1)], axis=1).astype(F32)
    dvec = jnp.repeat(d_skip.astype(F32).reshape(SSM_GROUPS, SSM_GROUP), tc, axis=1)
    return ktab, w, v, lam_rows, dvec.reshape(SSM_GROUPS, 1, SSM_GROUP * tc)


def _ssm_scan(ut, ops, bsz, seq):
    ktab, w, v, lam_rows, dvec = ops
    ntok = bsz * seq
    tc = SSM_CHUNK
    nch = ntok // tc
    u3 = ut.reshape(SSM_WIDTH, nch, tc)
    gmap = lambda g: (g, 0, 0)
    y3 = pl.pallas_call(
        functools.partial(_ssm_body, bsz=bsz, cpb=seq // tc),
        out_shape=jax.ShapeDtypeStruct((SSM_WIDTH, nch, tc), BF16),
        grid=(SSM_GROUPS,),
        in_specs=[pl.BlockSpec((SSM_GROUP, nch, tc), gmap),
                  pl.BlockSpec((1, SSM_GROUP * SSM_GROUP, tc), gmap),
                  pl.BlockSpec((1, SSM_GROUP * tc, 2 * SSM_STATE), gmap),
                  pl.BlockSpec((1, SSM_GROUP, 2 * SSM_STATE, tc), lambda g: (g, 0, 0, 0)),
                  pl.BlockSpec((1, 2, 2 * SSM_STATE), gmap),
                  pl.BlockSpec((1, 1, SSM_GROUP * tc), gmap)],
        out_specs=pl.BlockSpec((SSM_GROUP, nch, tc), gmap),
        scratch_shapes=[pltpu.VMEM((SSM_GROUP * tc, 2 * tc), BF16),
                        pltpu.VMEM((SSM_GROUP * tc, 2 * tc), BF16),
                        pltpu.VMEM((nch, 2 * SSM_STATE), F32),
                        pltpu.VMEM((nch, 2 * SSM_STATE), F32),
                        pltpu.VMEM((nch, 2 * SSM_STATE), F32),
                        pltpu.VMEM((SSM_GROUP // 2, nch, 2 * tc), F32)],
        compiler_params=_cparams(("parallel",)),
        name="ssm_scan",
    )(u3, ktab, w, v, lam_rows, dvec)
    return y3.reshape(SSM_WIDTH, ntok)


def _memkv_body(mem_ref, g_ref, w_ref, gk_ref, k_ref, v_ref):
    h = _rms(mem_ref[...], g_ref[...]).astype(BF16)
    kv = _dot(h, w_ref[...])
    for hd in range(X_HEADS):
        cols = slice(hd * X_HEAD_DIM, (hd + 1) * X_HEAD_DIM)
        k_ref[:, cols] = _rms(kv[:, cols], gk_ref[...]).astype(BF16)
    v_ref[...] = kv[:, X_WIDTH:].astype(BF16)


def _memory_kv(mem2d, g, w_kv, gk):
    m = mem2d.shape[0]
    tm = MEM_LEN
    return pl.pallas_call(
        _memkv_body,
        out_shape=(jax.ShapeDtypeStruct((m, X_WIDTH), BF16), jax.ShapeDtypeStruct((m, X_WIDTH), BF16)),
        grid=(m // tm,),
        in_specs=[pl.BlockSpec((tm, D_MODEL), lambda i: (i, 0)),
                  pl.BlockSpec((1, D_MODEL), lambda i: (0, 0)),
                  pl.BlockSpec((D_MODEL, 2 * X_WIDTH), lambda i: (0, 0)),
                  pl.BlockSpec((1, X_HEAD_DIM), lambda i: (0, 0))],
        out_specs=(pl.BlockSpec((tm, X_WIDTH), lambda i: (i, 0)),
                   pl.BlockSpec((tm, X_WIDTH), lambda i: (i, 0))),
        compiler_params=_cparams(("parallel",)),
        name="memory_kv",
    )(mem2d, g, w_kv, gk)


def _pack_bf16_pairs(x):
    c = x.shape[1] // 2
    bits = pltpu.bitcast(x.astype(BF16).astype(F32), U32)
    return (bits[:, :c] & jnp.uint32(0xFFFF0000)) | (bits[:, c:] >> 16)


def _unpack_bf16_pairs(p):
    hi = pltpu.bitcast(p & jnp.uint32(0xFFFF0000), F32)
    lo = pltpu.bitcast(p << 16, F32)
    return jnp.concatenate([hi, lo], axis=1)


def _merge_body(*refs, moe):
    (x_ref, memq_ref, gate_ref, yt_ref, o0_ref, l0_ref, o1_ref, l1_ref, o2_ref, l2_ref,
     km_ref, vm_ref, wglut_ref, wso_ref, wao_ref, wmo_ref, wo_ref, gmq_ref, gffn_ref) = refs[:19]
    if moe:
        wrh_ref, wrl_ref, x1_ref, hp_ref, ri_ref, rw_ref = refs[19:]
    else:
        x1_ref, h2_ref = refs[19:]

    ga = _dot(wglut_ref[...], yt_ref[...])
    glu = ga[:SSM_WIDTH] * _sigmoid(ga[SSM_WIDTH:])
    y_ssm = _dot(jnp.transpose(glu).astype(BF16), wso_ref[...])

    planes = []
    for p in range(ATT_OUT // LANES):
        l0, l1, l2 = l0_ref[p], l1_ref[p], l2_ref[p]
        mx = jnp.maximum(jnp.maximum(l0, l1), l2)
        e0, e1, e2 = jnp.exp(l0 - mx), jnp.exp(l1 - mx), jnp.exp(l2 - mx)
        planes.append((e0 * o0_ref[p] + e1 * o1_ref[p] + e2 * o2_ref[p]) / (e0 + e1 + e2))
    y_att = _dot(jnp.concatenate(planes, axis=1).astype(BF16), wao_ref[...])

    heads = []
    for hd in range(X_HEADS):
        cols = slice(hd * X_HEAD_DIM, (hd + 1) * X_HEAD_DIM)
        q = (_rms(memq_ref[:, cols].astype(F32), gmq_ref[...]) * (X_HEAD_DIM ** -0.5)).astype(BF16)
        sc = _dot_nt(q, km_ref[:, cols])
        m = jnp.max(sc, axis=-1, keepdims=True)
        pr = jnp.exp(sc - m)
        den = jnp.sum(pr, axis=-1, keepdims=True)
        heads.append(_dot(pr.astype(BF16), vm_ref[:, cols]) / den)
    y_mem = _dot(jnp.concatenate(heads, axis=1).astype(BF16), wmo_ref[...])

    d = D_MODEL
    merged = (_sigmoid(gate_ref[:, 0:d].astype(F32)) * y_ssm
              + _sigmoid(gate_ref[:, d:2 * d].astype(F32)) * y_att
              + _sigmoid(gate_ref[:, 2 * d:3 * d].astype(F32)) * y_mem)
    x1 = x_ref[...] + _dot(merged.astype(BF16), wo_ref[...])
    x1_ref[...] = x1
    h2 = _rms(x1, gffn_ref[...])
    if not moe:
        h2_ref[...] = h2.astype(BF16)
        return

    hp_ref[...] = _pack_bf16_pairs(h2)
    hi = h2.astype(BF16)
    lo = (h2 - hi.astype(F32)).astype(BF16)
    logits = _dot(hi, wrh_ref[...]) + _dot(hi, wrl_ref[...]) + _dot(lo, wrh_ref[...])
    lane = lax.broadcasted_iota(I32, logits.shape, 1)
    lg = jnp.where(lane < N_EXPERTS, logits, -jnp.inf)
    v1 = jnp.max(lg, axis=-1, keepdims=True)
    i1 = jnp.min(jnp.where(lg == v1, lane, LANES), axis=-1, keepdims=True)
    lg2 = jnp.where(lane == i1, -jnp.inf, lg)
    v2 = jnp.max(lg2, axis=-1, keepdims=True)
    i2 = jnp.min(jnp.where(lg2 == v2, lane, LANES), axis=-1, keepdims=True)
    e = jnp.exp(v2 - v1)
    ri_ref[...] = jnp.where(lane == 0, i1, jnp.where(lane == 1, i2, 0))
    rw_ref[...] = jnp.where(lane == 0, 1.0 / (1.0 + e), jnp.where(lane == 1, e / (1.0 + e), 0.0))


def _merge(x2d, memq, gates, yt, att, kmem, vmem, wts, bsz, seq, router=None):
    n = x2d.shape[0]
    tm = 512
    tpb = seq // tm
    moe = router is not None
    row = lambda i: (i, 0)
    const = lambda i: (0, 0)
    in_specs = [pl.BlockSpec((tm, D_MODEL), row),
                pl.BlockSpec((tm, X_WIDTH), row),
                pl.BlockSpec((tm, 3 * D_MODEL), row),
                pl.BlockSpec((SSM_WIDTH, tm), lambda i: (0, i))]
    in_specs += [pl.BlockSpec((ATT_OUT // LANES, tm, LANES), lambda i: (0, i, 0))] * 6
    in_specs += [pl.BlockSpec((MEM_LEN, X_WIDTH), lambda i: (i // tpb, 0))] * 2
    wglut, wso, wao, wmo, wo, gmq, gffn = wts
    in_specs += [pl.BlockSpec(w.shape, const) for w in (wglut, wso, wao, wmo, wo, gmq, gffn)]
    args = [x2d, memq, gates, yt, *att, kmem, vmem, wglut, wso, wao, wmo, wo, gmq, gffn]
    if moe:
        in_specs += [pl.BlockSpec(router[0].shape, const)] * 2
        args += list(router)
        out_shape = (jax.ShapeDtypeStruct((n, D_MODEL), F32),
                     jax.ShapeDtypeStruct((n, D_MODEL // 2), U32),
                     jax.ShapeDtypeStruct((n, LANES), I32),
                     jax.ShapeDtypeStruct((n, LANES), F32))
        out_specs = (pl.BlockSpec((tm, D_MODEL), row), pl.BlockSpec((tm, D_MODEL // 2), row),
                     pl.BlockSpec((tm, LANES), row), pl.BlockSpec((tm, LANES), row))
    else:
        out_shape = (jax.ShapeDtypeStruct((n, D_MODEL), F32), jax.ShapeDtypeStruct((n, D_MODEL), BF16))
        out_specs = (pl.BlockSpec((tm, D_MODEL), row), pl.BlockSpec((tm, D_MODEL), row))
    return pl.pallas_call(
        functools.partial(_merge_body, moe=moe),
        out_shape=out_shape,
        grid=(n // tm,),
        in_specs=in_specs,
        out_specs=out_specs,
        compiler_params=_cparams(("parallel",)),
        name="merge_moe" if moe else "merge_dense",
    )(*args)


FFN_CHUNK = 768


def _ffn_body(h_ref, x_ref, wg_ref, wu_ref, wd_ref, o_ref):
    h = h_ref[...]
    acc = x_ref[...]
    for c, w in _col_chunks(D_FF, FFN_CHUNK):
        a = _dot(h, wg_ref[:, c:c + w])
        act = (a * _sigmoid(a) * _dot(h, wu_ref[:, c:c + w])).astype(BF16)
        acc = acc + _dot(act, wd_ref[c:c + w, :])
    o_ref[...] = acc


def _dense_ffn(h2, x1, wg, wu, wd):
    n = h2.shape[0]
    tm = 512
    row = lambda i: (i, 0)
    resident = lambda shape: pl.BlockSpec(shape, lambda i: (0, 0), pipeline_mode=pl.Buffered(1))
    return pl.pallas_call(
        _ffn_body,
        out_shape=jax.ShapeDtypeStruct((n, D_MODEL), F32),
        grid=(n // tm,),
        in_specs=[pl.BlockSpec((tm, D_MODEL), row),
                  pl.BlockSpec((tm, D_MODEL), row),
                  resident((D_MODEL, D_FF)), resident((D_MODEL, D_FF)), resident((D_FF, D_MODEL))],
        out_specs=pl.BlockSpec((tm, D_MODEL), row),
        compiler_params=_cparams(("parallel",)),
        name="dense_ffn",
    )(h2, x1, wg, wu, wd)


MOE_TM = 512
PLAN_TB = 512


def _moe_rows(ntok):
    return 2 * ntok + N_EXPERTS * MOE_TM


def _plan_body(ri_ref, rank_ref, cnt_ref, carry_ref):
    i = pl.program_id(0)

    @pl.when(i == 0)
    def _():
        carry_ref[...] = jnp.zeros_like(carry_ref)

    ri = ri_ref[...]
    lane = lax.broadcasted_iota(I32, ri.shape, 1)
    e1 = ri[:, 0:1]
    e2 = ri[:, 1:2]
    oh = (jnp.where(lane < N_EXPERTS, e1, e2 + N_EXPERTS) == lane) & (lane < 2 * N_EXPERTS)
    ohf = jnp.where(oh, 1.0, 0.0)
    tr = lax.broadcasted_iota(I32, (PLAN_TB, PLAN_TB), 0)
    tcol = lax.broadcasted_iota(I32, (PLAN_TB, PLAN_TB), 1)
    tri = jnp.where(tcol < tr, 1.0, 0.0).astype(BF16)
    excl = _dot(tri, ohf.astype(BF16)) + carry_ref[...]
    mine = jnp.where(oh, excl, 0.0)
    r0 = jnp.sum(jnp.where(lane < N_EXPERTS, mine, 0.0), axis=-1, keepdims=True)
    r1 = jnp.sum(jnp.where(lane >= N_EXPERTS, mine, 0.0), axis=-1, keepdims=True)
    rank_ref[...] = jnp.where(lane == 0, r0, jnp.where(lane == 1, r1, 0.0))
    carry_ref[...] += jnp.sum(ohf, axis=0, keepdims=True)
    cnt_ref[...] = carry_ref[...]


def _moe_plan(ri):
    n = ri.shape[0]
    return pl.pallas_call(
        _plan_body,
        out_shape=(jax.ShapeDtypeStruct((n, LANES), F32), jax.ShapeDtypeStruct((1, LANES), F32)),
        grid=(n // PLAN_TB,),
        in_specs=[pl.BlockSpec((PLAN_TB, LANES), lambda i: (i, 0))],
        out_specs=(pl.BlockSpec((PLAN_TB, LANES), lambda i: (i, 0)),
                   pl.BlockSpec((1, LANES), lambda i: (0, 0))),
        scratch_shapes=[pltpu.VMEM((1, LANES), F32)],
        compiler_params=_cparams(("arbitrary",)),
        name="moe_plan",
    )(ri)


DISPATCH_TB = 1024


SUBLANES = 8


def _wait_rows(hbm_ref, nrows, sem):
    whole = hbm_ref.at[pl.ds(0, nrows)]
    pltpu.make_async_copy(whole, whole, sem).wait()


def _dispatch_body(pos_ref, h_ref, xs_in_ref, xs_ref, sem):
    del xs_in_ref
    base = pl.program_id(0) * DISPATCH_TB

    def issue(g, carry):
        for u in range(SUBLANES):
            tok = base + g * SUBLANES + u
            for k in range(2):
                pltpu.make_async_copy(h_ref.at[g, pl.ds(u, 1)],
                                      xs_ref.at[pl.ds(pos_ref[2 * tok + k], 1)], sem).start(priority=k)
        return carry

    lax.fori_loop(0, DISPATCH_TB // SUBLANES, issue, 0)

    for _ in range(2):
        _wait_rows(xs_ref, DISPATCH_TB, sem)


def _moe_dispatch(pos, hp):
    n, c = hp.shape
    xs0 = jnp.zeros((_moe_rows(n), c), U32)
    hp = hp.reshape(n // SUBLANES, SUBLANES, c)
    return pl.pallas_call(
        _dispatch_body,
        out_shape=jax.ShapeDtypeStruct(xs0.shape, U32),
        grid_spec=pltpu.PrefetchScalarGridSpec(
            num_scalar_prefetch=1,
            grid=(n // DISPATCH_TB,),
            in_specs=[pl.BlockSpec((DISPATCH_TB // SUBLANES, SUBLANES, c), lambda i, p: (i, 0, 0)),
                      pl.BlockSpec(memory_space=pl.ANY)],
            out_specs=pl.BlockSpec(memory_space=pl.ANY),
            scratch_shapes=[pltpu.SemaphoreType.DMA(())]),
        input_output_aliases={2: 0},
        compiler_params=_cparams(("arbitrary",)),
        name="moe_dispatch",
    )(pos, hp, xs0)


COMBINE_TB = 512


def _combine_body(pos_ref, x_ref, rw_ref, ys_ref, o_ref, buf, sem):
    base = pl.program_id(0) * COMBINE_TB

    def issue(g, carry):
        for u in range(SUBLANES):
            tok = base + g * SUBLANES + u
            for k in range(2):
                pltpu.make_async_copy(ys_ref.at[pl.ds(pos_ref[2 * tok + k], 1)],
                                      buf.at[k, g, pl.ds(u, 1)], sem).start(priority=k)
        return carry

    lax.fori_loop(0, COMBINE_TB // SUBLANES, issue, 0)
    for _ in range(2):
        _wait_rows(ys_ref, COMBINE_TB, sem)
    rw = rw_ref[...]
    rows = lambda k: _unpack_bf16_pairs(buf[k].reshape(COMBINE_TB, buf.shape[-1]))
    o_ref[...] = x_ref[...] + rw[:, 0:1] * rows(0) + rw[:, 1:2] * rows(1)


def _moe_combine(pos, x1, rw, ys):
    n = x1.shape[0]
    c = ys.shape[1]
    return pl.pallas_call(
        _combine_body,
        out_shape=jax.ShapeDtypeStruct((n, D_MODEL), F32),
        grid_spec=pltpu.PrefetchScalarGridSpec(
            num_scalar_prefetch=1,
            grid=(n // COMBINE_TB,),
            in_specs=[pl.BlockSpec((COMBINE_TB, D_MODEL), lambda i, p: (i, 0)),
                      pl.BlockSpec((COMBINE_TB, LANES), lambda i, p: (i, 0)),
                      pl.BlockSpec(memory_space=pl.ANY)],
            out_specs=pl.BlockSpec((COMBINE_TB, D_MODEL), lambda i, p: (i, 0)),
            scratch_shapes=[pltpu.VMEM((2, COMBINE_TB // SUBLANES, SUBLANES, c), U32),
                            pltpu.SemaphoreType.DMA(())]),
        compiler_params=_cparams(("arbitrary",)),
        name="moe_combine",
    )(pos, x1, rw, ys)


MOE_TF = 1792


def _experts_body(te_ref, nv_ref, xs_ref, wg_ref, wu_ref, wd_ref, ys_ref, xb_ref, acc_ref):
    i = pl.program_id(0)
    f = pl.program_id(1)
    nf = pl.num_programs(1)

    @pl.when(i < nv_ref[0])
    def _():
        @pl.when(f == 0)
        def _():
            xb_ref[...] = _unpack_bf16_pairs(xs_ref[...]).astype(BF16)

        h = xb_ref[...]
        part = None
        for c, w in _col_chunks(MOE_TF, 1024):
            a = _dot(h, wg_ref[0, :, c:c + w])
            act = (a * _sigmoid(a) * _dot(h, wu_ref[0, :, c:c + w])).astype(BF16)
            pc = _dot(act, wd_ref[0, c:c + w, :])
            part = pc if part is None else part + pc

        @pl.when(f == 0)
        def _():
            acc_ref[...] = part

        @pl.when(f > 0)
        def _():
            acc_ref[...] += part

        @pl.when(f == nf - 1)
        def _():
            ys_ref[...] = _pack_bf16_pairs(acc_ref[...])

    @pl.when((i >= nv_ref[0]) & (f == nf - 1))
    def _():
        ys_ref[...] = jnp.zeros_like(ys_ref)


def _moe_experts(tile_expert, n_valid, xs, wg, wu, wd):
    rows, c = xs.shape
    nt = rows // MOE_TM
    nf = D_FF_EXPERT // MOE_TF

    def tile(i, nv):
        return jnp.minimum(i, nv[0] - 1)

    def fblk(i, f, nv):
        return jnp.where(i < nv[0], f, nf - 1)

    return pl.pallas_call(
        _experts_body,
        out_shape=jax.ShapeDtypeStruct((rows, c), U32),
        grid_spec=pltpu.PrefetchScalarGridSpec(
            num_scalar_prefetch=2,
            grid=(nt, nf),
            in_specs=[pl.BlockSpec((MOE_TM, c), lambda i, f, te, nv: (tile(i, nv), 0)),
                      pl.BlockSpec((1, D_MODEL, MOE_TF),
                                   lambda i, f, te, nv: (te[tile(i, nv)], 0, fblk(i, f, nv))),
                      pl.BlockSpec((1, D_MODEL, MOE_TF),
                                   lambda i, f, te, nv: (te[tile(i, nv)], 0, fblk(i, f, nv))),
                      pl.BlockSpec((1, MOE_TF, D_MODEL),
                                   lambda i, f, te, nv: (te[tile(i, nv)], fblk(i, f, nv), 0))],
            out_specs=pl.BlockSpec((MOE_TM, c), lambda i, f, te, nv: (i, 0)),
            scratch_shapes=[pltpu.VMEM((MOE_TM, D_MODEL), BF16), pltpu.VMEM((MOE_TM, D_MODEL), F32)]),
        compiler_params=_cparams(("arbitrary", "arbitrary")),
        name="moe_experts",
    )(tile_expert, n_valid, xs, wg, wu, wd)


def _moe_ffn(x1, hp, ri, rw, wg, wu, wd):
    rank, cnt = _moe_plan(ri)
    c0 = cnt[0, :N_EXPERTS].astype(I32)
    c1 = cnt[0, N_EXPERTS:2 * N_EXPERTS].astype(I32)
    padded = ((c0 + c1 + MOE_TM - 1) // MOE_TM) * MOE_TM
    ends = jnp.cumsum(padded)
    off = ends - padded
    e1, e2 = ri[:, 0], ri[:, 1]
    pos0 = off[e1] + rank[:, 0].astype(I32)
    pos1 = off[e2] + c0[e2] + rank[:, 1].astype(I32)
    pos = jnp.stack([pos0, pos1], axis=1).reshape(-1)
    nt = _moe_rows(x1.shape[0]) // MOE_TM
    tile_start = jnp.arange(nt, dtype=I32) * MOE_TM
    tile_expert = jnp.minimum(jnp.sum(tile_start[:, None] >= ends[None, :], axis=1),
                              N_EXPERTS - 1).astype(I32)
    n_valid = (ends[-1:] // MOE_TM).astype(I32)
    xs = _moe_dispatch(pos, hp)
    ys = _moe_experts(tile_expert, n_valid, xs, wg, wu, wd)
    return _moe_combine(pos, x1, rw, ys)


def _head_consts():
    lane = jnp.arange(ATT_OUT)
    bd = jnp.where((lane[:, None] // ATT_HEAD_DIM) == (lane[None, :] // ATT_HEAD_DIM),
                   1.0 / ATT_HEAD_DIM, 0.0).astype(BF16)
    perm = (lane[:, None] == (lane[None, :] ^ (ATT_HEAD_DIM // 2))).astype(BF16)
    return bd, perm


def _head_gains(g):
    full = jnp.tile(g.astype(F32), ATT_SLOTS).reshape(1, ATT_OUT)
    half = ATT_HEAD_DIM // 2
    swapped = jnp.tile(jnp.concatenate([g[half:], g[:half]]).astype(F32), ATT_SLOTS).reshape(1, ATT_OUT)
    return full, swapped


def kernel(x, mem, positions, norm_mix, w_in, ssm_a_re, ssm_a_im, ssm_log_dt, ssm_b_re, ssm_b_im,
           ssm_c_re, ssm_c_im, ssm_d, ssm_w_glu, w_ssm_out, att_q_norm, att_k_norm, w_att_out,
           norm_mem, w_mem_kv, mem_q_norm, mem_k_norm, w_mem_out, w_o, norm_ffn, ffn_w_gate,
           ffn_w_up, ffn_w_down, moe_w_router, moe_w_gate, moe_w_up, moe_w_down):
    bsz, seq, d = x.shape
    ntok = bsz * seq
    depth = w_in.shape[0]
    cos_t, sin_t = _rope_tables(positions)
    bd, perm = _head_consts()
    mem2d = mem.reshape(bsz * MEM_LEN, d)
    x2d = x.reshape(ntok, d)
    for i in range(depth):
        wi = w_in[i]
        g_mix = norm_mix[i].reshape(1, d)

        wut = jnp.transpose(wi[:, :SSM_WIDTH]).astype(BF16)
        ut, qkv, memq, gates = _inproj_main(x2d, g_mix, wut, wi.astype(BF16))

        ops = _ssm_operators(ssm_a_re[i], ssm_a_im[i], ssm_log_dt[i], ssm_b_re[i], ssm_b_im[i],
                             ssm_c_re[i], ssm_c_im[i], ssm_d[i])
        yt = _ssm_scan(ut, ops, bsz, seq)

        gq, gqs = _head_gains(att_q_norm[i])
        gk, gks = _head_gains(att_k_norm[i])
        att = []
        for gi, (_, dil) in enumerate(DIL_PAIRS):
            att.extend(_dilated_attention_group(qkv, gi, cos_t, sin_t, gq, gqs, gk, gks, bd, perm,
                                                bsz, seq, dil))

        kmem, vmem = _memory_kv(mem2d, norm_mem[i].reshape(1, d), w_mem_kv[i].astype(BF16),
                                mem_k_norm[i].reshape(1, X_HEAD_DIM))

        wts = (jnp.transpose(ssm_w_glu[i]).astype(BF16), w_ssm_out[i].astype(BF16),
               w_att_out[i].astype(BF16), w_mem_out[i].astype(BF16), w_o[i].astype(BF16),
               mem_q_norm[i].reshape(1, X_HEAD_DIM), norm_ffn[i].reshape(1, d))
        j = i // 2
        if i % 2 == 0:
            x1, h2 = _merge(x2d, memq, gates, yt, att, kmem, vmem, wts, bsz, seq)
            x2d = _dense_ffn(h2, x1, ffn_w_gate[j].astype(BF16), ffn_w_up[j].astype(BF16),
                             ffn_w_down[j].astype(BF16))
        else:
            wr = jnp.zeros((d, LANES), F32).at[:, :N_EXPERTS].set(moe_w_router[j])
            wr_hi = wr.astype(BF16)
            wr_lo = (wr - wr_hi.astype(F32)).astype(BF16)
            x1, hp, ri, rw = _merge(x2d, memq, gates, yt, att, kmem, vmem, wts, bsz, seq,
                                    router=(wr_hi, wr_lo))
            x2d = _moe_ffn(x1, hp, ri, rw, moe_w_gate[j].astype(BF16), moe_w_up[j].astype(BF16),
                           moe_w_down[j].astype(BF16))
    return x2d.reshape(bsz, seq, d)
```

```python
import functools
import math

import jax
import jax.numpy as jnp
from jax import lax
from jax.experimental import pallas as pl
from jax.experimental.pallas import tpu as pltpu

F32 = jnp.float32
BF16 = jnp.bfloat16
I32 = jnp.int32
U32 = jnp.uint32

EPS = 1e-6
D_MODEL = 1024
MEM_LEN = 256
SSM_WIDTH = 512
SSM_GROUP = 16
SSM_GROUPS = 32
SSM_STATE = 64
ATT_HEAD_DIM = 64
ATT_SLOTS = 4
DIL_PAIRS = ((128, 1), (512, 4), (2048, 16))
ATT_WIDTH = 768
ATT_OUT = 256
BLOCK = 128
ROPE_THETA = 10000.0
X_HEADS = 4
X_HEAD_DIM = 128
X_WIDTH = 512
D_FF = 2816
N_EXPERTS = 8
D_FF_EXPERT = 3584

LANES = 128
SSM_CHUNK = 128
QKV_W = 3 * ATT_OUT
VMEM_LIMIT = 56 * 1024 * 1024


def _cparams(sem, vmem=VMEM_LIMIT):
    return pltpu.CompilerParams(dimension_semantics=sem, vmem_limit_bytes=vmem)


def _rms(x, g):
    ms = jnp.mean(x * x, axis=-1, keepdims=True)
    return x * lax.rsqrt(ms + EPS) * g


def _sigmoid(x):
    return 0.5 * jnp.tanh(0.5 * x) + 0.5


def _dot(a, b):
    return jnp.dot(a, b, preferred_element_type=F32)


def _dot_nt(a, b):
    return lax.dot_general(a, b, (((1,), (1,)), ((), ())), preferred_element_type=F32)


def _rope_body(pos_ref, inv_ref, sgn_ref, cos_ref, sin_ref):
    ang = pos_ref[...].astype(F32) * inv_ref[...]
    cos_ref[...] = jnp.cos(ang)
    sin_ref[...] = jnp.sin(ang) * sgn_ref[...]


def _rope_tables(positions):
    n = positions.size
    half = ATT_HEAD_DIM // 2
    inv = ROPE_THETA ** (-jnp.arange(half, dtype=F32) / half)
    inv_row = jnp.tile(inv, LANES // half).reshape(1, LANES)
    lane = jnp.arange(LANES)
    sgn_row = jnp.where((lane % ATT_HEAD_DIM) < half, -1.0, 1.0).astype(F32).reshape(1, LANES)
    tm = 2048
    return pl.pallas_call(
        _rope_body,
        out_shape=(jax.ShapeDtypeStruct((n, LANES), F32), jax.ShapeDtypeStruct((n, LANES), F32)),
        grid=(n // tm,),
        in_specs=[pl.BlockSpec((tm, 1), lambda i: (i, 0)),
                  pl.BlockSpec((1, LANES), lambda i: (0, 0)),
                  pl.BlockSpec((1, LANES), lambda i: (0, 0))],
        out_specs=(pl.BlockSpec((tm, LANES), lambda i: (i, 0)),
                   pl.BlockSpec((tm, LANES), lambda i: (i, 0))),
        compiler_params=_cparams(("parallel",)),
        name="rope_tables",
    )(positions.reshape(n, 1), inv_row, sgn_row)


def _col_chunks(width, step=512):
    return [(c, min(step, width - c)) for c in range(0, width, step)]


def _inproj_main_body(x_ref, g_ref, wut_ref, w_ref, ut_ref, qkv_ref, memq_ref, gate_ref):
    h = _rms(x_ref[...], g_ref[...]).astype(BF16)
    ut_ref[...] = _dot_nt(wut_ref[...], h).astype(BF16)
    for gi in range(len(DIL_PAIRS)):
        for j in range(3):
            src = SSM_WIDTH + j * ATT_WIDTH + gi * ATT_OUT
            dst = gi * QKV_W + j * ATT_OUT
            qkv_ref[:, dst:dst + ATT_OUT] = _dot(h, w_ref[:, src:src + ATT_OUT]).astype(BF16)
    col = SSM_WIDTH + 3 * ATT_WIDTH
    for ref in (memq_ref, gate_ref):
        for c, w in _col_chunks(ref.shape[1]):
            ref[:, c:c + w] = _dot(h, w_ref[:, col + c:col + c + w]).astype(BF16)
        col += ref.shape[1]


def _inproj_main(x2d, g, wut, wmain):
    n = x2d.shape[0]
    tm = 512
    row = lambda i: (i, 0)
    const = lambda i: (0, 0)
    widths = (3 * QKV_W, X_WIDTH, 3 * D_MODEL)
    return pl.pallas_call(
        _inproj_main_body,
        out_shape=(jax.ShapeDtypeStruct((SSM_WIDTH, n), BF16),)
        + tuple(jax.ShapeDtypeStruct((n, w), BF16) for w in widths),
        grid=(n // tm,),
        in_specs=[pl.BlockSpec((tm, D_MODEL), row),
                  pl.BlockSpec((1, D_MODEL), const),
                  pl.BlockSpec((SSM_WIDTH, D_MODEL), const, pipeline_mode=pl.Buffered(1)),
                  pl.BlockSpec(wmain.shape, const, pipeline_mode=pl.Buffered(1))],
        out_specs=(pl.BlockSpec((SSM_WIDTH, tm), lambda i: (0, i)),)
        + tuple(pl.BlockSpec((tm, w), row) for w in widths),
        compiler_params=_cparams(("parallel",)),
        name="inproj_main",
    )(x2d, g, wut, wmain)


def _qk_prep(xf, cos2, sin2, g, gs, bd, perm, scale):
    xb = xf.astype(BF16)
    ms = _dot((xf * xf).astype(BF16), bd)
    xs = _dot(xb, perm)
    y = lax.rsqrt(ms + EPS) * scale * (xf * (g * cos2) + xs * (gs * sin2))
    return y.astype(BF16)


def _attn_body(qkv_ref, cos_ref, sin_ref, gq_ref, gqs_ref, gk_ref, gks_ref, bd_ref, perm_ref,
               o_ref, lse_ref, sbuf, qbuf, kbuf, vbuf, *, dil, tq):
    j = pl.program_id(1)

    @pl.when(j == 0)
    def _():
        kbuf[:, 0:BLOCK, :] = jnp.zeros((dil, BLOCK, ATT_OUT), BF16)
        vbuf[:, 0:BLOCK, :] = jnp.zeros((dil, BLOCK, ATT_OUT), BF16)

    @pl.when(j > 0)
    def _():
        kbuf[:, 0:BLOCK, :] = kbuf[:, tq:tq + BLOCK, :]
        vbuf[:, 0:BLOCK, :] = vbuf[:, tq:tq + BLOCK, :]

    bd = bd_ref[...]
    perm = perm_ref[...]
    lane = lax.broadcasted_iota(I32, (BLOCK, LANES), 1)
    low = lane < ATT_HEAD_DIM
    qi = lax.broadcasted_iota(I32, (BLOCK, 2 * BLOCK), 0) + BLOCK
    ki = lax.broadcasted_iota(I32, (BLOCK, 2 * BLOCK), 1)
    off = qi - ki
    band = (off >= 0) & (off <= BLOCK)
    band_first = band & ((ki >= BLOCK) | (j > 0))
    for c in range(QKV_W // LANES):
        sbuf[c] = qkv_ref[:, c * LANES:(c + 1) * LANES].astype(F32)

    def rows(start, size):
        return pl.ds(start, size, stride=dil) if dil > 1 else pl.ds(start, size)

    def planes(first, sel):
        return jnp.concatenate([sbuf[first, sel, :], sbuf[first + 1, sel, :]], axis=1)

    def prep(r, carry):
        cos = cos_ref[rows(r, tq), :]
        sin = sin_ref[rows(r, tq), :]
        cos2 = jnp.concatenate([cos, cos], axis=1)
        sin2 = jnp.concatenate([sin, sin], axis=1)
        cur = rows(r, tq)
        qbuf[r] = _qk_prep(planes(0, cur), cos2, sin2, gq_ref[...], gqs_ref[...],
                           bd, perm, ATT_HEAD_DIM ** -0.5)
        kbuf[r, BLOCK:, :] = _qk_prep(planes(2, cur), cos2, sin2,
                                      gk_ref[...], gks_ref[...], bd, perm, 1.0)
        vbuf[r, BLOCK:, :] = planes(4, cur).astype(BF16)
        return carry

    if dil == 1:
        prep(0, 0)
    else:
        lax.fori_loop(0, dil, prep, 0, unroll=2)

    for r in range(dil):
        for s in range(tq // BLOCK):
            row0 = s * BLOCK
            valid = band_first if s == 0 else band
            dst = rows(r + row0 * dil, BLOCK)
            for p in range(ATT_OUT // LANES):
                cols = slice(p * LANES, (p + 1) * LANES)
                qp = qbuf[r, row0:row0 + BLOCK, cols]
                kp = kbuf[r, row0:row0 + 2 * BLOCK, cols]
                vp = vbuf[r, row0:row0 + 2 * BLOCK, cols]
                outs, lses = [], []
                for h in range(2):
                    qm = jnp.where(low if h == 0 else ~low, qp, jnp.zeros_like(qp))
                    sc = jnp.where(valid, _dot_nt(qm, kp), -1e30)
                    m = jnp.max(sc, axis=-1, keepdims=True)
                    pr = jnp.exp(sc - m)
                    den = jnp.sum(pr, axis=-1, keepdims=True)
                    outs.append(_dot(pr.astype(BF16), vp) / den)
                    lses.append(m + jnp.log(den))
                o_ref[p, dst, :] = jnp.where(low, outs[0], outs[1])
                lse_ref[p, dst, :] = jnp.where(low, lses[0], lses[1])


def _dilated_attention_group(qkv_all, gi, cos_t, sin_t, gq, gqs, gk, gks, bd, perm, bsz, seq, dil):
    ntok = bsz * seq
    tt = max(1024, BLOCK * dil)
    tq = tt // dil
    nblk = seq // tt
    const = lambda b, j: (0, 0)
    return pl.pallas_call(
        functools.partial(_attn_body, dil=dil, tq=tq),
        out_shape=(jax.ShapeDtypeStruct((ATT_OUT // LANES, ntok, LANES), F32),
                   jax.ShapeDtypeStruct((ATT_OUT // LANES, ntok, LANES), F32)),
        grid=(bsz, nblk),
        in_specs=[pl.BlockSpec((tt, QKV_W), lambda b, j: (b * nblk + j, gi)),
                  pl.BlockSpec((tt, LANES), lambda b, j: (b * nblk + j, 0)),
                  pl.BlockSpec((tt, LANES), lambda b, j: (b * nblk + j, 0)),
                  pl.BlockSpec((1, ATT_OUT), const), pl.BlockSpec((1, ATT_OUT), const),
                  pl.BlockSpec((1, ATT_OUT), const), pl.BlockSpec((1, ATT_OUT), const),
                  pl.BlockSpec((ATT_OUT, ATT_OUT), const), pl.BlockSpec((ATT_OUT, ATT_OUT), const)],
        out_specs=(pl.BlockSpec((ATT_OUT // LANES, tt, LANES), lambda b, j: (0, b * nblk + j, 0)),
                   pl.BlockSpec((ATT_OUT // LANES, tt, LANES), lambda b, j: (0, b * nblk + j, 0))),
        scratch_shapes=[pltpu.VMEM((QKV_W // LANES, tt, LANES), F32),
                        pltpu.VMEM((dil, tq, ATT_OUT), BF16),
                        pltpu.VMEM((dil, BLOCK + tq, ATT_OUT), BF16),
                        pltpu.VMEM((dil, BLOCK + tq, ATT_OUT), BF16)],
        compiler_params=_cparams(("arbitrary", "arbitrary")),
        name=f"dilated_attn{dil}",
    )(qkv_all, cos_t, sin_t, gq, gqs, gk, gks, bd, perm)


def _ssm_body(u_ref, ktab_ref, w_ref, v_ref, lam_ref, dvec_ref, y_ref, m_ref, m2_ref, sloc_ref,
              ssw_ref, sin_ref, yin_ref, *, bsz, cpb):
    tc = SSM_CHUNK
    row = lax.broadcasted_iota(I32, (tc, tc), 0)
    col = lax.broadcasted_iota(I32, (tc, tc), 1)
    causal = col >= row

    u = jnp.concatenate([u_ref[c] for c in range(SSM_GROUP)], axis=1)
    sloc = _dot(u, w_ref[0])
    sloc_ref[...] = sloc
    ssw_ref[...] = pltpu.roll(sloc, SSM_STATE, 1)

    a1 = lam_ref[0, 0:1, :]
    a2 = lam_ref[0, 1:2, :]
    s = jnp.zeros((bsz, 2 * SSM_STATE), F32)
    sw = s
    sin_ref[pl.ds(0, bsz, stride=cpb), :] = s
    for k in range(1, cpb):
        prev = pl.ds(k - 1, bsz, stride=cpb)
        s, sw = (a1 * s + a2 * sw + sloc_ref[prev, :], a1 * sw - a2 * s + ssw_ref[prev, :])
        sin_ref[pl.ds(k, bsz, stride=cpb), :] = s

    s_in = sin_ref[...].astype(BF16)
    nblk = SSM_GROUP // 2

    def y_in(c):
        return _dot(s_in, v_ref[0, c]) + dvec_ref[0, :, c * tc:(c + 1) * tc] * u_ref[c].astype(F32)

    for cb in range(nblk):
        yin_ref[cb] = jnp.concatenate([y_in(2 * cb), y_in(2 * cb + 1)], axis=1)

    def build(cb, dst):
        for cp in range(SSM_GROUP):
            for h in range(2):
                kv = ktab_ref[0, pl.ds(cp * SSM_GROUP + 2 * cb + h, 1), :]
                tile = pltpu.roll(jnp.broadcast_to(kv, (tc, tc)), 0, 1, stride=1, stride_axis=0)
                dst[cp * tc:(cp + 1) * tc, h * tc:(h + 1) * tc] = (
                    jnp.where(causal, tile, 0.0).astype(BF16))

    def multiply(cb, src):
        y = jax.nn.gelu(_dot(u, src[...]) + yin_ref[cb], approximate=True)
        y_ref[2 * cb] = y[:, :tc].astype(BF16)
        y_ref[2 * cb + 1] = y[:, tc:].astype(BF16)

    build(0, m_ref)

    def pair(i, carry):
        build(2 * i + 1, m2_ref)
        multiply(2 * i, m_ref)
        build(jnp.minimum(2 * i + 2, nblk - 1), m_ref)
        multiply(2 * i + 1, m2_ref)
        return carry

    lax.fori_loop(0, nblk // 2, pair, 0)


def _ssm_operators(a_re, a_im, log_dt, b_re, b_im, c_re, c_im, d_skip):
    tc = SSM_CHUNK
    ng = a_re.shape[0]
    lam = lax.complex(a_re.astype(F32), a_im.astype(F32))
    dt = jnp.exp(log_dt.astype(F32))[:, None]
    lam_dt = lam * dt
    lam_bar = jnp.exp(lam_dt)
    b = lax.complex(b_re.astype(F32), b_im.astype(F32))
    b_bar = ((lam_bar - 1.0) / lam)[..., None] * b
    c = lax.complex(c_re.astype(F32), c_im.astype(F32))
    k = jnp.arange(tc + 1, dtype=F32)
    pw = jnp.exp(lam_dt[:, None, :] * k[None, :, None])
    ktab = jnp.einsum('gcp,gkp,gpd->gdck', c, pw[:, :tc], b_bar).real
    ktab = ktab.reshape(ng, SSM_GROUP * SSM_GROUP, tc).astype(F32)
    wc = jnp.einsum('gjp,gpd->gdjp', pw[:, tc - 1::-1][:, :tc], b_bar)
    wc = wc.reshape(ng, SSM_GROUP * tc, SSM_STATE)
    w = jnp.concatenate([wc.real, wc.imag], axis=-1).astype(BF16)
    vc = jnp.einsum('gcp,gtp->gcpt', c, pw[:, 1:tc + 1])
    v = jnp.concatenate([vc.real, -vc.imag], axis=2).astype(BF16)
    lt = pw[:, tc]
    lam_rows = jnp.stack([jnp.concatenate([lt.real, lt.real], -1),
                          jnp.concatenate([-lt.imag, lt.imag], -1)], axis=1).astype(F32)
    dvec = jnp.repeat(d_skip.astype(F32).reshape(ng, SSM_GROUP), tc, axis=1)
    return ktab, w, v, lam_rows, dvec.reshape(ng, 1, SSM_GROUP * tc)


def _ssm_scan(ut, ops, layer, bsz, seq):
    ktab, w, v, lam_rows, dvec = ops
    ntok = bsz * seq
    tc = SSM_CHUNK
    nch = ntok // tc
    u3 = ut.reshape(SSM_WIDTH, nch, tc)
    gmap = lambda g: (g, 0, 0)
    tmap = lambda g: (layer * SSM_GROUPS + g, 0, 0)
    y3 = pl.pallas_call(
        functools.partial(_ssm_body, bsz=bsz, cpb=seq // tc),
        out_shape=jax.ShapeDtypeStruct((SSM_WIDTH, nch, tc), BF16),
        grid=(SSM_GROUPS,),
        in_specs=[pl.BlockSpec((SSM_GROUP, nch, tc), gmap),
                  pl.BlockSpec((1, SSM_GROUP * SSM_GROUP, tc), tmap),
                  pl.BlockSpec((1, SSM_GROUP * tc, 2 * SSM_STATE), tmap),
                  pl.BlockSpec((1, SSM_GROUP, 2 * SSM_STATE, tc),
                               lambda g: (layer * SSM_GROUPS + g, 0, 0, 0)),
                  pl.BlockSpec((1, 2, 2 * SSM_STATE), tmap),
                  pl.BlockSpec((1, 1, SSM_GROUP * tc), tmap)],
        out_specs=pl.BlockSpec((SSM_GROUP, nch, tc), gmap),
        scratch_shapes=[pltpu.VMEM((SSM_GROUP * tc, 2 * tc), BF16),
                        pltpu.VMEM((SSM_GROUP * tc, 2 * tc), BF16),
                        pltpu.VMEM((nch, 2 * SSM_STATE), F32),
                        pltpu.VMEM((nch, 2 * SSM_STATE), F32),
                        pltpu.VMEM((nch, 2 * SSM_STATE), F32),
                        pltpu.VMEM((SSM_GROUP // 2, nch, 2 * tc), F32)],
        compiler_params=_cparams(("parallel",)),
        name="ssm_scan",
    )(u3, ktab, w, v, lam_rows, dvec)
    return y3.reshape(SSM_WIDTH, ntok)


def _memkv_body(mem_ref, g_ref, w_ref, gk_ref, k_ref, v_ref):
    h = _rms(mem_ref[...], g_ref[...]).astype(BF16)
    kv = _dot(h, w_ref[...])
    for hd in range(X_HEADS):
        cols = slice(hd * X_HEAD_DIM, (hd + 1) * X_HEAD_DIM)
        k_ref[:, cols] = _rms(kv[:, cols], gk_ref[...]).astype(BF16)
    v_ref[...] = kv[:, X_WIDTH:].astype(BF16)


def _memory_kv(mem2d, g, w_kv, gk):
    m = mem2d.shape[0]
    tm = MEM_LEN
    return pl.pallas_call(
        _memkv_body,
        out_shape=(jax.ShapeDtypeStruct((m, X_WIDTH), BF16), jax.ShapeDtypeStruct((m, X_WIDTH), BF16)),
        grid=(m // tm,),
        in_specs=[pl.BlockSpec((tm, D_MODEL), lambda i: (i, 0)),
                  pl.BlockSpec((1, D_MODEL), lambda i: (0, 0)),
                  pl.BlockSpec((D_MODEL, 2 * X_WIDTH), lambda i: (0, 0)),
                  pl.BlockSpec((1, X_HEAD_DIM), lambda i: (0, 0))],
        out_specs=(pl.BlockSpec((tm, X_WIDTH), lambda i: (i, 0)),
                   pl.BlockSpec((tm, X_WIDTH), lambda i: (i, 0))),
        compiler_params=_cparams(("parallel",)),
        name="memory_kv",
    )(mem2d, g, w_kv, gk)


def _pack_bf16_pairs(x):
    c = x.shape[1] // 2
    bits = pltpu.bitcast(x.astype(BF16).astype(F32), U32)
    return (bits[:, :c] & jnp.uint32(0xFFFF0000)) | (bits[:, c:] >> 16)


def _unpack_bf16_pairs(p):
    hi = pltpu.bitcast(p & jnp.uint32(0xFFFF0000), F32)
    lo = pltpu.bitcast(p << 16, F32)
    return jnp.concatenate([hi, lo], axis=1)


def _merge_body(*refs, moe):
    (x_ref, memq_ref, gate_ref, yt_ref, o0_ref, l0_ref, o1_ref, l1_ref, o2_ref, l2_ref,
     km_ref, vm_ref, wglut_ref, wso_ref, wao_ref, wmo_ref, wo_ref, gmq_ref, gffn_ref) = refs[:19]
    if moe:
        wrh_ref, wrl_ref, x1_ref, hp_ref, ri_ref, rw_ref = refs[19:]
    else:
        x1_ref, h2_ref = refs[19:]

    ga = _dot(wglut_ref[...], yt_ref[...])
    glu = ga[:SSM_WIDTH] * _sigmoid(ga[SSM_WIDTH:])
    y_ssm = _dot(jnp.transpose(glu).astype(BF16), wso_ref[...])

    planes = []
    for p in range(ATT_OUT // LANES):
        l0, l1, l2 = l0_ref[p], l1_ref[p], l2_ref[p]
        mx = jnp.maximum(jnp.maximum(l0, l1), l2)
        e0, e1, e2 = jnp.exp(l0 - mx), jnp.exp(l1 - mx), jnp.exp(l2 - mx)
        planes.append((e0 * o0_ref[p] + e1 * o1_ref[p] + e2 * o2_ref[p]) / (e0 + e1 + e2))
    y_att = _dot(jnp.concatenate(planes, axis=1).astype(BF16), wao_ref[...])

    heads = []
    for hd in range(X_HEADS):
        cols = slice(hd * X_HEAD_DIM, (hd + 1) * X_HEAD_DIM)
        q = (_rms(memq_ref[:, cols].astype(F32), gmq_ref[...]) * (X_HEAD_DIM ** -0.5)).astype(BF16)
        sc = _dot_nt(q, km_ref[:, cols])
        m = jnp.max(sc, axis=-1, keepdims=True)
        pr = jnp.exp(sc - m)
        den = jnp.sum(pr, axis=-1, keepdims=True)
        heads.append(_dot(pr.astype(BF16), vm_ref[:, cols]) / den)
    y_mem = _dot(jnp.concatenate(heads, axis=1).astype(BF16), wmo_ref[...])

    d = D_MODEL
    merged = (_sigmoid(gate_ref[:, 0:d].astype(F32)) * y_ssm
              + _sigmoid(gate_ref[:, d:2 * d].astype(F32)) * y_att
              + _sigmoid(gate_ref[:, 2 * d:3 * d].astype(F32)) * y_mem)
    x1 = x_ref[...] + _dot(merged.astype(BF16), wo_ref[...])
    x1_ref[...] = x1
    h2 = _rms(x1, gffn_ref[...])
    if not moe:
        h2_ref[...] = h2.astype(BF16)
        return

    hp_ref[...] = _pack_bf16_pairs(h2)
    hi = h2.astype(BF16)
    lo = (h2 - hi.astype(F32)).astype(BF16)
    logits = _dot(hi, wrh_ref[...]) + _dot(hi, wrl_ref[...]) + _dot(lo, wrh_ref[...])
    lane = lax.broadcasted_iota(I32, logits.shape, 1)
    lg = jnp.where(lane < N_EXPERTS, logits, -jnp.inf)
    v1 = jnp.max(lg, axis=-1, keepdims=True)
    i1 = jnp.min(jnp.where(lg == v1, lane, LANES), axis=-1, keepdims=True)
    lg2 = jnp.where(lane == i1, -jnp.inf, lg)
    v2 = jnp.max(lg2, axis=-1, keepdims=True)
    i2 = jnp.min(jnp.where(lg2 == v2, lane, LANES), axis=-1, keepdims=True)
    e = jnp.exp(v2 - v1)
    ri_ref[...] = jnp.where(lane == 0, i1, jnp.where(lane == 1, i2, 0))
    rw_ref[...] = jnp.where(lane == 0, 1.0 / (1.0 + e), jnp.where(lane == 1, e / (1.0 + e), 0.0))


def _merge(x2d, memq, gates, yt, att, kmem, vmem, wts, bsz, seq, router=None):
    n = x2d.shape[0]
    tm = 512
    tpb = seq // tm
    moe = router is not None
    row = lambda i: (i, 0)
    const = lambda i: (0, 0)
    in_specs = [pl.BlockSpec((tm, D_MODEL), row),
                pl.BlockSpec((tm, X_WIDTH), row),
                pl.BlockSpec((tm, 3 * D_MODEL), row),
                pl.BlockSpec((SSM_WIDTH, tm), lambda i: (0, i))]
    in_specs += [pl.BlockSpec((ATT_OUT // LANES, tm, LANES), lambda i: (0, i, 0))] * 6
    in_specs += [pl.BlockSpec((MEM_LEN, X_WIDTH), lambda i: (i // tpb, 0))] * 2
    wglut, wso, wao, wmo, wo, gmq, gffn = wts
    in_specs += [pl.BlockSpec(w.shape, const) for w in (wglut, wso, wao, wmo, wo, gmq, gffn)]
    args = [x2d, memq, gates, yt, *att, kmem, vmem, wglut, wso, wao, wmo, wo, gmq, gffn]
    if moe:
        in_specs += [pl.BlockSpec(router[0].shape, const)] * 2
        args += list(router)
        out_shape = (jax.ShapeDtypeStruct((n, D_MODEL), F32),
                     jax.ShapeDtypeStruct((n, D_MODEL // 2), U32),
                     jax.ShapeDtypeStruct((n, LANES), I32),
                     jax.ShapeDtypeStruct((n, LANES), F32))
        out_specs = (pl.BlockSpec((tm, D_MODEL), row), pl.BlockSpec((tm, D_MODEL // 2), row),
                     pl.BlockSpec((tm, LANES), row), pl.BlockSpec((tm, LANES), row))
    else:
        out_shape = (jax.ShapeDtypeStruct((n, D_MODEL), F32), jax.ShapeDtypeStruct((n, D_MODEL), BF16))
        out_specs = (pl.BlockSpec((tm, D_MODEL), row), pl.BlockSpec((tm, D_MODEL), row))
    return pl.pallas_call(
        functools.partial(_merge_body, moe=moe),
        out_shape=out_shape,
        grid=(n // tm,),
        in_specs=in_specs,
        out_specs=out_specs,
        compiler_params=_cparams(("parallel",)),
        name="merge_moe" if moe else "merge_dense",
    )(*args)


FFN_CHUNK = 768


def _ffn_body(h_ref, x_ref, wg_ref, wu_ref, wd_ref, o_ref):
    h = h_ref[...]
    acc = x_ref[...]
    for c, w in _col_chunks(D_FF, FFN_CHUNK):
        a = _dot(h, wg_ref[:, c:c + w])
        act = (a * _sigmoid(a) * _dot(h, wu_ref[:, c:c + w])).astype(BF16)
        acc = acc + _dot(act, wd_ref[c:c + w, :])
    o_ref[...] = acc


def _dense_ffn(h2, x1, wg, wu, wd):
    n = h2.shape[0]
    tm = 512
    row = lambda i: (i, 0)
    resident = lambda shape: pl.BlockSpec(shape, lambda i: (0, 0), pipeline_mode=pl.Buffered(1))
    return pl.pallas_call(
        _ffn_body,
        out_shape=jax.ShapeDtypeStruct((n, D_MODEL), F32),
        grid=(n // tm,),
        in_specs=[pl.BlockSpec((tm, D_MODEL), row),
                  pl.BlockSpec((tm, D_MODEL), row),
                  resident((D_MODEL, D_FF)), resident((D_MODEL, D_FF)), resident((D_FF, D_MODEL))],
        out_specs=pl.BlockSpec((tm, D_MODEL), row),
        compiler_params=_cparams(("parallel",)),
        name="dense_ffn",
    )(h2, x1, wg, wu, wd)


MOE_TM = 512
PLAN_TB = 512


def _moe_rows(ntok):
    return 2 * ntok + N_EXPERTS * MOE_TM


def _plan_body(ri_ref, rank_ref, cnt_ref, carry_ref):
    i = pl.program_id(0)

    @pl.when(i == 0)
    def _():
        carry_ref[...] = jnp.zeros_like(carry_ref)

    ri = ri_ref[...]
    lane = lax.broadcasted_iota(I32, ri.shape, 1)
    e1 = ri[:, 0:1]
    e2 = ri[:, 1:2]
    oh = (jnp.where(lane < N_EXPERTS, e1, e2 + N_EXPERTS) == lane) & (lane < 2 * N_EXPERTS)
    ohf = jnp.where(oh, 1.0, 0.0)
    tr = lax.broadcasted_iota(I32, (PLAN_TB, PLAN_TB), 0)
    tcol = lax.broadcasted_iota(I32, (PLAN_TB, PLAN_TB), 1)
    tri = jnp.where(tcol < tr, 1.0, 0.0).astype(BF16)
    excl = _dot(tri, ohf.astype(BF16)) + carry_ref[...]
    mine = jnp.where(oh, excl, 0.0)
    r0 = jnp.sum(jnp.where(lane < N_EXPERTS, mine, 0.0), axis=-1, keepdims=True)
    r1 = jnp.sum(jnp.where(lane >= N_EXPERTS, mine, 0.0), axis=-1, keepdims=True)
    rank_ref[...] = jnp.where(lane == 0, r0, jnp.where(lane == 1, r1, 0.0))
    carry_ref[...] += jnp.sum(ohf, axis=0, keepdims=True)
    cnt_ref[...] = carry_ref[...]


def _moe_plan(ri):
    n = ri.shape[0]
    return pl.pallas_call(
        _plan_body,
        out_shape=(jax.ShapeDtypeStruct((n, LANES), F32), jax.ShapeDtypeStruct((1, LANES), F32)),
        grid=(n // PLAN_TB,),
        in_specs=[pl.BlockSpec((PLAN_TB, LANES), lambda i: (i, 0))],
        out_specs=(pl.BlockSpec((PLAN_TB, LANES), lambda i: (i, 0)),
                   pl.BlockSpec((1, LANES), lambda i: (0, 0))),
        scratch_shapes=[pltpu.VMEM((1, LANES), F32)],
        compiler_params=_cparams(("arbitrary",)),
        name="moe_plan",
    )(ri)


DISPATCH_TB = 1024


SUBLANES = 8


def _wait_rows(hbm_ref, nrows, sem):
    whole = hbm_ref.at[pl.ds(0, nrows)]
    pltpu.make_async_copy(whole, whole, sem).wait()


def _dispatch_body(pos_ref, h_ref, xs_in_ref, xs_ref, sem):
    del xs_in_ref
    base = pl.program_id(0) * DISPATCH_TB

    def issue(g, carry):
        for u in range(SUBLANES):
            tok = base + g * SUBLANES + u
            for k in range(2):
                pltpu.make_async_copy(h_ref.at[g, pl.ds(u, 1)],
                                      xs_ref.at[pl.ds(pos_ref[2 * tok + k], 1)], sem).start(priority=k)
        return carry

    lax.fori_loop(0, DISPATCH_TB // SUBLANES, issue, 0)

    for _ in range(2):
        _wait_rows(xs_ref, DISPATCH_TB, sem)


def _moe_dispatch(pos, hp):
    n, c = hp.shape
    xs0 = jnp.zeros((_moe_rows(n), c), U32)
    hp = hp.reshape(n // SUBLANES, SUBLANES, c)
    return pl.pallas_call(
        _dispatch_body,
        out_shape=jax.ShapeDtypeStruct(xs0.shape, U32),
        grid_spec=pltpu.PrefetchScalarGridSpec(
            num_scalar_prefetch=1,
            grid=(n // DISPATCH_TB,),
            in_specs=[pl.BlockSpec((DISPATCH_TB // SUBLANES, SUBLANES, c), lambda i, p: (i, 0, 0)),
                      pl.BlockSpec(memory_space=pl.ANY)],
            out_specs=pl.BlockSpec(memory_space=pl.ANY),
            scratch_shapes=[pltpu.SemaphoreType.DMA(())]),
        input_output_aliases={2: 0},
        compiler_params=_cparams(("arbitrary",)),
        name="moe_dispatch",
    )(pos, hp, xs0)


COMBINE_TB = 512


def _combine_body(pos_ref, x_ref, rw_ref, ys_ref, o_ref, buf, sem):
    base = pl.program_id(0) * COMBINE_TB

    def issue(g, carry):
        for u in range(SUBLANES):
            tok = base + g * SUBLANES + u
            for k in range(2):
                pltpu.make_async_copy(ys_ref.at[pl.ds(pos_ref[2 * tok + k], 1)],
                                      buf.at[k, g, pl.ds(u, 1)], sem).start(priority=k)
        return carry

    lax.fori_loop(0, COMBINE_TB // SUBLANES, issue, 0)
    for _ in range(2):
        _wait_rows(ys_ref, COMBINE_TB, sem)
    rw = rw_ref[...]
    rows = lambda k: _unpack_bf16_pairs(buf[k].reshape(COMBINE_TB, buf.shape[-1]))
    o_ref[...] = x_ref[...] + rw[:, 0:1] * rows(0) + rw[:, 1:2] * rows(1)


def _moe_combine(pos, x1, rw, ys):
    n = x1.shape[0]
    c = ys.shape[1]
    return pl.pallas_call(
        _combine_body,
        out_shape=jax.ShapeDtypeStruct((n, D_MODEL), F32),
        grid_spec=pltpu.PrefetchScalarGridSpec(
            num_scalar_prefetch=1,
            grid=(n // COMBINE_TB,),
            in_specs=[pl.BlockSpec((COMBINE_TB, D_MODEL), lambda i, p: (i, 0)),
                      pl.BlockSpec((COMBINE_TB, LANES), lambda i, p: (i, 0)),
                      pl.BlockSpec(memory_space=pl.ANY)],
            out_specs=pl.BlockSpec((COMBINE_TB, D_MODEL), lambda i, p: (i, 0)),
            scratch_shapes=[pltpu.VMEM((2, COMBINE_TB // SUBLANES, SUBLANES, c), U32),
                            pltpu.SemaphoreType.DMA(())]),
        compiler_params=_cparams(("arbitrary",)),
        name="moe_combine",
    )(pos, x1, rw, ys)


MOE_TF = 1792


def _experts_body(te_ref, nv_ref, xs_ref, wg_ref, wu_ref, wd_ref, ys_ref, xb_ref, acc_ref):
    i = pl.program_id(0)
    f = pl.program_id(1)
    nf = pl.num_programs(1)

    @pl.when(i < nv_ref[0])
    def _():
        @pl.when(f == 0)
        def _():
            xb_ref[...] = _unpack_bf16_pairs(xs_ref[...]).astype(BF16)

        h = xb_ref[...]
        part = None
        for c, w in _col_chunks(MOE_TF, 1024):
            a = _dot(h, wg_ref[0, :, c:c + w])
            act = (a * _sigmoid(a) * _dot(h, wu_ref[0, :, c:c + w])).astype(BF16)
            pc = _dot(act, wd_ref[0, c:c + w, :])
            part = pc if part is None else part + pc

        @pl.when(f == 0)
        def _():
            acc_ref[...] = part

        @pl.when(f > 0)
        def _():
            acc_ref[...] += part

        @pl.when(f == nf - 1)
        def _():
            ys_ref[...] = _pack_bf16_pairs(acc_ref[...])

    @pl.when((i >= nv_ref[0]) & (f == nf - 1))
    def _():
        ys_ref[...] = jnp.zeros_like(ys_ref)


def _moe_experts(tile_expert, n_valid, xs, wg, wu, wd):
    rows, c = xs.shape
    nt = rows // MOE_TM
    nf = D_FF_EXPERT // MOE_TF

    def tile(i, nv):
        return jnp.minimum(i, nv[0] - 1)

    def fblk(i, f, nv):
        return jnp.where(i < nv[0], f, nf - 1)

    return pl.pallas_call(
        _experts_body,
        out_shape=jax.ShapeDtypeStruct((rows, c), U32),
        grid_spec=pltpu.PrefetchScalarGridSpec(
            num_scalar_prefetch=2,
            grid=(nt, nf),
            in_specs=[pl.BlockSpec((MOE_TM, c), lambda i, f, te, nv: (tile(i, nv), 0)),
                      pl.BlockSpec((1, D_MODEL, MOE_TF),
                                   lambda i, f, te, nv: (te[tile(i, nv)], 0, fblk(i, f, nv))),
                      pl.BlockSpec((1, D_MODEL, MOE_TF),
                                   lambda i, f, te, nv: (te[tile(i, nv)], 0, fblk(i, f, nv))),
                      pl.BlockSpec((1, MOE_TF, D_MODEL),
                                   lambda i, f, te, nv: (te[tile(i, nv)], fblk(i, f, nv), 0))],
            out_specs=pl.BlockSpec((MOE_TM, c), lambda i, f, te, nv: (i, 0)),
            scratch_shapes=[pltpu.VMEM((MOE_TM, D_MODEL), BF16), pltpu.VMEM((MOE_TM, D_MODEL), F32)]),
        compiler_params=_cparams(("arbitrary", "arbitrary")),
        name="moe_experts",
    )(tile_expert, n_valid, xs, wg, wu, wd)


def _moe_ffn(x1, hp, ri, rw, wg, wu, wd):
    rank, cnt = _moe_plan(ri)
    c0 = cnt[0, :N_EXPERTS].astype(I32)
    c1 = cnt[0, N_EXPERTS:2 * N_EXPERTS].astype(I32)
    padded = ((c0 + c1 + MOE_TM - 1) // MOE_TM) * MOE_TM
    ends = jnp.cumsum(padded)
    off = ends - padded
    e1, e2 = ri[:, 0], ri[:, 1]
    pos0 = off[e1] + rank[:, 0].astype(I32)
    pos1 = off[e2] + c0[e2] + rank[:, 1].astype(I32)
    pos = jnp.stack([pos0, pos1], axis=1).reshape(-1)
    nt = _moe_rows(x1.shape[0]) // MOE_TM
    tile_start = jnp.arange(nt, dtype=I32) * MOE_TM
    tile_expert = jnp.minimum(jnp.sum(tile_start[:, None] >= ends[None, :], axis=1),
                              N_EXPERTS - 1).astype(I32)
    n_valid = (ends[-1:] // MOE_TM).astype(I32)
    xs = _moe_dispatch(pos, hp)
    ys = _moe_experts(tile_expert, n_valid, xs, wg, wu, wd)
    return _moe_combine(pos, x1, rw, ys)


def _head_consts():
    lane = jnp.arange(ATT_OUT)
    bd = jnp.where((lane[:, None] // ATT_HEAD_DIM) == (lane[None, :] // ATT_HEAD_DIM),
                   1.0 / ATT_HEAD_DIM, 0.0).astype(BF16)
    perm = (lane[:, None] == (lane[None, :] ^ (ATT_HEAD_DIM // 2))).astype(BF16)
    return bd, perm


def _head_gains(g):
    full = jnp.tile(g.astype(F32), ATT_SLOTS).reshape(1, ATT_OUT)
    half = ATT_HEAD_DIM // 2
    swapped = jnp.tile(jnp.concatenate([g[half:], g[:half]]).astype(F32), ATT_SLOTS).reshape(1, ATT_OUT)
    return full, swapped


def kernel(x, mem, positions, norm_mix, w_in, ssm_a_re, ssm_a_im, ssm_log_dt, ssm_b_re, ssm_b_im,
           ssm_c_re, ssm_c_im, ssm_d, ssm_w_glu, w_ssm_out, att_q_norm, att_k_norm, w_att_out,
           norm_mem, w_mem_kv, mem_q_norm, mem_k_norm, w_mem_out, w_o, norm_ffn, ffn_w_gate,
           ffn_w_up, ffn_w_down, moe_w_router, moe_w_gate, moe_w_up, moe_w_down):
    bsz, seq, d = x.shape
    ntok = bsz * seq
    depth = w_in.shape[0]
    cos_t, sin_t = _rope_tables(positions)
    bd, perm = _head_consts()
    mem2d = mem.reshape(bsz * MEM_LEN, d)
    x2d = x.reshape(ntok, d)
    flat = lambda a: a.reshape((-1,) + a.shape[2:])
    ssm_ops = _ssm_operators(flat(ssm_a_re), flat(ssm_a_im), flat(ssm_log_dt), flat(ssm_b_re),
                             flat(ssm_b_im), flat(ssm_c_re), flat(ssm_c_im), ssm_d)
    for i in range(depth):
        wi = w_in[i]
        g_mix = norm_mix[i].reshape(1, d)

        wut = jnp.transpose(wi[:, :SSM_WIDTH]).astype(BF16)
        ut, qkv, memq, gates = _inproj_main(x2d, g_mix, wut, wi.astype(BF16))

        yt = _ssm_scan(ut, ssm_ops, i, bsz, seq)

        gq, gqs = _head_gains(att_q_norm[i])
        gk, gks = _head_gains(att_k_norm[i])
        att = []
        for gi, (_, dil) in enumerate(DIL_PAIRS):
            att.extend(_dilated_attention_group(qkv, gi, cos_t, sin_t, gq, gqs, gk, gks, bd, perm,
                                                bsz, seq, dil))

        kmem, vmem = _memory_kv(mem2d, norm_mem[i].reshape(1, d), w_mem_kv[i].astype(BF16),
                                mem_k_norm[i].reshape(1, X_HEAD_DIM))

        wts = (jnp.transpose(ssm_w_glu[i]).astype(BF16), w_ssm_out[i].astype(BF16),
               w_att_out[i].astype(BF16), w_mem_out[i].astype(BF16), w_o[i].astype(BF16),
               mem_q_norm[i].reshape(1, X_HEAD_DIM), norm_ffn[i].reshape(1, d))
        j = i // 2
        if i % 2 == 0:
            x1, h2 = _merge(x2d, memq, gates, yt, att, kmem, vmem, wts, bsz, seq)
            x2d = _dense_ffn(h2, x1, ffn_w_gate[j].astype(BF16), ffn_w_up[j].astype(BF16),
                             ffn_w_down[j].astype(BF16))
        else:
            wr = jnp.zeros((d, LANES), F32).at[:, :N_EXPERTS].set(moe_w_router[j])
            wr_hi = wr.astype(BF16)
            wr_lo = (wr - wr_hi.astype(F32)).astype(BF16)
            x1, hp, ri, rw = _merge(x2d, memq, gates, yt, att, kmem, vmem, wts, bsz, seq,
                                    router=(wr_hi, wr_lo))
            x2d = _moe_ffn(x1, hp, ri, rw, moe_w_gate[j].astype(BF16), moe_w_up[j].astype(BF16),
                           moe_w_down[j].astype(BF16))
    return x2d.reshape(bsz, seq, d)
```

```python
import functools
import math

import jax
import jax.numpy as jnp
from jax import lax
from jax.experimental import pallas as pl
from jax.experimental.pallas import tpu as pltpu

F32 = jnp.float32
BF16 = jnp.bfloat16
I32 = jnp.int32
U32 = jnp.uint32

EPS = 1e-6
D_MODEL = 1024
MEM_LEN = 256
SSM_WIDTH = 512
SSM_GROUP = 16
SSM_GROUPS = 32
SSM_STATE = 64
ATT_HEAD_DIM = 64
ATT_SLOTS = 4
DIL_PAIRS = ((128, 1), (512, 4), (2048, 16))
ATT_WIDTH = 768
ATT_OUT = 256
BLOCK = 128
ROPE_THETA = 10000.0
X_HEADS = 4
X_HEAD_DIM = 128
X_WIDTH = 512
D_FF = 2816
N_EXPERTS = 8
D_FF_EXPERT = 3584

LANES = 128
SSM_CHUNK = 128
QKV_W = 3 * ATT_OUT
VMEM_LIMIT = 56 * 1024 * 1024


def _cparams(sem, vmem=VMEM_LIMIT):
    return pltpu.CompilerParams(dimension_semantics=sem, vmem_limit_bytes=vmem)


def _rms(x, g):
    ms = jnp.mean(x * x, axis=-1, keepdims=True)
    return x * lax.rsqrt(ms + EPS) * g


def _sigmoid(x):
    return 0.5 * jnp.tanh(0.5 * x) + 0.5


def _dot(a, b):
    return jnp.dot(a, b, preferred_element_type=F32)


def _dot_nt(a, b):
    return lax.dot_general(a, b, (((1,), (1,)), ((), ())), preferred_element_type=F32)


def _rope_body(pos_ref, inv_ref, sgn_ref, cos_ref, sin_ref):
    ang = pos_ref[...].astype(F32) * inv_ref[...]
    cos_ref[...] = jnp.cos(ang)
    sin_ref[...] = jnp.sin(ang) * sgn_ref[...]


def _rope_tables(positions):
    n = positions.size
    half = ATT_HEAD_DIM // 2
    inv = ROPE_THETA ** (-jnp.arange(half, dtype=F32) / half)
    inv_row = jnp.tile(inv, LANES // half).reshape(1, LANES)
    lane = jnp.arange(LANES)
    sgn_row = jnp.where((lane % ATT_HEAD_DIM) < half, -1.0, 1.0).astype(F32).reshape(1, LANES)
    tm = 2048
    return pl.pallas_call(
        _rope_body,
        out_shape=(jax.ShapeDtypeStruct((n, LANES), F32), jax.ShapeDtypeStruct((n, LANES), F32)),
        grid=(n // tm,),
        in_specs=[pl.BlockSpec((tm, 1), lambda i: (i, 0)),
                  pl.BlockSpec((1, LANES), lambda i: (0, 0)),
                  pl.BlockSpec((1, LANES), lambda i: (0, 0))],
        out_specs=(pl.BlockSpec((tm, LANES), lambda i: (i, 0)),
                   pl.BlockSpec((tm, LANES), lambda i: (i, 0))),
        compiler_params=_cparams(("parallel",)),
        name="rope_tables",
    )(positions.reshape(n, 1), inv_row, sgn_row)


def _col_chunks(width, step=512):
    return [(c, min(step, width - c)) for c in range(0, width, step)]


def _inproj_main_body(x_ref, g_ref, wut_ref, w_ref, ut_ref, qkv_ref, memq_ref, gate_ref):
    h = _rms(x_ref[...], g_ref[...]).astype(BF16)
    ut_ref[...] = _dot_nt(wut_ref[...], h).astype(BF16)
    for gi in range(len(DIL_PAIRS)):
        for j in range(3):
            src = SSM_WIDTH + j * ATT_WIDTH + gi * ATT_OUT
            dst = gi * QKV_W + j * ATT_OUT
            qkv_ref[:, dst:dst + ATT_OUT] = _dot(h, w_ref[:, src:src + ATT_OUT]).astype(BF16)
    col = SSM_WIDTH + 3 * ATT_WIDTH
    for ref in (memq_ref, gate_ref):
        for c, w in _col_chunks(ref.shape[1]):
            ref[:, c:c + w] = _dot(h, w_ref[:, col + c:col + c + w]).astype(BF16)
        col += ref.shape[1]


def _inproj_main(x2d, g, wut, wmain):
    n = x2d.shape[0]
    tm = 512
    row = lambda i: (i, 0)
    const = lambda i: (0, 0)
    widths = (3 * QKV_W, X_WIDTH, 3 * D_MODEL)
    return pl.pallas_call(
        _inproj_main_body,
        out_shape=(jax.ShapeDtypeStruct((SSM_WIDTH, n), BF16),)
        + tuple(jax.ShapeDtypeStruct((n, w), BF16) for w in widths),
        grid=(n // tm,),
        in_specs=[pl.BlockSpec((tm, D_MODEL), row),
                  pl.BlockSpec((1, D_MODEL), const),
                  pl.BlockSpec((SSM_WIDTH, D_MODEL), const, pipeline_mode=pl.Buffered(1)),
                  pl.BlockSpec(wmain.shape, const, pipeline_mode=pl.Buffered(1))],
        out_specs=(pl.BlockSpec((SSM_WIDTH, tm), lambda i: (0, i)),)
        + tuple(pl.BlockSpec((tm, w), row) for w in widths),
        compiler_params=_cparams(("parallel",)),
        name="inproj_main",
    )(x2d, g, wut, wmain)


def _qk_prep(xf, cos2, sin2, g, gs, bd, perm, scale):
    xb = xf.astype(BF16)
    ms = _dot((xf * xf).astype(BF16), bd)
    xs = _dot(xb, perm)
    y = lax.rsqrt(ms + EPS) * scale * (xf * (g * cos2) + xs * (gs * sin2))
    return y.astype(BF16)


def _attn_body(qkv_ref, cos_ref, sin_ref, gq_ref, gqs_ref, gk_ref, gks_ref, bd_ref, perm_ref,
               o_ref, lse_ref, sbuf, qbuf, kbuf, vbuf, *, dil, tq):
    j = pl.program_id(1)

    @pl.when(j == 0)
    def _():
        kbuf[:, 0:BLOCK, :] = jnp.zeros((dil, BLOCK, ATT_OUT), BF16)
        vbuf[:, 0:BLOCK, :] = jnp.zeros((dil, BLOCK, ATT_OUT), BF16)

    @pl.when(j > 0)
    def _():
        kbuf[:, 0:BLOCK, :] = kbuf[:, tq:tq + BLOCK, :]
        vbuf[:, 0:BLOCK, :] = vbuf[:, tq:tq + BLOCK, :]

    bd = bd_ref[...]
    perm = perm_ref[...]
    lane = lax.broadcasted_iota(I32, (BLOCK, LANES), 1)
    low = lane < ATT_HEAD_DIM
    qi = lax.broadcasted_iota(I32, (BLOCK, 2 * BLOCK), 0) + BLOCK
    ki = lax.broadcasted_iota(I32, (BLOCK, 2 * BLOCK), 1)
    off = qi - ki
    band = (off >= 0) & (off <= BLOCK)
    band_first = band & ((ki >= BLOCK) | (j > 0))
    for c in range(QKV_W // LANES):
        sbuf[c] = qkv_ref[:, c * LANES:(c + 1) * LANES].astype(F32)

    def rows(start, size):
        return pl.ds(start, size, stride=dil) if dil > 1 else pl.ds(start, size)

    def planes(first, sel):
        return jnp.concatenate([sbuf[first, sel, :], sbuf[first + 1, sel, :]], axis=1)

    def prep(r, carry):
        cos = cos_ref[rows(r, tq), :]
        sin = sin_ref[rows(r, tq), :]
        cos2 = jnp.concatenate([cos, cos], axis=1)
        sin2 = jnp.concatenate([sin, sin], axis=1)
        cur = rows(r, tq)
        qbuf[r] = _qk_prep(planes(0, cur), cos2, sin2, gq_ref[...], gqs_ref[...],
                           bd, perm, ATT_HEAD_DIM ** -0.5)
        kbuf[r, BLOCK:, :] = _qk_prep(planes(2, cur), cos2, sin2,
                                      gk_ref[...], gks_ref[...], bd, perm, 1.0)
        vbuf[r, BLOCK:, :] = planes(4, cur).astype(BF16)
        return carry

    if dil == 1:
        prep(0, 0)
    else:
        lax.fori_loop(0, dil, prep, 0, unroll=2)

    for r in range(dil):
        for s in range(tq // BLOCK):
            row0 = s * BLOCK
            valid = band_first if s == 0 else band
            dst = rows(r + row0 * dil, BLOCK)
            for p in range(ATT_OUT // LANES):
                cols = slice(p * LANES, (p + 1) * LANES)
                qp = qbuf[r, row0:row0 + BLOCK, cols]
                kp = kbuf[r, row0:row0 + 2 * BLOCK, cols]
                vp = vbuf[r, row0:row0 + 2 * BLOCK, cols]
                outs, lses = [], []
                for h in range(2):
                    qm = jnp.where(low if h == 0 else ~low, qp, jnp.zeros_like(qp))
                    sc = jnp.where(valid, _dot_nt(qm, kp), -1e30)
                    m = jnp.max(sc, axis=-1, keepdims=True)
                    pr = jnp.exp(sc - m)
                    den = jnp.sum(pr, axis=-1, keepdims=True)
                    outs.append(_dot(pr.astype(BF16), vp) / den)
                    lses.append(m + jnp.log(den))
                o_ref[p, dst, :] = jnp.where(low, outs[0], outs[1])
                lse_ref[p, dst, :] = jnp.where(low, lses[0], lses[1])


def _dilated_attention_group(qkv_all, gi, cos_t, sin_t, gq, gqs, gk, gks, bd, perm, bsz, seq, dil):
    ntok = bsz * seq
    tt = max(1024, BLOCK * dil)
    tq = tt // dil
    nblk = seq // tt
    const = lambda b, j: (0, 0)
    return pl.pallas_call(
        functools.partial(_attn_body, dil=dil, tq=tq),
        out_shape=(jax.ShapeDtypeStruct((ATT_OUT // LANES, ntok, LANES), F32),
                   jax.ShapeDtypeStruct((ATT_OUT // LANES, ntok, LANES), F32)),
        grid=(bsz, nblk),
        in_specs=[pl.BlockSpec((tt, QKV_W), lambda b, j: (b * nblk + j, gi)),
                  pl.BlockSpec((tt, LANES), lambda b, j: (b * nblk + j, 0)),
                  pl.BlockSpec((tt, LANES), lambda b, j: (b * nblk + j, 0)),
                  pl.BlockSpec((1, ATT_OUT), const), pl.BlockSpec((1, ATT_OUT), const),
                  pl.BlockSpec((1, ATT_OUT), const), pl.BlockSpec((1, ATT_OUT), const),
                  pl.BlockSpec((ATT_OUT, ATT_OUT), const), pl.BlockSpec((ATT_OUT, ATT_OUT), const)],
        out_specs=(pl.BlockSpec((ATT_OUT // LANES, tt, LANES), lambda b, j: (0, b * nblk + j, 0)),
                   pl.BlockSpec((ATT_OUT // LANES, tt, LANES), lambda b, j: (0, b * nblk + j, 0))),
        scratch_shapes=[pltpu.VMEM((QKV_W // LANES, tt, LANES), F32),
                        pltpu.VMEM((dil, tq, ATT_OUT), BF16),
                        pltpu.VMEM((dil, BLOCK + tq, ATT_OUT), BF16),
                        pltpu.VMEM((dil, BLOCK + tq, ATT_OUT), BF16)],
        compiler_params=_cparams(("arbitrary", "arbitrary")),
        name=f"dilated_attn{dil}",
    )(qkv_all, cos_t, sin_t, gq, gqs, gk, gks, bd, perm)


def _ssm_body(u_ref, ktab_ref, w_ref, v_ref, lam_ref, dvec_ref, y_ref, m_ref, m2_ref, sloc_ref,
              ssw_ref, sin_ref, yin_ref, *, bsz, cpb):
    tc = SSM_CHUNK
    row = lax.broadcasted_iota(I32, (tc, tc), 0)
    col = lax.broadcasted_iota(I32, (tc, tc), 1)
    causal = col >= row

    u = jnp.concatenate([u_ref[c] for c in range(SSM_GROUP)], axis=1)
    sloc = _dot(u, w_ref[0])
    sloc_ref[...] = sloc
    ssw_ref[...] = pltpu.roll(sloc, SSM_STATE, 1)

    a1 = lam_ref[0, 0:1, :]
    a2 = lam_ref[0, 1:2, :]
    s = jnp.zeros((bsz, 2 * SSM_STATE), F32)
    sw = s
    sin_ref[pl.ds(0, bsz, stride=cpb), :] = s
    for k in range(1, cpb):
        prev = pl.ds(k - 1, bsz, stride=cpb)
        s, sw = (a1 * s + a2 * sw + sloc_ref[prev, :], a1 * sw - a2 * s + ssw_ref[prev, :])
        sin_ref[pl.ds(k, bsz, stride=cpb), :] = s

    s_in = sin_ref[...].astype(BF16)
    nblk = SSM_GROUP // 2

    def y_in(c):
        return _dot(s_in, v_ref[0, c]) + dvec_ref[0, :, c * tc:(c + 1) * tc] * u_ref[c].astype(F32)

    for cb in range(nblk):
        yin_ref[cb] = jnp.concatenate([y_in(2 * cb), y_in(2 * cb + 1)], axis=1)

    def build(cb, dst):
        for cp in range(SSM_GROUP):
            for h in range(2):
                kv = ktab_ref[0, pl.ds(cp * SSM_GROUP + 2 * cb + h, 1), :]
                tile = pltpu.roll(jnp.broadcast_to(kv, (tc, tc)), 0, 1, stride=1, stride_axis=0)
                dst[cp * tc:(cp + 1) * tc, h * tc:(h + 1) * tc] = (
                    jnp.where(causal, tile, 0.0).astype(BF16))

    def multiply(cb, src):
        y = jax.nn.gelu(_dot(u, src[...]) + yin_ref[cb], approximate=True)
        y_ref[2 * cb] = y[:, :tc].astype(BF16)
        y_ref[2 * cb + 1] = y[:, tc:].astype(BF16)

    build(0, m_ref)

    def pair(i, carry):
        build(2 * i + 1, m2_ref)
        multiply(2 * i, m_ref)
        build(jnp.minimum(2 * i + 2, nblk - 1), m_ref)
        multiply(2 * i + 1, m2_ref)
        return carry

    lax.fori_loop(0, nblk // 2, pair, 0)


def _ssm_operators(a_re, a_im, log_dt, b_re, b_im, c_re, c_im, d_skip):
    tc = SSM_CHUNK
    ng = a_re.shape[0]
    lam = lax.complex(a_re.astype(F32), a_im.astype(F32))
    dt = jnp.exp(log_dt.astype(F32))[:, None]
    lam_dt = lam * dt
    lam_bar = jnp.exp(lam_dt)
    b = lax.complex(b_re.astype(F32), b_im.astype(F32))
    b_bar = ((lam_bar - 1.0) / lam)[..., None] * b
    c = lax.complex(c_re.astype(F32), c_im.astype(F32))
    k = jnp.arange(tc + 1, dtype=F32)
    pw = jnp.exp(lam_dt[:, None, :] * k[None, :, None])
    ktab = jnp.einsum('gcp,gkp,gpd->gdck', c, pw[:, :tc], b_bar).real
    ktab = ktab.reshape(ng, SSM_GROUP * SSM_GROUP, tc).astype(F32)
    wc = jnp.einsum('gjp,gpd->gdjp', pw[:, tc - 1::-1][:, :tc], b_bar)
    wc = wc.reshape(ng, SSM_GROUP * tc, SSM_STATE)
    w = jnp.concatenate([wc.real, wc.imag], axis=-1).astype(BF16)
    vc = jnp.einsum('gcp,gtp->gcpt', c, pw[:, 1:tc + 1])
    v = jnp.concatenate([vc.real, -vc.imag], axis=2).astype(BF16)
    lt = pw[:, tc]
    lam_rows = jnp.stack([jnp.concatenate([lt.real, lt.real], -1),
                          jnp.concatenate([-lt.imag, lt.imag], -1)], axis=1).astype(F32)
    dvec = jnp.repeat(d_skip.astype(F32).reshape(ng, SSM_GROUP), tc, axis=1)
    return ktab, w, v, lam_rows, dvec.reshape(ng, 1, SSM_GROUP * tc)


def _ssm_scan(ut, ops, layer, bsz, seq):
    ktab, w, v, lam_rows, dvec = ops
    ntok = bsz * seq
    tc = SSM_CHUNK
    nch = ntok // tc
    u3 = ut.reshape(SSM_WIDTH, nch, tc)
    gmap = lambda g: (g, 0, 0)
    tmap = lambda g: (layer * SSM_GROUPS + g, 0, 0)
    y3 = pl.pallas_call(
        functools.partial(_ssm_body, bsz=bsz, cpb=seq // tc),
        out_shape=jax.ShapeDtypeStruct((SSM_WIDTH, nch, tc), BF16),
        grid=(SSM_GROUPS,),
        in_specs=[pl.BlockSpec((SSM_GROUP, nch, tc), gmap),
                  pl.BlockSpec((1, SSM_GROUP * SSM_GROUP, tc), tmap),
                  pl.BlockSpec((1, SSM_GROUP * tc, 2 * SSM_STATE), tmap),
                  pl.BlockSpec((1, SSM_GROUP, 2 * SSM_STATE, tc),
                               lambda g: (layer * SSM_GROUPS + g, 0, 0, 0)),
                  pl.BlockSpec((1, 2, 2 * SSM_STATE), tmap),
                  pl.BlockSpec((1, 1, SSM_GROUP * tc), tmap)],
        out_specs=pl.BlockSpec((SSM_GROUP, nch, tc), gmap),
        scratch_shapes=[pltpu.VMEM((SSM_GROUP * tc, 2 * tc), BF16),
                        pltpu.VMEM((SSM_GROUP * tc, 2 * tc), BF16),
                        pltpu.VMEM((nch, 2 * SSM_STATE), F32),
                        pltpu.VMEM((nch, 2 * SSM_STATE), F32),
                        pltpu.VMEM((nch, 2 * SSM_STATE), F32),
                        pltpu.VMEM((SSM_GROUP // 2, nch, 2 * tc), F32)],
        compiler_params=_cparams(("parallel",)),
        name="ssm_scan",
    )(u3, ktab, w, v, lam_rows, dvec)
    return y3.reshape(SSM_WIDTH, ntok)


def _memkv_body(mem_ref, g_ref, w_ref, gk_ref, k_ref, v_ref):
    h = _rms(mem_ref[...], g_ref[...]).astype(BF16)
    kv = _dot(h, w_ref[...])
    for hd in range(X_HEADS):
        cols = slice(hd * X_HEAD_DIM, (hd + 1) * X_HEAD_DIM)
        k_ref[:, cols] = _rms(kv[:, cols], gk_ref[...]).astype(BF16)
    v_ref[...] = kv[:, X_WIDTH:].astype(BF16)


def _memory_kv(mem2d, g, w_kv, gk):
    m = mem2d.shape[0]
    tm = MEM_LEN
    return pl.pallas_call(
        _memkv_body,
        out_shape=(jax.ShapeDtypeStruct((m, X_WIDTH), BF16), jax.ShapeDtypeStruct((m, X_WIDTH), BF16)),
        grid=(m // tm,),
        in_specs=[pl.BlockSpec((tm, D_MODEL), lambda i: (i, 0)),
                  pl.BlockSpec((1, D_MODEL), lambda i: (0, 0)),
                  pl.BlockSpec((D_MODEL, 2 * X_WIDTH), lambda i: (0, 0)),
                  pl.BlockSpec((1, X_HEAD_DIM), lambda i: (0, 0))],
        out_specs=(pl.BlockSpec((tm, X_WIDTH), lambda i: (i, 0)),
                   pl.BlockSpec((tm, X_WIDTH), lambda i: (i, 0))),
        compiler_params=_cparams(("parallel",)),
        name="memory_kv",
    )(mem2d, g, w_kv, gk)


def _pack_bf16_pairs(x):
    c = x.shape[1] // 2
    bits = pltpu.bitcast(x.astype(BF16).astype(F32), U32)
    return (bits[:, :c] & jnp.uint32(0xFFFF0000)) | (bits[:, c:] >> 16)


def _unpack_bf16_pairs(p):
    hi = pltpu.bitcast(p & jnp.uint32(0xFFFF0000), F32)
    lo = pltpu.bitcast(p << 16, F32)
    return jnp.concatenate([hi, lo], axis=1)


def _merge_body(*refs, moe):
    (x_ref, memq_ref, gate_ref, yt_ref, o0_ref, l0_ref, o1_ref, l1_ref, o2_ref, l2_ref,
     km_ref, vm_ref, wglut_ref, wso_ref, wao_ref, wmo_ref, wo_ref, gmq_ref, gffn_ref) = refs[:19]
    if moe:
        wrh_ref, wrl_ref, x1_ref, hp_ref, ri_ref, rw_ref = refs[19:]
    else:
        x1_ref, h2_ref = refs[19:]

    ga = _dot(wglut_ref[...], yt_ref[...])
    glu = ga[:SSM_WIDTH] * _sigmoid(ga[SSM_WIDTH:])
    y_ssm = _dot(jnp.transpose(glu).astype(BF16), wso_ref[...])

    planes = []
    for p in range(ATT_OUT // LANES):
        l0, l1, l2 = l0_ref[p], l1_ref[p], l2_ref[p]
        mx = jnp.maximum(jnp.maximum(l0, l1), l2)
        e0, e1, e2 = jnp.exp(l0 - mx), jnp.exp(l1 - mx), jnp.exp(l2 - mx)
        planes.append((e0 * o0_ref[p] + e1 * o1_ref[p] + e2 * o2_ref[p]) / (e0 + e1 + e2))
    y_att = _dot(jnp.concatenate(planes, axis=1).astype(BF16), wao_ref[...])

    heads = []
    for hd in range(X_HEADS):
        cols = slice(hd * X_HEAD_DIM, (hd + 1) * X_HEAD_DIM)
        q = (_rms(memq_ref[:, cols].astype(F32), gmq_ref[...]) * (X_HEAD_DIM ** -0.5)).astype(BF16)
        sc = _dot_nt(q, km_ref[:, cols])
        m = jnp.max(sc, axis=-1, keepdims=True)
        pr = jnp.exp(sc - m)
        den = jnp.sum(pr, axis=-1, keepdims=True)
        heads.append(_dot(pr.astype(BF16), vm_ref[:, cols]) / den)
    y_mem = _dot(jnp.concatenate(heads, axis=1).astype(BF16), wmo_ref[...])

    d = D_MODEL
    merged = (_sigmoid(gate_ref[:, 0:d].astype(F32)) * y_ssm
              + _sigmoid(gate_ref[:, d:2 * d].astype(F32)) * y_att
              + _sigmoid(gate_ref[:, 2 * d:3 * d].astype(F32)) * y_mem)
    x1 = x_ref[...] + _dot(merged.astype(BF16), wo_ref[...])
    x1_ref[...] = x1
    h2 = _rms(x1, gffn_ref[...])
    if not moe:
        h2_ref[...] = h2.astype(BF16)
        return

    hp_ref[...] = _pack_bf16_pairs(h2)
    hi = h2.astype(BF16)
    lo = (h2 - hi.astype(F32)).astype(BF16)
    logits = _dot(hi, wrh_ref[...]) + _dot(hi, wrl_ref[...]) + _dot(lo, wrh_ref[...])
    lane = lax.broadcasted_iota(I32, logits.shape, 1)
    lg = jnp.where(lane < N_EXPERTS, logits, -jnp.inf)
    v1 = jnp.max(lg, axis=-1, keepdims=True)
    i1 = jnp.min(jnp.where(lg == v1, lane, LANES), axis=-1, keepdims=True)
    lg2 = jnp.where(lane == i1, -jnp.inf, lg)
    v2 = jnp.max(lg2, axis=-1, keepdims=True)
    i2 = jnp.min(jnp.where(lg2 == v2, lane, LANES), axis=-1, keepdims=True)
    e = jnp.exp(v2 - v1)
    ri_ref[...] = jnp.where(lane == 0, i1, jnp.where(lane == 1, i2, 0))
    rw_ref[...] = jnp.where(lane == 0, 1.0 / (1.0 + e), jnp.where(lane == 1, e / (1.0 + e), 0.0))


def _merge(x2d, memq, gates, yt, att, kmem, vmem, wts, bsz, seq, router=None):
    n = x2d.shape[0]
    tm = 512
    tpb = seq // tm
    moe = router is not None
    row = lambda i: (i, 0)
    const = lambda i: (0, 0)
    in_specs = [pl.BlockSpec((tm, D_MODEL), row),
                pl.BlockSpec((tm, X_WIDTH), row),
                pl.BlockSpec((tm, 3 * D_MODEL), row),
                pl.BlockSpec((SSM_WIDTH, tm), lambda i: (0, i))]
    in_specs += [pl.BlockSpec((ATT_OUT // LANES, tm, LANES), lambda i: (0, i, 0))] * 6
    in_specs += [pl.BlockSpec((MEM_LEN, X_WIDTH), lambda i: (i // tpb, 0))] * 2
    wglut, wso, wao, wmo, wo, gmq, gffn = wts
    in_specs += [pl.BlockSpec(w.shape, const) for w in (wglut, wso, wao, wmo, wo, gmq, gffn)]
    args = [x2d, memq, gates, yt, *att, kmem, vmem, wglut, wso, wao, wmo, wo, gmq, gffn]
    if moe:
        in_specs += [pl.BlockSpec(router[0].shape, const)] * 2
        args += list(router)
        out_shape = (jax.ShapeDtypeStruct((n, D_MODEL), F32),
                     jax.ShapeDtypeStruct((n, D_MODEL // 2), U32),
                     jax.ShapeDtypeStruct((n, LANES), I32),
                     jax.ShapeDtypeStruct((n, LANES), F32))
        out_specs = (pl.BlockSpec((tm, D_MODEL), row), pl.BlockSpec((tm, D_MODEL // 2), row),
                     pl.BlockSpec((tm, LANES), row), pl.BlockSpec((tm, LANES), row))
    else:
        out_shape = (jax.ShapeDtypeStruct((n, D_MODEL), F32), jax.ShapeDtypeStruct((n, D_MODEL), BF16))
        out_specs = (pl.BlockSpec((tm, D_MODEL), row), pl.BlockSpec((tm, D_MODEL), row))
    return pl.pallas_call(
        functools.partial(_merge_body, moe=moe),
        out_shape=out_shape,
        grid=(n // tm,),
        in_specs=in_specs,
        out_specs=out_specs,
        compiler_params=_cparams(("parallel",)),
        name="merge_moe" if moe else "merge_dense",
    )(*args)


FFN_CHUNK = 768


def _ffn_body(h_ref, x_ref, wg_ref, wu_ref, wd_ref, o_ref):
    h = h_ref[...]
    acc = x_ref[...]
    for c, w in _col_chunks(D_FF, FFN_CHUNK):
        a = _dot(h, wg_ref[:, c:c + w])
        act = (a * _sigmoid(a) * _dot(h, wu_ref[:, c:c + w])).astype(BF16)
        acc = acc + _dot(act, wd_ref[c:c + w, :])
    o_ref[...] = acc


def _dense_ffn(h2, x1, wg, wu, wd):
    n = h2.shape[0]
    tm = 512
    row = lambda i: (i, 0)
    resident = lambda shape: pl.BlockSpec(shape, lambda i: (0, 0), pipeline_mode=pl.Buffered(1))
    return pl.pallas_call(
        _ffn_body,
        out_shape=jax.ShapeDtypeStruct((n, D_MODEL), F32),
        grid=(n // tm,),
        in_specs=[pl.BlockSpec((tm, D_MODEL), row),
                  pl.BlockSpec((tm, D_MODEL), row),
                  resident((D_MODEL, D_FF)), resident((D_MODEL, D_FF)), resident((D_FF, D_MODEL))],
        out_specs=pl.BlockSpec((tm, D_MODEL), row),
        compiler_params=_cparams(("parallel",)),
        name="dense_ffn",
    )(h2, x1, wg, wu, wd)


MOE_TM = 512
PLAN_TB = 512


def _moe_rows(ntok):
    return 2 * ntok + N_EXPERTS * MOE_TM


def _plan_body(ri_ref, rank_ref, cnt_ref, carry_ref):
    i = pl.program_id(0)

    @pl.when(i == 0)
    def _():
        carry_ref[...] = jnp.zeros_like(carry_ref)

    ri = ri_ref[...]
    lane = lax.broadcasted_iota(I32, ri.shape, 1)
    e1 = ri[:, 0:1]
    e2 = ri[:, 1:2]
    oh = (jnp.where(lane < N_EXPERTS, e1, e2 + N_EXPERTS) == lane) & (lane < 2 * N_EXPERTS)
    ohf = jnp.where(oh, 1.0, 0.0)
    tr = lax.broadcasted_iota(I32, (PLAN_TB, PLAN_TB), 0)
    tcol = lax.broadcasted_iota(I32, (PLAN_TB, PLAN_TB), 1)
    tri = jnp.where(tcol < tr, 1.0, 0.0).astype(BF16)
    excl = _dot(tri, ohf.astype(BF16)) + carry_ref[...]
    mine = jnp.where(oh, excl, 0.0)
    r0 = jnp.sum(jnp.where(lane < N_EXPERTS, mine, 0.0), axis=-1, keepdims=True)
    r1 = jnp.sum(jnp.where(lane >= N_EXPERTS, mine, 0.0), axis=-1, keepdims=True)
    rank_ref[...] = jnp.where(lane == 0, r0, jnp.where(lane == 1, r1, 0.0))
    carry_ref[...] += jnp.sum(ohf, axis=0, keepdims=True)
    cnt_ref[...] = carry_ref[...]


def _moe_plan(ri):
    n = ri.shape[0]
    return pl.pallas_call(
        _plan_body,
        out_shape=(jax.ShapeDtypeStruct((n, LANES), F32), jax.ShapeDtypeStruct((1, LANES), F32)),
        grid=(n // PLAN_TB,),
        in_specs=[pl.BlockSpec((PLAN_TB, LANES), lambda i: (i, 0))],
        out_specs=(pl.BlockSpec((PLAN_TB, LANES), lambda i: (i, 0)),
                   pl.BlockSpec((1, LANES), lambda i: (0, 0))),
        scratch_shapes=[pltpu.VMEM((1, LANES), F32)],
        compiler_params=_cparams(("arbitrary",)),
        name="moe_plan",
    )(ri)


DISPATCH_TB = 1024


SUBLANES = 8


def _wait_rows(hbm_ref, nrows, sem):
    whole = hbm_ref.at[pl.ds(0, nrows)]
    pltpu.make_async_copy(whole, whole, sem).wait()


def _dispatch_body(pos_ref, h_ref, xs_in_ref, xs_ref, sem):
    del xs_in_ref
    base = pl.program_id(0) * DISPATCH_TB

    def issue(g, carry):
        for u in range(SUBLANES):
            tok = base + g * SUBLANES + u
            for k in range(2):
                pltpu.make_async_copy(h_ref.at[g, pl.ds(u, 1)],
                                      xs_ref.at[pl.ds(pos_ref[2 * tok + k], 1)], sem).start(priority=k)
        return carry

    lax.fori_loop(0, DISPATCH_TB // SUBLANES, issue, 0)

    for _ in range(2):
        _wait_rows(xs_ref, DISPATCH_TB, sem)


def _moe_dispatch(pos, hp):
    n, c = hp.shape
    xs0 = jnp.zeros((_moe_rows(n), c), U32)
    hp = hp.reshape(n // SUBLANES, SUBLANES, c)
    return pl.pallas_call(
        _dispatch_body,
        out_shape=jax.ShapeDtypeStruct(xs0.shape, U32),
        grid_spec=pltpu.PrefetchScalarGridSpec(
            num_scalar_prefetch=1,
            grid=(n // DISPATCH_TB,),
            in_specs=[pl.BlockSpec((DISPATCH_TB // SUBLANES, SUBLANES, c), lambda i, p: (i, 0, 0)),
                      pl.BlockSpec(memory_space=pl.ANY)],
            out_specs=pl.BlockSpec(memory_space=pl.ANY),
            scratch_shapes=[pltpu.SemaphoreType.DMA(())]),
        input_output_aliases={2: 0},
        compiler_params=_cparams(("arbitrary",)),
        name="moe_dispatch",
    )(pos, hp, xs0)


COMBINE_TB = 512


def _combine_body(pos_ref, x_ref, rw_ref, ys_ref, o_ref, buf, sem):
    base = pl.program_id(0) * COMBINE_TB

    def issue(g, carry):
        for u in range(SUBLANES):
            tok = base + g * SUBLANES + u
            for k in range(2):
                pltpu.make_async_copy(ys_ref.at[pl.ds(pos_ref[2 * tok + k], 1)],
                                      buf.at[k, g, pl.ds(u, 1)], sem).start(priority=k)
        return carry

    lax.fori_loop(0, COMBINE_TB // SUBLANES, issue, 0)
    for _ in range(2):
        _wait_rows(ys_ref, COMBINE_TB, sem)
    rw = rw_ref[...]
    rows = lambda k: _unpack_bf16_pairs(buf[k].reshape(COMBINE_TB, buf.shape[-1]))
    o_ref[...] = x_ref[...] + rw[:, 0:1] * rows(0) + rw[:, 1:2] * rows(1)


def _moe_combine(pos, x1, rw, ys):
    n = x1.shape[0]
    c = ys.shape[1]
    return pl.pallas_call(
        _combine_body,
        out_shape=jax.ShapeDtypeStruct((n, D_MODEL), F32),
        grid_spec=pltpu.PrefetchScalarGridSpec(
            num_scalar_prefetch=1,
            grid=(n // COMBINE_TB,),
            in_specs=[pl.BlockSpec((COMBINE_TB, D_MODEL), lambda i, p: (i, 0)),
                      pl.BlockSpec((COMBINE_TB, LANES), lambda i, p: (i, 0)),
                      pl.BlockSpec(memory_space=pl.ANY)],
            out_specs=pl.BlockSpec((COMBINE_TB, D_MODEL), lambda i, p: (i, 0)),
            scratch_shapes=[pltpu.VMEM((2, COMBINE_TB // SUBLANES, SUBLANES, c), U32),
                            pltpu.SemaphoreType.DMA(())]),
        compiler_params=_cparams(("arbitrary",)),
        name="moe_combine",
    )(pos, x1, rw, ys)


MOE_TF = 1792


def _experts_body(te_ref, nv_ref, xs_ref, wg_ref, wu_ref, wd_ref, ys_ref, xb_ref, acc_ref):
    i = pl.program_id(0)
    f = pl.program_id(1)
    nf = pl.num_programs(1)

    @pl.when(i < nv_ref[0])
    def _():
        @pl.when(f == 0)
        def _():
            xb_ref[...] = _unpack_bf16_pairs(xs_ref[...]).astype(BF16)

        h = xb_ref[...]
        part = None
        for c, w in _col_chunks(MOE_TF, 1024):
            a = _dot(h, wg_ref[0, :, c:c + w])
            act = (a * _sigmoid(a) * _dot(h, wu_ref[0, :, c:c + w])).astype(BF16)
            pc = _dot(act, wd_ref[0, c:c + w, :])
            part = pc if part is None else part + pc

        @pl.when(f == 0)
        def _():
            acc_ref[...] = part

        @pl.when(f > 0)
        def _():
            acc_ref[...] += part

        @pl.when(f == nf - 1)
        def _():
            ys_ref[...] = _pack_bf16_pairs(acc_ref[...])

    @pl.when((i >= nv_ref[0]) & (f == nf - 1))
    def _():
        ys_ref[...] = jnp.zeros_like(ys_ref)


def _moe_experts(tile_expert, n_valid, xs, wg, wu, wd):
    rows, c = xs.shape
    nt = rows // MOE_TM
    nf = D_FF_EXPERT // MOE_TF

    def tile(i, nv):
        return jnp.minimum(i, nv[0] - 1)

    def fblk(i, f, nv):
        return jnp.where(i < nv[0], f, nf - 1)

    return pl.pallas_call(
        _experts_body,
        out_shape=jax.ShapeDtypeStruct((rows, c), U32),
        grid_spec=pltpu.PrefetchScalarGridSpec(
            num_scalar_prefetch=2,
            grid=(nt, nf),
            in_specs=[pl.BlockSpec((MOE_TM, c), lambda i, f, te, nv: (tile(i, nv), 0)),
                      pl.BlockSpec((1, D_MODEL, MOE_TF),
                                   lambda i, f, te, nv: (te[tile(i, nv)], 0, fblk(i, f, nv))),
                      pl.BlockSpec((1, D_MODEL, MOE_TF),
                                   lambda i, f, te, nv: (te[tile(i, nv)], 0, fblk(i, f, nv))),
                      pl.BlockSpec((1, MOE_TF, D_MODEL),
                                   lambda i, f, te, nv: (te[tile(i, nv)], fblk(i, f, nv), 0))],
            out_specs=pl.BlockSpec((MOE_TM, c), lambda i, f, te, nv: (i, 0)),
            scratch_shapes=[pltpu.VMEM((MOE_TM, D_MODEL), BF16), pltpu.VMEM((MOE_TM, D_MODEL), F32)]),
        compiler_params=_cparams(("arbitrary", "arbitrary")),
        name="moe_experts",
    )(tile_expert, n_valid, xs, wg, wu, wd)


def _positions_body(ri_ref, rank_ref, start_ref, pos_ref):
    ri = ri_ref[...]
    rank = rank_ref[...]
    lane = lax.broadcasted_iota(I32, ri.shape, 1)
    first = jnp.where(lane == ri[:, 0:1], start_ref[...], 0.0)
    second = jnp.where(lane == ri[:, 1:2] + N_EXPERTS, start_ref[...], 0.0)
    p0 = jnp.sum(first, axis=-1, keepdims=True) + rank[:, 0:1]
    p1 = jnp.sum(second, axis=-1, keepdims=True) + rank[:, 1:2]
    pos_ref[...] = jnp.where(lane == 0, p0, jnp.where(lane == 1, p1, 0.0)).astype(I32)


def _moe_positions(ri, rank, starts):
    n = ri.shape[0]
    row = lambda i: (i, 0)
    return pl.pallas_call(
        _positions_body,
        out_shape=jax.ShapeDtypeStruct((n, LANES), I32),
        grid=(n // PLAN_TB,),
        in_specs=[pl.BlockSpec((PLAN_TB, LANES), row), pl.BlockSpec((PLAN_TB, LANES), row),
                  pl.BlockSpec((1, LANES), lambda i: (0, 0))],
        out_specs=pl.BlockSpec((PLAN_TB, LANES), row),
        compiler_params=_cparams(("parallel",)),
        name="moe_positions",
    )(ri, rank, starts)


def _moe_ffn(x1, hp, ri, rw, wg, wu, wd):
    rank, cnt = _moe_plan(ri)
    c0 = cnt[0, :N_EXPERTS].astype(I32)
    c1 = cnt[0, N_EXPERTS:2 * N_EXPERTS].astype(I32)
    padded = ((c0 + c1 + MOE_TM - 1) // MOE_TM) * MOE_TM
    ends = jnp.cumsum(padded)
    off = ends - padded
    starts = jnp.zeros((1, LANES), F32).at[0, :2 * N_EXPERTS].set(
        jnp.concatenate([off, off + c0]).astype(F32))
    pos = _moe_positions(ri, rank, starts)[:, :2].reshape(-1)
    nt = _moe_rows(x1.shape[0]) // MOE_TM
    tile_start = jnp.arange(nt, dtype=I32) * MOE_TM
    tile_expert = jnp.minimum(jnp.sum(tile_start[:, None] >= ends[None, :], axis=1),
                              N_EXPERTS - 1).astype(I32)
    n_valid = (ends[-1:] // MOE_TM).astype(I32)
    xs = _moe_dispatch(pos, hp)
    ys = _moe_experts(tile_expert, n_valid, xs, wg, wu, wd)
    return _moe_combine(pos, x1, rw, ys)


def _head_consts():
    lane = jnp.arange(ATT_OUT)
    bd = jnp.where((lane[:, None] // ATT_HEAD_DIM) == (lane[None, :] // ATT_HEAD_DIM),
                   1.0 / ATT_HEAD_DIM, 0.0).astype(BF16)
    perm = (lane[:, None] == (lane[None, :] ^ (ATT_HEAD_DIM // 2))).astype(BF16)
    return bd, perm


def _head_gains(g):
    full = jnp.tile(g.astype(F32), ATT_SLOTS).reshape(1, ATT_OUT)
    half = ATT_HEAD_DIM // 2
    swapped = jnp.tile(jnp.concatenate([g[half:], g[:half]]).astype(F32), ATT_SLOTS).reshape(1, ATT_OUT)
    return full, swapped


def kernel(x, mem, positions, norm_mix, w_in, ssm_a_re, ssm_a_im, ssm_log_dt, ssm_b_re, ssm_b_im,
           ssm_c_re, ssm_c_im, ssm_d, ssm_w_glu, w_ssm_out, att_q_norm, att_k_norm, w_att_out,
           norm_mem, w_mem_kv, mem_q_norm, mem_k_norm, w_mem_out, w_o, norm_ffn, ffn_w_gate,
           ffn_w_up, ffn_w_down, moe_w_router, moe_w_gate, moe_w_up, moe_w_down):
    bsz, seq, d = x.shape
    ntok = bsz * seq
    depth = w_in.shape[0]
    cos_t, sin_t = _rope_tables(positions)
    bd, perm = _head_consts()
    mem2d = mem.reshape(bsz * MEM_LEN, d)
    x2d = x.reshape(ntok, d)
    flat = lambda a: a.reshape((-1,) + a.shape[2:])
    ssm_ops = _ssm_operators(flat(ssm_a_re), flat(ssm_a_im), flat(ssm_log_dt), flat(ssm_b_re),
                             flat(ssm_b_im), flat(ssm_c_re), flat(ssm_c_im), ssm_d)
    for i in range(depth):
        wi = w_in[i]
        g_mix = norm_mix[i].reshape(1, d)

        wut = jnp.transpose(wi[:, :SSM_WIDTH]).astype(BF16)
        ut, qkv, memq, gates = _inproj_main(x2d, g_mix, wut, wi.astype(BF16))

        yt = _ssm_scan(ut, ssm_ops, i, bsz, seq)

        gq, gqs = _head_gains(att_q_norm[i])
        gk, gks = _head_gains(att_k_norm[i])
        att = []
        for gi, (_, dil) in enumerate(DIL_PAIRS):
            att.extend(_dilated_attention_group(qkv, gi, cos_t, sin_t, gq, gqs, gk, gks, bd, perm,
                                                bsz, seq, dil))

        kmem, vmem = _memory_kv(mem2d, norm_mem[i].reshape(1, d), w_mem_kv[i].astype(BF16),
                                mem_k_norm[i].reshape(1, X_HEAD_DIM))

        wts = (jnp.transpose(ssm_w_glu[i]).astype(BF16), w_ssm_out[i].astype(BF16),
               w_att_out[i].astype(BF16), w_mem_out[i].astype(BF16), w_o[i].astype(BF16),
               mem_q_norm[i].reshape(1, X_HEAD_DIM), norm_ffn[i].reshape(1, d))
        j = i // 2
        if i % 2 == 0:
            x1, h2 = _merge(x2d, memq, gates, yt, att, kmem, vmem, wts, bsz, seq)
            x2d = _dense_ffn(h2, x1, ffn_w_gate[j].astype(BF16), ffn_w_up[j].astype(BF16),
                             ffn_w_down[j].astype(BF16))
        else:
            wr = jnp.zeros((d, LANES), F32).at[:, :N_EXPERTS].set(moe_w_router[j])
            wr_hi = wr.astype(BF16)
            wr_lo = (wr - wr_hi.astype(F32)).astype(BF16)
            x1, hp, ri, rw = _merge(x2d, memq, gates, yt, att, kmem, vmem, wts, bsz, seq,
                                    router=(wr_hi, wr_lo))
            x2d = _moe_ffn(x1, hp, ri, rw, moe_w_gate[j].astype(BF16), moe_w_up[j].astype(BF16),
                           moe_w_down[j].astype(BF16))
    return x2d.reshape(bsz, seq, d)
```

```python
import functools
import math

import jax
import jax.numpy as jnp
from jax import lax
from jax.experimental import pallas as pl
from jax.experimental.pallas import tpu as pltpu

F32 = jnp.float32
BF16 = jnp.bfloat16
I32 = jnp.int32
U32 = jnp.uint32

EPS = 1e-6
D_MODEL = 1024
MEM_LEN = 256
SSM_WIDTH = 512
SSM_GROUP = 16
SSM_GROUPS = 32
SSM_STATE = 64
ATT_HEAD_DIM = 64
ATT_SLOTS = 4
DIL_PAIRS = ((128, 1), (512, 4), (2048, 16))
ATT_WIDTH = 768
ATT_OUT = 256
BLOCK = 128
ROPE_THETA = 10000.0
X_HEADS = 4
X_HEAD_DIM = 128
X_WIDTH = 512
D_FF = 2816
N_EXPERTS = 8
D_FF_EXPERT = 3584

LANES = 128
SSM_CHUNK = 128
QKV_W = 3 * ATT_OUT
VMEM_LIMIT = 56 * 1024 * 1024


def _cparams(sem, vmem=VMEM_LIMIT):
    return pltpu.CompilerParams(dimension_semantics=sem, vmem_limit_bytes=vmem)


def _rms(x, g):
    ms = jnp.mean(x * x, axis=-1, keepdims=True)
    return x * lax.rsqrt(ms + EPS) * g


def _sigmoid(x):
    return 0.5 * jnp.tanh(0.5 * x) + 0.5


def _dot(a, b):
    return jnp.dot(a, b, preferred_element_type=F32)


def _dot_nt(a, b):
    return lax.dot_general(a, b, (((1,), (1,)), ((), ())), preferred_element_type=F32)


def _rope_body(pos_ref, inv_ref, sgn_ref, cos_ref, sin_ref):
    ang = pos_ref[...].astype(F32) * inv_ref[...]
    cos_ref[...] = jnp.cos(ang)
    sin_ref[...] = jnp.sin(ang) * sgn_ref[...]


def _rope_tables(positions):
    n = positions.size
    half = ATT_HEAD_DIM // 2
    inv = ROPE_THETA ** (-jnp.arange(half, dtype=F32) / half)
    inv_row = jnp.tile(inv, LANES // half).reshape(1, LANES)
    lane = jnp.arange(LANES)
    sgn_row = jnp.where((lane % ATT_HEAD_DIM) < half, -1.0, 1.0).astype(F32).reshape(1, LANES)
    tm = 2048
    return pl.pallas_call(
        _rope_body,
        out_shape=(jax.ShapeDtypeStruct((n, LANES), F32), jax.ShapeDtypeStruct((n, LANES), F32)),
        grid=(n // tm,),
        in_specs=[pl.BlockSpec((tm, 1), lambda i: (i, 0)),
                  pl.BlockSpec((1, LANES), lambda i: (0, 0)),
                  pl.BlockSpec((1, LANES), lambda i: (0, 0))],
        out_specs=(pl.BlockSpec((tm, LANES), lambda i: (i, 0)),
                   pl.BlockSpec((tm, LANES), lambda i: (i, 0))),
        compiler_params=_cparams(("parallel",)),
        name="rope_tables",
    )(positions.reshape(n, 1), inv_row, sgn_row)


def _col_chunks(width, step=512):
    return [(c, min(step, width - c)) for c in range(0, width, step)]


def _inproj_main_body(x_ref, g_ref, wut_ref, w_ref, ut_ref, qkv_ref, memq_ref, gate_ref):
    h = _rms(x_ref[...], g_ref[...]).astype(BF16)
    ut_ref[...] = _dot_nt(wut_ref[...], h).astype(BF16)
    for gi in range(len(DIL_PAIRS)):
        for j in range(3):
            src = SSM_WIDTH + j * ATT_WIDTH + gi * ATT_OUT
            dst = gi * QKV_W + j * ATT_OUT
            qkv_ref[:, dst:dst + ATT_OUT] = _dot(h, w_ref[:, src:src + ATT_OUT]).astype(BF16)
    col = SSM_WIDTH + 3 * ATT_WIDTH
    for ref in (memq_ref, gate_ref):
        for c, w in _col_chunks(ref.shape[1]):
            ref[:, c:c + w] = _dot(h, w_ref[:, col + c:col + c + w]).astype(BF16)
        col += ref.shape[1]


def _inproj_main(x2d, g, wut, wmain):
    n = x2d.shape[0]
    tm = 512
    row = lambda i: (i, 0)
    const = lambda i: (0, 0)
    widths = (3 * QKV_W, X_WIDTH, 3 * D_MODEL)
    return pl.pallas_call(
        _inproj_main_body,
        out_shape=(jax.ShapeDtypeStruct((SSM_WIDTH, n), BF16),)
        + tuple(jax.ShapeDtypeStruct((n, w), BF16) for w in widths),
        grid=(n // tm,),
        in_specs=[pl.BlockSpec((tm, D_MODEL), row),
                  pl.BlockSpec((1, D_MODEL), const),
                  pl.BlockSpec((SSM_WIDTH, D_MODEL), const, pipeline_mode=pl.Buffered(1)),
                  pl.BlockSpec(wmain.shape, const, pipeline_mode=pl.Buffered(1))],
        out_specs=(pl.BlockSpec((SSM_WIDTH, tm), lambda i: (0, i)),)
        + tuple(pl.BlockSpec((tm, w), row) for w in widths),
        compiler_params=_cparams(("parallel",)),
        name="inproj_main",
    )(x2d, g, wut, wmain)


def _qk_prep(xf, cos2, sin2, g, gs, bd, perm, scale):
    xb = xf.astype(BF16)
    ms = _dot((xf * xf).astype(BF16), bd)
    xs = _dot(xb, perm)
    y = lax.rsqrt(ms + EPS) * scale * (xf * (g * cos2) + xs * (gs * sin2))
    return y.astype(BF16)


def _attn_body(qkv_ref, cos_ref, sin_ref, gq_ref, gqs_ref, gk_ref, gks_ref, bd_ref, perm_ref,
               o_ref, lse_ref, sbuf, qbuf, kbuf, vbuf, *, dil, tq):
    j = pl.program_id(1)

    @pl.when(j == 0)
    def _():
        kbuf[:, 0:BLOCK, :] = jnp.zeros((dil, BLOCK, ATT_OUT), BF16)
        vbuf[:, 0:BLOCK, :] = jnp.zeros((dil, BLOCK, ATT_OUT), BF16)

    @pl.when(j > 0)
    def _():
        kbuf[:, 0:BLOCK, :] = kbuf[:, tq:tq + BLOCK, :]
        vbuf[:, 0:BLOCK, :] = vbuf[:, tq:tq + BLOCK, :]

    bd = bd_ref[...]
    perm = perm_ref[...]
    lane = lax.broadcasted_iota(I32, (BLOCK, LANES), 1)
    low = lane < ATT_HEAD_DIM
    qi = lax.broadcasted_iota(I32, (BLOCK, 2 * BLOCK), 0) + BLOCK
    ki = lax.broadcasted_iota(I32, (BLOCK, 2 * BLOCK), 1)
    off = qi - ki
    band = (off >= 0) & (off <= BLOCK)
    band_first = band & ((ki >= BLOCK) | (j > 0))
    for c in range(QKV_W // LANES):
        sbuf[c] = qkv_ref[:, c * LANES:(c + 1) * LANES].astype(F32)

    def rows(start, size):
        return pl.ds(start, size, stride=dil) if dil > 1 else pl.ds(start, size)

    def planes(first, sel):
        return jnp.concatenate([sbuf[first, sel, :], sbuf[first + 1, sel, :]], axis=1)

    def prep(r, carry):
        cos = cos_ref[rows(r, tq), :]
        sin = sin_ref[rows(r, tq), :]
        cos2 = jnp.concatenate([cos, cos], axis=1)
        sin2 = jnp.concatenate([sin, sin], axis=1)
        cur = rows(r, tq)
        qbuf[r] = _qk_prep(planes(0, cur), cos2, sin2, gq_ref[...], gqs_ref[...],
                           bd, perm, ATT_HEAD_DIM ** -0.5)
        kbuf[r, BLOCK:, :] = _qk_prep(planes(2, cur), cos2, sin2,
                                      gk_ref[...], gks_ref[...], bd, perm, 1.0)
        vbuf[r, BLOCK:, :] = planes(4, cur).astype(BF16)
        return carry

    if dil == 1:
        prep(0, 0)
    else:
        lax.fori_loop(0, dil, prep, 0, unroll=2)

    for r in range(dil):
        for s in range(tq // BLOCK):
            row0 = s * BLOCK
            valid = band_first if s == 0 else band
            dst = rows(r + row0 * dil, BLOCK)
            for p in range(ATT_OUT // LANES):
                cols = slice(p * LANES, (p + 1) * LANES)
                qp = qbuf[r, row0:row0 + BLOCK, cols]
                kp = kbuf[r, row0:row0 + 2 * BLOCK, cols]
                vp = vbuf[r, row0:row0 + 2 * BLOCK, cols]
                outs, lses = [], []
                for h in range(2):
                    qm = jnp.where(low if h == 0 else ~low, qp, jnp.zeros_like(qp))
                    sc = jnp.where(valid, _dot_nt(qm, kp), -1e30)
                    m = jnp.max(sc, axis=-1, keepdims=True)
                    pr = jnp.exp(sc - m)
                    den = jnp.sum(pr, axis=-1, keepdims=True)
                    outs.append(_dot(pr.astype(BF16), vp) / den)
                    lses.append(m + jnp.log(den))
                o_ref[p, dst, :] = jnp.where(low, outs[0], outs[1])
                lse_ref[p, dst, :] = jnp.where(low, lses[0], lses[1])


def _dilated_attention_group(qkv_all, gi, cos_t, sin_t, gq, gqs, gk, gks, bd, perm, bsz, seq, dil):
    ntok = bsz * seq
    tt = max(1024, BLOCK * dil)
    tq = tt // dil
    nblk = seq // tt
    const = lambda b, j: (0, 0)
    return pl.pallas_call(
        functools.partial(_attn_body, dil=dil, tq=tq),
        out_shape=(jax.ShapeDtypeStruct((ATT_OUT // LANES, ntok, LANES), F32),
                   jax.ShapeDtypeStruct((ATT_OUT // LANES, ntok, LANES), F32)),
        grid=(bsz, nblk),
        in_specs=[pl.BlockSpec((tt, QKV_W), lambda b, j: (b * nblk + j, gi)),
                  pl.BlockSpec((tt, LANES), lambda b, j: (b * nblk + j, 0)),
                  pl.BlockSpec((tt, LANES), lambda b, j: (b * nblk + j, 0)),
                  pl.BlockSpec((1, ATT_OUT), const), pl.BlockSpec((1, ATT_OUT), const),
                  pl.BlockSpec((1, ATT_OUT), const), pl.BlockSpec((1, ATT_OUT), const),
                  pl.BlockSpec((ATT_OUT, ATT_OUT), const), pl.BlockSpec((ATT_OUT, ATT_OUT), const)],
        out_specs=(pl.BlockSpec((ATT_OUT // LANES, tt, LANES), lambda b, j: (0, b * nblk + j, 0)),
                   pl.BlockSpec((ATT_OUT // LANES, tt, LANES), lambda b, j: (0, b * nblk + j, 0))),
        scratch_shapes=[pltpu.VMEM((QKV_W // LANES, tt, LANES), F32),
                        pltpu.VMEM((dil, tq, ATT_OUT), BF16),
                        pltpu.VMEM((dil, BLOCK + tq, ATT_OUT), BF16),
                        pltpu.VMEM((dil, BLOCK + tq, ATT_OUT), BF16)],
        compiler_params=_cparams(("arbitrary", "arbitrary")),
        name=f"dilated_attn{dil}",
    )(qkv_all, cos_t, sin_t, gq, gqs, gk, gks, bd, perm)


def _ssm_body(u_ref, ktab_ref, w_ref, v_ref, lam_ref, dvec_ref, y_ref, m_ref, m2_ref, sloc_ref,
              ssw_ref, sin_ref, yin_ref, *, bsz, cpb):
    tc = SSM_CHUNK
    row = lax.broadcasted_iota(I32, (tc, tc), 0)
    col = lax.broadcasted_iota(I32, (tc, tc), 1)
    causal = col >= row

    u = jnp.concatenate([u_ref[c] for c in range(SSM_GROUP)], axis=1)
    sloc = _dot(u, w_ref[0])
    sloc_ref[...] = sloc
    ssw_ref[...] = pltpu.roll(sloc, SSM_STATE, 1)

    a1 = lam_ref[0, 0:1, :]
    a2 = lam_ref[0, 1:2, :]
    s = jnp.zeros((bsz, 2 * SSM_STATE), F32)
    sw = s
    sin_ref[pl.ds(0, bsz, stride=cpb), :] = s
    for k in range(1, cpb):
        prev = pl.ds(k - 1, bsz, stride=cpb)
        s, sw = (a1 * s + a2 * sw + sloc_ref[prev, :], a1 * sw - a2 * s + ssw_ref[prev, :])
        sin_ref[pl.ds(k, bsz, stride=cpb), :] = s

    s_in = sin_ref[...].astype(BF16)
    nblk = SSM_GROUP // 2

    def y_in(c):
        return _dot(s_in, v_ref[0, c]) + dvec_ref[0, :, c * tc:(c + 1) * tc] * u_ref[c].astype(F32)

    for cb in range(nblk):
        yin_ref[cb] = jnp.concatenate([y_in(2 * cb), y_in(2 * cb + 1)], axis=1)

    def build(cb, dst):
        for cp in range(SSM_GROUP):
            for h in range(2):
                kv = ktab_ref[0, pl.ds(cp * SSM_GROUP + 2 * cb + h, 1), :]
                tile = pltpu.roll(jnp.broadcast_to(kv, (tc, tc)), 0, 1, stride=1, stride_axis=0)
                dst[cp * tc:(cp + 1) * tc, h * tc:(h + 1) * tc] = (
                    jnp.where(causal, tile, 0.0).astype(BF16))

    def multiply(cb, src):
        y = jax.nn.gelu(_dot(u, src[...]) + yin_ref[cb], approximate=True)
        y_ref[2 * cb] = y[:, :tc].astype(BF16)
        y_ref[2 * cb + 1] = y[:, tc:].astype(BF16)

    build(0, m_ref)

    def pair(i, carry):
        build(2 * i + 1, m2_ref)
        multiply(2 * i, m_ref)
        build(jnp.minimum(2 * i + 2, nblk - 1), m_ref)
        multiply(2 * i + 1, m2_ref)
        return carry

    lax.fori_loop(0, nblk // 2, pair, 0)


def _ssm_operators(a_re, a_im, log_dt, b_re, b_im, c_re, c_im, d_skip):
    tc = SSM_CHUNK
    ng = a_re.shape[0]
    lam = lax.complex(a_re.astype(F32), a_im.astype(F32))
    dt = jnp.exp(log_dt.astype(F32))[:, None]
    lam_dt = lam * dt
    lam_bar = jnp.exp(lam_dt)
    b = lax.complex(b_re.astype(F32), b_im.astype(F32))
    b_bar = ((lam_bar - 1.0) / lam)[..., None] * b
    c = lax.complex(c_re.astype(F32), c_im.astype(F32))
    k = jnp.arange(tc + 1, dtype=F32)
    pw = jnp.exp(lam_dt[:, None, :] * k[None, :, None])
    ktab = jnp.einsum('gcp,gkp,gpd->gdck', c, pw[:, :tc], b_bar).real
    ktab = ktab.reshape(ng, SSM_GROUP * SSM_GROUP, tc).astype(F32)
    wc = jnp.einsum('gjp,gpd->gdjp', pw[:, tc - 1::-1][:, :tc], b_bar)
    wc = wc.reshape(ng, SSM_GROUP * tc, SSM_STATE)
    w = jnp.concatenate([wc.real, wc.imag], axis=-1).astype(BF16)
    vc = jnp.einsum('gcp,gtp->gcpt', c, pw[:, 1:tc + 1])
    v = jnp.concatenate([vc.real, -vc.imag], axis=2).astype(BF16)
    lt = pw[:, tc]
    lam_rows = jnp.stack([jnp.concatenate([lt.real, lt.real], -1),
                          jnp.concatenate([-lt.imag, lt.imag], -1)], axis=1).astype(F32)
    dvec = jnp.repeat(d_skip.astype(F32).reshape(ng, SSM_GROUP), tc, axis=1)
    return ktab, w, v, lam_rows, dvec.reshape(ng, 1, SSM_GROUP * tc)


def _ssm_scan(ut, ops, layer, bsz, seq):
    ktab, w, v, lam_rows, dvec = ops
    ntok = bsz * seq
    tc = SSM_CHUNK
    nch = ntok // tc
    u3 = ut.reshape(SSM_WIDTH, nch, tc)
    gmap = lambda g: (g, 0, 0)
    tmap = lambda g: (layer * SSM_GROUPS + g, 0, 0)
    y3 = pl.pallas_call(
        functools.partial(_ssm_body, bsz=bsz, cpb=seq // tc),
        out_shape=jax.ShapeDtypeStruct((SSM_WIDTH, nch, tc), BF16),
        grid=(SSM_GROUPS,),
        in_specs=[pl.BlockSpec((SSM_GROUP, nch, tc), gmap),
                  pl.BlockSpec((1, SSM_GROUP * SSM_GROUP, tc), tmap),
                  pl.BlockSpec((1, SSM_GROUP * tc, 2 * SSM_STATE), tmap),
                  pl.BlockSpec((1, SSM_GROUP, 2 * SSM_STATE, tc),
                               lambda g: (layer * SSM_GROUPS + g, 0, 0, 0)),
                  pl.BlockSpec((1, 2, 2 * SSM_STATE), tmap),
                  pl.BlockSpec((1, 1, SSM_GROUP * tc), tmap)],
        out_specs=pl.BlockSpec((SSM_GROUP, nch, tc), gmap),
        scratch_shapes=[pltpu.VMEM((SSM_GROUP * tc, 2 * tc), BF16),
                        pltpu.VMEM((SSM_GROUP * tc, 2 * tc), BF16),
                        pltpu.VMEM((nch, 2 * SSM_STATE), F32),
                        pltpu.VMEM((nch, 2 * SSM_STATE), F32),
                        pltpu.VMEM((nch, 2 * SSM_STATE), F32),
                        pltpu.VMEM((SSM_GROUP // 2, nch, 2 * tc), F32)],
        compiler_params=_cparams(("parallel",)),
        name="ssm_scan",
    )(u3, ktab, w, v, lam_rows, dvec)
    return y3.reshape(SSM_WIDTH, ntok)


def _memkv_body(mem_ref, g_ref, w_ref, gk_ref, k_ref, v_ref):
    h = _rms(mem_ref[...], g_ref[...]).astype(BF16)
    kv = _dot(h, w_ref[...])
    for hd in range(X_HEADS):
        cols = slice(hd * X_HEAD_DIM, (hd + 1) * X_HEAD_DIM)
        k_ref[:, cols] = _rms(kv[:, cols], gk_ref[...]).astype(BF16)
    v_ref[...] = kv[:, X_WIDTH:].astype(BF16)


def _memory_kv(mem2d, g, w_kv, gk):
    m = mem2d.shape[0]
    tm = MEM_LEN
    return pl.pallas_call(
        _memkv_body,
        out_shape=(jax.ShapeDtypeStruct((m, X_WIDTH), BF16), jax.ShapeDtypeStruct((m, X_WIDTH), BF16)),
        grid=(m // tm,),
        in_specs=[pl.BlockSpec((tm, D_MODEL), lambda i: (i, 0)),
                  pl.BlockSpec((1, D_MODEL), lambda i: (0, 0)),
                  pl.BlockSpec((D_MODEL, 2 * X_WIDTH), lambda i: (0, 0)),
                  pl.BlockSpec((1, X_HEAD_DIM), lambda i: (0, 0))],
        out_specs=(pl.BlockSpec((tm, X_WIDTH), lambda i: (i, 0)),
                   pl.BlockSpec((tm, X_WIDTH), lambda i: (i, 0))),
        compiler_params=_cparams(("parallel",)),
        name="memory_kv",
    )(mem2d, g, w_kv, gk)


def _pack_bf16_pairs(x):
    c = x.shape[1] // 2
    bits = pltpu.bitcast(x.astype(BF16).astype(F32), U32)
    return (bits[:, :c] & jnp.uint32(0xFFFF0000)) | (bits[:, c:] >> 16)


def _unpack_bf16_pairs(p):
    hi = pltpu.bitcast(p & jnp.uint32(0xFFFF0000), F32)
    lo = pltpu.bitcast(p << 16, F32)
    return jnp.concatenate([hi, lo], axis=1)


def _merge_body(*refs, moe):
    (x_ref, memq_ref, gate_ref, yt_ref, o0_ref, l0_ref, o1_ref, l1_ref, o2_ref, l2_ref,
     km_ref, vm_ref, wglut_ref, wso_ref, wao_ref, wmo_ref, wo_ref, gmq_ref, gffn_ref) = refs[:19]
    if moe:
        wrh_ref, wrl_ref, x1_ref, hp_ref, ri_ref, rw_ref = refs[19:]
    else:
        x1_ref, h2_ref = refs[19:]

    ga = _dot(wglut_ref[...], yt_ref[...])
    glu = ga[:SSM_WIDTH] * _sigmoid(ga[SSM_WIDTH:])
    y_ssm = _dot(jnp.transpose(glu).astype(BF16), wso_ref[...])

    planes = []
    for p in range(ATT_OUT // LANES):
        l0, l1, l2 = l0_ref[p], l1_ref[p], l2_ref[p]
        mx = jnp.maximum(jnp.maximum(l0, l1), l2)
        e0, e1, e2 = jnp.exp(l0 - mx), jnp.exp(l1 - mx), jnp.exp(l2 - mx)
        planes.append((e0 * o0_ref[p] + e1 * o1_ref[p] + e2 * o2_ref[p]) / (e0 + e1 + e2))
    y_att = _dot(jnp.concatenate(planes, axis=1).astype(BF16), wao_ref[...])

    heads = []
    for hd in range(X_HEADS):
        cols = slice(hd * X_HEAD_DIM, (hd + 1) * X_HEAD_DIM)
        q = (_rms(memq_ref[:, cols].astype(F32), gmq_ref[...]) * (X_HEAD_DIM ** -0.5)).astype(BF16)
        sc = _dot_nt(q, km_ref[:, cols])
        m = jnp.max(sc, axis=-1, keepdims=True)
        pr = jnp.exp(sc - m)
        den = jnp.sum(pr, axis=-1, keepdims=True)
        heads.append(_dot(pr.astype(BF16), vm_ref[:, cols]) / den)
    y_mem = _dot(jnp.concatenate(heads, axis=1).astype(BF16), wmo_ref[...])

    d = D_MODEL
    merged = (_sigmoid(gate_ref[:, 0:d].astype(F32)) * y_ssm
              + _sigmoid(gate_ref[:, d:2 * d].astype(F32)) * y_att
              + _sigmoid(gate_ref[:, 2 * d:3 * d].astype(F32)) * y_mem)
    x1 = x_ref[...] + _dot(merged.astype(BF16), wo_ref[...])
    x1_ref[...] = x1
    h2 = _rms(x1, gffn_ref[...])
    if not moe:
        h2_ref[...] = h2.astype(BF16)
        return

    hp_ref[...] = _pack_bf16_pairs(h2)
    hi = h2.astype(BF16)
    lo = (h2 - hi.astype(F32)).astype(BF16)
    logits = _dot(hi, wrh_ref[...]) + _dot(hi, wrl_ref[...]) + _dot(lo, wrh_ref[...])
    lane = lax.broadcasted_iota(I32, logits.shape, 1)
    lg = jnp.where(lane < N_EXPERTS, logits, -jnp.inf)
    v1 = jnp.max(lg, axis=-1, keepdims=True)
    i1 = jnp.min(jnp.where(lg == v1, lane, LANES), axis=-1, keepdims=True)
    lg2 = jnp.where(lane == i1, -jnp.inf, lg)
    v2 = jnp.max(lg2, axis=-1, keepdims=True)
    i2 = jnp.min(jnp.where(lg2 == v2, lane, LANES), axis=-1, keepdims=True)
    e = jnp.exp(v2 - v1)
    ri_ref[...] = jnp.where(lane == 0, i1, jnp.where(lane == 1, i2, 0))
    rw_ref[...] = jnp.where(lane == 0, 1.0 / (1.0 + e), jnp.where(lane == 1, e / (1.0 + e), 0.0))


def _merge(x2d, memq, gates, yt, att, kmem, vmem, wts, bsz, seq, router=None):
    n = x2d.shape[0]
    tm = 512
    tpb = seq // tm
    moe = router is not None
    row = lambda i: (i, 0)
    const = lambda i: (0, 0)
    in_specs = [pl.BlockSpec((tm, D_MODEL), row),
                pl.BlockSpec((tm, X_WIDTH), row),
                pl.BlockSpec((tm, 3 * D_MODEL), row),
                pl.BlockSpec((SSM_WIDTH, tm), lambda i: (0, i))]
    in_specs += [pl.BlockSpec((ATT_OUT // LANES, tm, LANES), lambda i: (0, i, 0))] * 6
    in_specs += [pl.BlockSpec((MEM_LEN, X_WIDTH), lambda i: (i // tpb, 0))] * 2
    wglut, wso, wao, wmo, wo, gmq, gffn = wts
    in_specs += [pl.BlockSpec(w.shape, const) for w in (wglut, wso, wao, wmo, wo, gmq, gffn)]
    args = [x2d, memq, gates, yt, *att, kmem, vmem, wglut, wso, wao, wmo, wo, gmq, gffn]
    if moe:
        in_specs += [pl.BlockSpec(router[0].shape, const)] * 2
        args += list(router)
        out_shape = (jax.ShapeDtypeStruct((n, D_MODEL), F32),
                     jax.ShapeDtypeStruct((n, D_MODEL // 2), U32),
                     jax.ShapeDtypeStruct((n, LANES), I32),
                     jax.ShapeDtypeStruct((n, LANES), F32))
        out_specs = (pl.BlockSpec((tm, D_MODEL), row), pl.BlockSpec((tm, D_MODEL // 2), row),
                     pl.BlockSpec((tm, LANES), row), pl.BlockSpec((tm, LANES), row))
    else:
        out_shape = (jax.ShapeDtypeStruct((n, D_MODEL), F32), jax.ShapeDtypeStruct((n, D_MODEL), BF16))
        out_specs = (pl.BlockSpec((tm, D_MODEL), row), pl.BlockSpec((tm, D_MODEL), row))
    return pl.pallas_call(
        functools.partial(_merge_body, moe=moe),
        out_shape=out_shape,
        grid=(n // tm,),
        in_specs=in_specs,
        out_specs=out_specs,
        compiler_params=_cparams(("parallel",)),
        name="merge_moe" if moe else "merge_dense",
    )(*args)


FFN_CHUNK = 768


def _ffn_body(h_ref, x_ref, wg_ref, wu_ref, wd_ref, o_ref):
    h = h_ref[...]
    acc = x_ref[...]
    for c, w in _col_chunks(D_FF, FFN_CHUNK):
        a = _dot(h, wg_ref[:, c:c + w])
        act = (a * _sigmoid(a) * _dot(h, wu_ref[:, c:c + w])).astype(BF16)
        acc = acc + _dot(act, wd_ref[c:c + w, :])
    o_ref[...] = acc


def _dense_ffn(h2, x1, wg, wu, wd):
    n = h2.shape[0]
    tm = 512
    row = lambda i: (i, 0)
    resident = lambda shape: pl.BlockSpec(shape, lambda i: (0, 0), pipeline_mode=pl.Buffered(1))
    return pl.pallas_call(
        _ffn_body,
        out_shape=jax.ShapeDtypeStruct((n, D_MODEL), F32),
        grid=(n // tm,),
        in_specs=[pl.BlockSpec((tm, D_MODEL), row),
                  pl.BlockSpec((tm, D_MODEL), row),
                  resident((D_MODEL, D_FF)), resident((D_MODEL, D_FF)), resident((D_FF, D_MODEL))],
        out_specs=pl.BlockSpec((tm, D_MODEL), row),
        compiler_params=_cparams(("parallel",)),
        name="dense_ffn",
    )(h2, x1, wg, wu, wd)


MOE_TM = 512
PLAN_TB = 512


def _moe_rows(ntok):
    return 2 * ntok + N_EXPERTS * MOE_TM


def _plan_body(ri_ref, rank_ref, cnt_ref, carry_ref):
    i = pl.program_id(0)

    @pl.when(i == 0)
    def _():
        carry_ref[...] = jnp.zeros_like(carry_ref)

    ri = ri_ref[...]
    lane = lax.broadcasted_iota(I32, ri.shape, 1)
    e1 = ri[:, 0:1]
    e2 = ri[:, 1:2]
    oh = (jnp.where(lane < N_EXPERTS, e1, e2 + N_EXPERTS) == lane) & (lane < 2 * N_EXPERTS)
    ohf = jnp.where(oh, 1.0, 0.0)
    tr = lax.broadcasted_iota(I32, (PLAN_TB, PLAN_TB), 0)
    tcol = lax.broadcasted_iota(I32, (PLAN_TB, PLAN_TB), 1)
    tri = jnp.where(tcol < tr, 1.0, 0.0).astype(BF16)
    excl = _dot(tri, ohf.astype(BF16)) + carry_ref[...]
    mine = jnp.where(oh, excl, 0.0)
    r0 = jnp.sum(jnp.where(lane < N_EXPERTS, mine, 0.0), axis=-1, keepdims=True)
    r1 = jnp.sum(jnp.where(lane >= N_EXPERTS, mine, 0.0), axis=-1, keepdims=True)
    rank_ref[...] = jnp.where(lane == 0, r0, jnp.where(lane == 1, r1, 0.0))
    carry_ref[...] += jnp.sum(ohf, axis=0, keepdims=True)
    cnt_ref[...] = carry_ref[...]


def _moe_plan(ri):
    n = ri.shape[0]
    return pl.pallas_call(
        _plan_body,
        out_shape=(jax.ShapeDtypeStruct((n, LANES), F32), jax.ShapeDtypeStruct((1, LANES), F32)),
        grid=(n // PLAN_TB,),
        in_specs=[pl.BlockSpec((PLAN_TB, LANES), lambda i: (i, 0))],
        out_specs=(pl.BlockSpec((PLAN_TB, LANES), lambda i: (i, 0)),
                   pl.BlockSpec((1, LANES), lambda i: (0, 0))),
        scratch_shapes=[pltpu.VMEM((1, LANES), F32)],
        compiler_params=_cparams(("arbitrary",)),
        name="moe_plan",
    )(ri)


DISPATCH_TB = 2048


SUBLANES = 8


def _wait_rows(hbm_ref, nrows, sem):
    whole = hbm_ref.at[pl.ds(0, nrows)]
    pltpu.make_async_copy(whole, whole, sem).wait()


def _dispatch_body(pos_ref, h_ref, xs_in_ref, xs_ref, sem):
    del xs_in_ref
    base = pl.program_id(0) * DISPATCH_TB

    def issue(g, carry):
        for u in range(SUBLANES):
            tok = base + g * SUBLANES + u
            for k in range(2):
                pltpu.make_async_copy(h_ref.at[g, pl.ds(u, 1)],
                                      xs_ref.at[pl.ds(pos_ref[2 * tok + k], 1)], sem).start(priority=k)
        return carry

    lax.fori_loop(0, DISPATCH_TB // SUBLANES, issue, 0)

    for _ in range(2):
        _wait_rows(xs_ref, DISPATCH_TB, sem)


def _moe_dispatch(pos, hp):
    n, c = hp.shape
    xs0 = jnp.zeros((_moe_rows(n), c), U32)
    hp = hp.reshape(n // SUBLANES, SUBLANES, c)
    return pl.pallas_call(
        _dispatch_body,
        out_shape=jax.ShapeDtypeStruct(xs0.shape, U32),
        grid_spec=pltpu.PrefetchScalarGridSpec(
            num_scalar_prefetch=1,
            grid=(n // DISPATCH_TB,),
            in_specs=[pl.BlockSpec((DISPATCH_TB // SUBLANES, SUBLANES, c), lambda i, p: (i, 0, 0)),
                      pl.BlockSpec(memory_space=pl.ANY)],
            out_specs=pl.BlockSpec(memory_space=pl.ANY),
            scratch_shapes=[pltpu.SemaphoreType.DMA(())]),
        input_output_aliases={2: 0},
        compiler_params=_cparams(("arbitrary",)),
        name="moe_dispatch",
    )(pos, hp, xs0)


COMBINE_TB = 1024


def _combine_body(pos_ref, x_ref, rw_ref, ys_ref, o_ref, buf, sem):
    base = pl.program_id(0) * COMBINE_TB

    def issue(g, carry):
        for u in range(SUBLANES):
            tok = base + g * SUBLANES + u
            for k in range(2):
                pltpu.make_async_copy(ys_ref.at[pl.ds(pos_ref[2 * tok + k], 1)],
                                      buf.at[k, g, pl.ds(u, 1)], sem).start(priority=k)
        return carry

    lax.fori_loop(0, COMBINE_TB // SUBLANES, issue, 0)
    for _ in range(2):
        _wait_rows(ys_ref, COMBINE_TB, sem)
    rw = rw_ref[...]
    rows = lambda k: _unpack_bf16_pairs(buf[k].reshape(COMBINE_TB, buf.shape[-1]))
    o_ref[...] = x_ref[...] + rw[:, 0:1] * rows(0) + rw[:, 1:2] * rows(1)


def _moe_combine(pos, x1, rw, ys):
    n = x1.shape[0]
    c = ys.shape[1]
    return pl.pallas_call(
        _combine_body,
        out_shape=jax.ShapeDtypeStruct((n, D_MODEL), F32),
        grid_spec=pltpu.PrefetchScalarGridSpec(
            num_scalar_prefetch=1,
            grid=(n // COMBINE_TB,),
            in_specs=[pl.BlockSpec((COMBINE_TB, D_MODEL), lambda i, p: (i, 0)),
                      pl.BlockSpec((COMBINE_TB, LANES), lambda i, p: (i, 0)),
                      pl.BlockSpec(memory_space=pl.ANY)],
            out_specs=pl.BlockSpec((COMBINE_TB, D_MODEL), lambda i, p: (i, 0)),
            scratch_shapes=[pltpu.VMEM((2, COMBINE_TB // SUBLANES, SUBLANES, c), U32),
                            pltpu.SemaphoreType.DMA(())]),
        compiler_params=_cparams(("arbitrary",)),
        name="moe_combine",
    )(pos, x1, rw, ys)


MOE_TF = 1792


def _experts_body(te_ref, nv_ref, xs_ref, wg_ref, wu_ref, wd_ref, ys_ref, xb_ref, acc_ref):
    i = pl.program_id(0)
    f = pl.program_id(1)
    nf = pl.num_programs(1)

    @pl.when(i < nv_ref[0])
    def _():
        @pl.when(f == 0)
        def _():
            xb_ref[...] = _unpack_bf16_pairs(xs_ref[...]).astype(BF16)

        h = xb_ref[...]
        part = None
        for c, w in _col_chunks(MOE_TF, 1024):
            a = _dot(h, wg_ref[0, :, c:c + w])
            act = (a * _sigmoid(a) * _dot(h, wu_ref[0, :, c:c + w])).astype(BF16)
            pc = _dot(act, wd_ref[0, c:c + w, :])
            part = pc if part is None else part + pc

        @pl.when(f == 0)
        def _():
            acc_ref[...] = part

        @pl.when(f > 0)
        def _():
            acc_ref[...] += part

        @pl.when(f == nf - 1)
        def _():
            ys_ref[...] = _pack_bf16_pairs(acc_ref[...])

    @pl.when((i >= nv_ref[0]) & (f == nf - 1))
    def _():
        ys_ref[...] = jnp.zeros_like(ys_ref)


def _moe_experts(tile_expert, n_valid, xs, wg, wu, wd):
    rows, c = xs.shape
    nt = rows // MOE_TM
    nf = D_FF_EXPERT // MOE_TF

    def tile(i, nv):
        return jnp.minimum(i, nv[0] - 1)

    def fblk(i, f, nv):
        return jnp.where(i < nv[0], f, nf - 1)

    return pl.pallas_call(
        _experts_body,
        out_shape=jax.ShapeDtypeStruct((rows, c), U32),
        grid_spec=pltpu.PrefetchScalarGridSpec(
            num_scalar_prefetch=2,
            grid=(nt, nf),
            in_specs=[pl.BlockSpec((MOE_TM, c), lambda i, f, te, nv: (tile(i, nv), 0)),
                      pl.BlockSpec((1, D_MODEL, MOE_TF),
                                   lambda i, f, te, nv: (te[tile(i, nv)], 0, fblk(i, f, nv))),
                      pl.BlockSpec((1, D_MODEL, MOE_TF),
                                   lambda i, f, te, nv: (te[tile(i, nv)], 0, fblk(i, f, nv))),
                      pl.BlockSpec((1, MOE_TF, D_MODEL),
                                   lambda i, f, te, nv: (te[tile(i, nv)], fblk(i, f, nv), 0))],
            out_specs=pl.BlockSpec((MOE_TM, c), lambda i, f, te, nv: (i, 0)),
            scratch_shapes=[pltpu.VMEM((MOE_TM, D_MODEL), BF16), pltpu.VMEM((MOE_TM, D_MODEL), F32)]),
        compiler_params=_cparams(("arbitrary", "arbitrary")),
        name="moe_experts",
    )(tile_expert, n_valid, xs, wg, wu, wd)


def _positions_body(ri_ref, rank_ref, start_ref, pos_ref):
    ri = ri_ref[...]
    rank = rank_ref[...]
    lane = lax.broadcasted_iota(I32, ri.shape, 1)
    first = jnp.where(lane == ri[:, 0:1], start_ref[...], 0.0)
    second = jnp.where(lane == ri[:, 1:2] + N_EXPERTS, start_ref[...], 0.0)
    p0 = jnp.sum(first, axis=-1, keepdims=True) + rank[:, 0:1]
    p1 = jnp.sum(second, axis=-1, keepdims=True) + rank[:, 1:2]
    pos_ref[...] = jnp.where(lane == 0, p0, jnp.where(lane == 1, p1, 0.0)).astype(I32)


def _moe_positions(ri, rank, starts):
    n = ri.shape[0]
    row = lambda i: (i, 0)
    return pl.pallas_call(
        _positions_body,
        out_shape=jax.ShapeDtypeStruct((n, LANES), I32),
        grid=(n // PLAN_TB,),
        in_specs=[pl.BlockSpec((PLAN_TB, LANES), row), pl.BlockSpec((PLAN_TB, LANES), row),
                  pl.BlockSpec((1, LANES), lambda i: (0, 0))],
        out_specs=pl.BlockSpec((PLAN_TB, LANES), row),
        compiler_params=_cparams(("parallel",)),
        name="moe_positions",
    )(ri, rank, starts)


def _moe_ffn(x1, hp, ri, rw, wg, wu, wd):
    rank, cnt = _moe_plan(ri)
    c0 = cnt[0, :N_EXPERTS].astype(I32)
    c1 = cnt[0, N_EXPERTS:2 * N_EXPERTS].astype(I32)
    padded = ((c0 + c1 + MOE_TM - 1) // MOE_TM) * MOE_TM
    ends = jnp.cumsum(padded)
    off = ends - padded
    starts = jnp.zeros((1, LANES), F32).at[0, :2 * N_EXPERTS].set(
        jnp.concatenate([off, off + c0]).astype(F32))
    pos = _moe_positions(ri, rank, starts)[:, :2].reshape(-1)
    nt = _moe_rows(x1.shape[0]) // MOE_TM
    tile_start = jnp.arange(nt, dtype=I32) * MOE_TM
    tile_expert = jnp.minimum(jnp.sum(tile_start[:, None] >= ends[None, :], axis=1),
                              N_EXPERTS - 1).astype(I32)
    n_valid = (ends[-1:] // MOE_TM).astype(I32)
    xs = _moe_dispatch(pos, hp)
    ys = _moe_experts(tile_expert, n_valid, xs, wg, wu, wd)
    return _moe_combine(pos, x1, rw, ys)


def _head_consts():
    lane = jnp.arange(ATT_OUT)
    bd = jnp.where((lane[:, None] // ATT_HEAD_DIM) == (lane[None, :] // ATT_HEAD_DIM),
                   1.0 / ATT_HEAD_DIM, 0.0).astype(BF16)
    perm = (lane[:, None] == (lane[None, :] ^ (ATT_HEAD_DIM // 2))).astype(BF16)
    return bd, perm


def _head_gains(g):
    full = jnp.tile(g.astype(F32), ATT_SLOTS).reshape(1, ATT_OUT)
    half = ATT_HEAD_DIM // 2
    swapped = jnp.tile(jnp.concatenate([g[half:], g[:half]]).astype(F32), ATT_SLOTS).reshape(1, ATT_OUT)
    return full, swapped


def kernel(x, mem, positions, norm_mix, w_in, ssm_a_re, ssm_a_im, ssm_log_dt, ssm_b_re, ssm_b_im,
           ssm_c_re, ssm_c_im, ssm_d, ssm_w_glu, w_ssm_out, att_q_norm, att_k_norm, w_att_out,
           norm_mem, w_mem_kv, mem_q_norm, mem_k_norm, w_mem_out, w_o, norm_ffn, ffn_w_gate,
           ffn_w_up, ffn_w_down, moe_w_router, moe_w_gate, moe_w_up, moe_w_down):
    bsz, seq, d = x.shape
    ntok = bsz * seq
    depth = w_in.shape[0]
    cos_t, sin_t = _rope_tables(positions)
    bd, perm = _head_consts()
    mem2d = mem.reshape(bsz * MEM_LEN, d)
    x2d = x.reshape(ntok, d)
    flat = lambda a: a.reshape((-1,) + a.shape[2:])
    ssm_ops = _ssm_operators(flat(ssm_a_re), flat(ssm_a_im), flat(ssm_log_dt), flat(ssm_b_re),
                             flat(ssm_b_im), flat(ssm_c_re), flat(ssm_c_im), ssm_d)
    for i in range(depth):
        wi = w_in[i]
        g_mix = norm_mix[i].reshape(1, d)

        wut = jnp.transpose(wi[:, :SSM_WIDTH]).astype(BF16)
        ut, qkv, memq, gates = _inproj_main(x2d, g_mix, wut, wi.astype(BF16))

        yt = _ssm_scan(ut, ssm_ops, i, bsz, seq)

        gq, gqs = _head_gains(att_q_norm[i])
        gk, gks = _head_gains(att_k_norm[i])
        att = []
        for gi, (_, dil) in enumerate(DIL_PAIRS):
            att.extend(_dilated_attention_group(qkv, gi, cos_t, sin_t, gq, gqs, gk, gks, bd, perm,
                                                bsz, seq, dil))

        kmem, vmem = _memory_kv(mem2d, norm_mem[i].reshape(1, d), w_mem_kv[i].astype(BF16),
                                mem_k_norm[i].reshape(1, X_HEAD_DIM))

        wts = (jnp.transpose(ssm_w_glu[i]).astype(BF16), w_ssm_out[i].astype(BF16),
               w_att_out[i].astype(BF16), w_mem_out[i].astype(BF16), w_o[i].astype(BF16),
               mem_q_norm[i].reshape(1, X_HEAD_DIM), norm_ffn[i].reshape(1, d))
        j = i // 2
        if i % 2 == 0:
            x1, h2 = _merge(x2d, memq, gates, yt, att, kmem, vmem, wts, bsz, seq)
            x2d = _dense_ffn(h2, x1, ffn_w_gate[j].astype(BF16), ffn_w_up[j].astype(BF16),
                             ffn_w_down[j].astype(BF16))
        else:
            wr = jnp.zeros((d, LANES), F32).at[:, :N_EXPERTS].set(moe_w_router[j])
            wr_hi = wr.astype(BF16)
            wr_lo = (wr - wr_hi.astype(F32)).astype(BF16)
            x1, hp, ri, rw = _merge(x2d, memq, gates, yt, att, kmem, vmem, wts, bsz, seq,
                                    router=(wr_hi, wr_lo))
            x2d = _moe_ffn(x1, hp, ri, rw, moe_w_gate[j].astype(BF16), moe_w_up[j].astype(BF16),
                           moe_w_down[j].astype(BF16))
    return x2d.reshape(bsz, seq, d)
```
